```python
import jax, jax.numpy as jnp
from jax import lax
import numpy as np

D_MODEL = 4096
BATCH = 2
SEQ = 4096
DEPTH = 1

CTX_LEN = 256
GRID_W = 64
HG_HEADS = 16
HG_DIM = 128
HG_WIDTH = HG_HEADS * HG_DIM
RET_HEADS = 16
RET_DK = 128
RET_DV = 128
RET_QK_WIDTH = RET_HEADS * RET_DK
RET_V_WIDTH = RET_HEADS * RET_DV
RET_DECAY_EXP_MIN = 5.0
RET_DECAY_EXP_MAX = 12.0
D_FF = 4 * D_MODEL
CHUNK = 64
ROPE_BASE = 10000.0
EPS = 1e-6
ADALN_SCALE = 0.3
STATE_SPLITS = (HG_WIDTH, HG_WIDTH, HG_WIDTH, RET_QK_WIDTH, RET_V_WIDTH)
READ_SPLITS = (HG_WIDTH, HG_WIDTH, RET_QK_WIDTH, RET_V_WIDTH, D_MODEL, D_MODEL)
N_STATE_COLS = sum(STATE_SPLITS)
N_IN_COLS = N_STATE_COLS + sum(READ_SPLITS)

kernel_name = "hybrid_hgrn2_retention_dit_block"

F32 = jnp.float32


def _rmsnorm(x, gain):
    xf = x.astype(F32)
    y = xf * lax.rsqrt(jnp.mean(xf * xf, axis=-1, keepdims=True) + EPS)
    return (y * gain.astype(F32)).astype(x.dtype)


def _head_rmsnorm(o, gain):
    return o * lax.rsqrt(jnp.mean(o * o, axis=-1, keepdims=True) + EPS) * gain.astype(F32)


def _head_layernorm(o):
    mu = jnp.mean(o, axis=-1, keepdims=True)
    oc = o - mu
    return oc * lax.rsqrt(jnp.mean(oc * oc, axis=-1, keepdims=True) + EPS)


def _split_cols(a, sizes):
    out, off = [], 0
    for s in sizes:
        out.append(a[..., off:off + s])
        off += s
    return out


def _split_heads(a, n_heads):
    b, t, w = a.shape
    return a.reshape(b, t, n_heads, w // n_heads).transpose(0, 2, 1, 3)


def _merge_heads(a):
    b, h, t, d = a.shape
    return a.transpose(0, 2, 1, 3).reshape(b, t, h * d)


def _flip(a):
    return None if a is None else jnp.flip(a, axis=2)


def _rope_2d(t_len):
    rows = t_len // GRID_W
    pos = jnp.arange(rows * GRID_W)
    row = (pos // GRID_W).astype(F32)
    col = (pos % GRID_W).astype(F32)
    n_freq = RET_DK // 4
    inv_freq = ROPE_BASE ** (-jnp.arange(n_freq, dtype=F32) / n_freq)
    ang = jnp.concatenate([row[:, None] * inv_freq, col[:, None] * inv_freq], axis=-1)
    return jnp.cos(ang), jnp.sin(ang)


def _apply_rope(a, cos, sin):
    a1, a2 = a[..., 0::2], a[..., 1::2]
    return jnp.stack([a1 * cos - a2 * sin, a1 * sin + a2 * cos], axis=-1).reshape(a.shape)


def _chunk_scan(q, k, v, log_f, s0):
    b, h, t, _ = k.shape
    n = t // CHUNK

    def to_chunks(a):
        return jnp.moveaxis(a.astype(F32).reshape(b, h, n, CHUNK, a.shape[-1]), 2, 0)

    kc, vc, fc = to_chunks(k), to_chunks(v), to_chunks(log_f)

    def update(state, ki, vi, cum):
        last = cum[..., -1:, :]
        k_dec = ki * jnp.exp(last - cum)
        return jnp.exp(last[..., 0, :])[..., None] * state + jnp.einsum('bhck,bhcv->bhkv', k_dec, vi)

    if q is None:
        def step_state(state, inp):
            ki, vi, fi = inp
            return update(state, ki, vi, jnp.cumsum(fi, axis=-2)), None
        s_final, _ = lax.scan(step_state, s0, (kc, vc, fc))
        return None, s_final

    qc = to_chunks(q)
    mask = jnp.tril(jnp.ones((CHUNK, CHUNK), dtype=bool))[:, :, None]

    def step(state, inp):
        qi, ki, vi, fi = inp
        cum = jnp.cumsum(fi, axis=-2)
        diff = jnp.where(mask, cum[..., :, None, :] - cum[..., None, :, :], -jnp.inf)
        decay = jnp.exp(diff)
        if fi.shape[-1] == 1:
            scores = jnp.einsum('bhtk,bhsk->bhts', qi, ki) * decay[..., 0]
        else:
            scores = jnp.einsum('bhtk,bhsk,bhtsk->bhts', qi, ki, decay)
        o = (jnp.einsum('bhck,bhkv->bhcv', qi * jnp.exp(cum), state)
             + jnp.einsum('bhts,bhsv->bhtv', scores, vi))
        return update(state, ki, vi, cum), o

    s_final, oc = lax.scan(step, s0, (qc, kc, vc, fc))
    return jnp.moveaxis(oc, 0, 2).reshape(b, h, t, -1), s_final


def _prefix_bidirectional_scan(q_ctx, k_ctx, v_ctx, lf_ctx, q_lat, k_lat, v_lat, lf_lat):
    b, h, _, dk = k_lat[0].shape
    dv = v_lat.shape[-1]
    zero = jnp.zeros((b, h, dk, dv), F32)
    oc_f, sc_f = _chunk_scan(q_ctx, k_ctx[0], v_ctx, lf_ctx[0], zero)
    oc_b, sc_b = _chunk_scan(_flip(q_ctx), _flip(k_ctx[1]), _flip(v_ctx), _flip(lf_ctx[1]), zero)
    ox_f, _ = _chunk_scan(q_lat, k_lat[0], v_lat, lf_lat[0], sc_f)
    ox_b, _ = _chunk_scan(_flip(q_lat), _flip(k_lat[1]), _flip(v_lat), _flip(lf_lat[1]), sc_b)
    o_ctx = None if q_ctx is None else oc_f + _flip(oc_b)
    return o_ctx, ox_f + _flip(ox_b)


def _state_inputs(p, lb, ret_log_decay, rope):
    hg_i, hg_f_fwd, hg_f_bwd, ret_k, ret_v = _split_cols(p, STATE_SPLITS)
    b, t, _ = p.shape

    def hgrn_gates(z, lbd):
        f = lbd + (1.0 - lbd) * jax.nn.sigmoid(z.astype(F32))
        return _split_heads(1.0 - f, HG_HEADS), _split_heads(jnp.log(f), HG_HEADS)

    k_f, lf_f = hgrn_gates(hg_f_fwd, lb[0])
    k_b, lf_b = hgrn_gates(hg_f_bwd, lb[1])
    rk = _split_heads(ret_k.astype(F32), RET_HEADS) * RET_DK ** -0.5
    if rope is not None:
        rk = _apply_rope(rk, *rope)
    rlf = [jnp.broadcast_to(ret_log_decay[d][None, :, None, None], (b, RET_HEADS, t, 1)) for d in range(2)]
    return dict(hg_i=_split_heads(hg_i, HG_HEADS), hg_k=(k_f, k_b), hg_lf=(lf_f, lf_b),
                ret_k=(rk, rk), ret_v=_split_heads(ret_v, RET_HEADS), ret_lf=(rlf[0], rlf[1]))


def _read_inputs(p, rope):
    hg_q, hg_g, ret_q, ret_g, gate_hgrn, gate_ret = _split_cols(p, READ_SPLITS)
    q_h = _split_heads(jax.nn.silu(hg_q.astype(F32)) * HG_DIM ** -0.5, HG_HEADS)
    q_r = _split_heads(ret_q.astype(F32), RET_HEADS)
    if rope is not None:
        q_r = _apply_rope(q_r, *rope)
    return dict(hg_q=q_h, hg_g=hg_g, ret_q=q_r, ret_g=ret_g, gate_hgrn=gate_hgrn, gate_ret=gate_ret)


def _mixer_output(hg_o, ret_o, rd, hg_gain, w_bh, w_br, w_o, dtype):
    y_hg = _merge_heads(_head_rmsnorm(hg_o, hg_gain)).astype(dtype) * jax.nn.silu(rd['hg_g'])
    y_ret = _merge_heads(_head_layernorm(ret_o)).astype(dtype) * jax.nn.silu(rd['ret_g'])
    merged = (jax.nn.sigmoid(rd['gate_hgrn']) * (y_hg @ w_bh)
              + jax.nn.sigmoid(rd['gate_ret']) * (y_ret @ w_br))
    return merged @ w_o


def _sq_relu_mlp(h, w1, w2):
    return jnp.square(jax.nn.relu(h @ w1)) @ w2


def setup_inputs(seed: int = 0) -> dict:
    key = jax.random.key(seed)
    ks = jax.random.split(key, 18)

    def nrm(k, shape, scale):
        return jax.random.normal(k, shape, F32) * scale

    e = np.linspace(RET_DECAY_EXP_MIN, RET_DECAY_EXP_MAX, RET_HEADS).astype(np.float32)
    base_logit = jnp.asarray(np.log(np.exp2(e) - 1.0).astype(np.float32))
    return {
        "x": nrm(ks[0], (BATCH, SEQ, D_MODEL), 1.0),
        "c": nrm(ks[1], (BATCH, D_MODEL), 1.0),
        "ctx": nrm(ks[2], (BATCH, CTX_LEN, D_MODEL), 1.0),
        "c_ctx": nrm(ks[3], (D_MODEL,), 1.0),
        "w_mod": nrm(ks[4], (DEPTH, D_MODEL, 6 * D_MODEL), ADALN_SCALE * D_MODEL ** -0.5),
        "b_mod": nrm(ks[5], (DEPTH, 6 * D_MODEL), 0.02),
        "norm1_g": 1.0 + nrm(ks[6], (DEPTH, D_MODEL), 0.02),
        "norm2_g": 1.0 + nrm(ks[7], (DEPTH, D_MODEL), 0.02),
        "w_in": nrm(ks[8], (DEPTH, D_MODEL, N_IN_COLS), D_MODEL ** -0.5),
        "hg_lb_logits": nrm(ks[9], (DEPTH + 1, 2, HG_WIDTH), 0.1),
        "hg_norm_g": 1.0 + nrm(ks[10], (DEPTH, HG_DIM), 0.02),
        "ret_decay_logit": base_logit + nrm(ks[11], (DEPTH, 2, RET_HEADS), 0.1),
        "w_branch_hgrn": nrm(ks[12], (DEPTH, HG_WIDTH, D_MODEL), HG_WIDTH ** -0.5),
        "w_branch_ret": nrm(ks[13], (DEPTH, RET_V_WIDTH, D_MODEL), RET_V_WIDTH ** -0.5),
        "w_out": nrm(ks[14], (DEPTH, D_MODEL, D_MODEL), D_MODEL ** -0.5),
        "w_ff1": nrm(ks[15], (DEPTH, D_MODEL, D_FF), D_MODEL ** -0.5),
        "w_ff2": nrm(ks[16], (DEPTH, D_FF, D_MODEL), D_FF ** -0.5),
        "final_norm_g": 1.0 + nrm(ks[17], (D_MODEL,), 0.02),
    }


def reference(x, c, ctx, c_ctx, w_mod, b_mod, norm1_g, norm2_g, w_in, hg_lb_logits, hg_norm_g,
              ret_decay_logit, w_branch_hgrn, w_branch_ret, w_out, w_ff1, w_ff2, final_norm_g):
    rope = _rope_2d(x.shape[1])
    lower_bounds = jnp.cumsum(jax.nn.softmax(hg_lb_logits.astype(F32), axis=0), axis=0)[:DEPTH]
    ret_log_decay = jax.nn.log_sigmoid(ret_decay_logit.astype(F32))
    silu_c = jax.nn.silu(c)
    silu_cc = jax.nn.silu(c_ctx)

    for layer in range(DEPTH):
        keep_ctx = layer + 1 < DEPTH
        mod_x = (silu_c @ w_mod[layer] + b_mod[layer])[:, None, :]
        mod_c = silu_cc @ w_mod[layer] + b_mod[layer]
        shift1_x, scale1_x, gate1_x, shift2_x, scale2_x, gate2_x = jnp.split(mod_x, 6, axis=-1)
        shift1_c, scale1_c, gate1_c, shift2_c, scale2_c, gate2_c = jnp.split(mod_c, 6, axis=-1)

        h_x = _rmsnorm(x, norm1_g[layer]) * (1.0 + scale1_x) + shift1_x
        h_c = _rmsnorm(ctx, norm1_g[layer]) * (1.0 + scale1_c) + shift1_c
        p_x = h_x @ w_in[layer]
        p_c = h_c @ (w_in[layer] if keep_ctx else w_in[layer][:, :N_STATE_COLS])
        st_x = _state_inputs(p_x[..., :N_STATE_COLS], lower_bounds[layer], ret_log_decay[layer], rope)
        st_c = _state_inputs(p_c[..., :N_STATE_COLS], lower_bounds[layer], ret_log_decay[layer], None)
        rd_x = _read_inputs(p_x[..., N_STATE_COLS:], rope)
        rd_c = _read_inputs(p_c[..., N_STATE_COLS:], None) if keep_ctx else None

        hg_ctx, hg_lat = _prefix_bidirectional_scan(
            None if rd_c is None else rd_c['hg_q'], st_c['hg_k'], st_c['hg_i'], st_c['hg_lf'],
            rd_x['hg_q'], st_x['hg_k'], st_x['hg_i'], st_x['hg_lf'])
        ret_ctx, ret_lat = _prefix_bidirectional_scan(
            None if rd_c is None else rd_c['ret_q'], st_c['ret_k'], st_c['ret_v'], st_c['ret_lf'],
            rd_x['ret_q'], st_x['ret_k'], st_x['ret_v'], st_x['ret_lf'])

        out_x = _mixer_output(hg_lat, ret_lat, rd_x, hg_norm_g[layer], w_branch_hgrn[layer],
                              w_branch_ret[layer], w_out[layer], x.dtype)
        x = x + gate1_x * out_x
        h2_x = _rmsnorm(x, norm2_g[layer]) * (1.0 + scale2_x) + shift2_x
        x = x + gate2_x * _sq_relu_mlp(h2_x, w_ff1[layer], w_ff2[layer])

        if keep_ctx:
            out_c = _mixer_output(hg_ctx, ret_ctx, rd_c, hg_norm_g[layer], w_branch_hgrn[layer],
                                  w_branch_ret[layer], w_out[layer], ctx.dtype)
            ctx = ctx + gate1_c * out_c
            h2_c = _rmsnorm(ctx, norm2_g[layer]) * (1.0 + scale2_c) + shift2_c
            ctx = ctx + gate2_c * _sq_relu_mlp(h2_c, w_ff1[layer], w_ff2[layer])

    return _rmsnorm(x, final_norm_g)
```

```python
import functools

import jax
import jax.numpy as jnp
from jax import lax
from jax.experimental import pallas as pl
from jax.experimental.pallas import tpu as pltpu

F32 = jnp.float32
BF16 = jnp.bfloat16

LANES = 128
SUBLANES = 8
VMEM_LIMIT = 56 * 2 ** 20
EPS = 1e-6
ROPE_BASE = 10000.0
GRID_W = 64
HG_CHUNK = 64
RET_CHUNK = 256


def _params(*sem):
    return pltpu.CompilerParams(dimension_semantics=sem, vmem_limit_bytes=VMEM_LIMIT)


def _tile(n, target, unit=LANES):
    t = min(n, target) // unit * unit
    while n % t:
        t -= unit
    return t


def _silu(z):
    return z * jax.nn.sigmoid(z)


def _dot(a, b):
    return jnp.dot(a, b, preferred_element_type=F32)


def _dot_nt(a, b):
    return lax.dot_general(a, b, (((1,), (1,)), ((), ())), preferred_element_type=F32)


def _dot_tn(a, b):
    return lax.dot_general(a, b, (((0,), (0,)), ((), ())), preferred_element_type=F32)


def _mod_kernel(c_ref, w_ref, b_ref, o_ref):
    s = _silu(c_ref[...]).astype(BF16)
    o_ref[...] = _dot(s, w_ref[...].astype(BF16)) + b_ref[...]


def _mod_call(c_rows, w_mod, b_mod):
    r, d = c_rows.shape
    n = w_mod.shape[1]
    tn = _tile(n, 512)
    return pl.pallas_call(
        _mod_kernel,
        grid=(n // tn,),
        in_specs=[pl.BlockSpec((r, d), lambda j: (0, 0)),
                  pl.BlockSpec((d, tn), lambda j: (0, j)),
                  pl.BlockSpec((1, tn), lambda j: (0, j))],
        out_specs=pl.BlockSpec((r, tn), lambda j: (0, j)),
        out_shape=jax.ShapeDtypeStruct((r, n), F32),
        compiler_params=_params("parallel"),
        name="mod",
    )(c_rows, w_mod, b_mod.reshape(1, n))


def _norm_mod_kernel(x_ref, g_ref, scale_ref, shift_ref, o_ref):
    x = x_ref[0]
    y = x * lax.rsqrt(jnp.mean(x * x, axis=-1, keepdims=True) + EPS) * g_ref[...]
    o_ref[0] = (y * (1.0 + scale_ref[0]) + shift_ref[0]).astype(o_ref.dtype)


def _norm_mod_call(x, gain, mod3, row_of_batch, shift_blk, scale_blk):
    b, t, d = x.shape
    tm = _tile(t, 256, SUBLANES)
    return pl.pallas_call(
        _norm_mod_kernel,
        grid=(b, t // tm),
        in_specs=[pl.BlockSpec((1, tm, d), lambda bi, i: (bi, i, 0)),
                  pl.BlockSpec((1, d), lambda bi, i: (0, 0)),
                  pl.BlockSpec((1, 1, d), lambda bi, i: (row_of_batch(bi), 0, scale_blk)),
                  pl.BlockSpec((1, 1, d), lambda bi, i: (row_of_batch(bi), 0, shift_blk))],
        out_specs=pl.BlockSpec((1, tm, d), lambda bi, i: (bi, i, 0)),
        out_shape=jax.ShapeDtypeStruct((b, t, d), BF16),
        compiler_params=_params("parallel", "parallel"),
        name="norm_mod",
    )(x, gain.reshape(1, d), mod3, mod3)


def _final_norm_kernel(x_ref, g_ref, o_ref):
    x = x_ref[0]
    o_ref[0] = x * lax.rsqrt(jnp.mean(x * x, axis=-1, keepdims=True) + EPS) * g_ref[...]


def _final_norm_call(x, gain):
    b, t, d = x.shape
    tm = _tile(t, 256, SUBLANES)
    return pl.pallas_call(
        _final_norm_kernel,
        grid=(b, t // tm),
        in_specs=[pl.BlockSpec((1, tm, d), lambda bi, i: (bi, i, 0)),
                  pl.BlockSpec((1, d), lambda bi, i: (0, 0))],
        out_specs=pl.BlockSpec((1, tm, d), lambda bi, i: (bi, i, 0)),
        out_shape=jax.ShapeDtypeStruct((b, t, d), F32),
        compiler_params=_params("parallel", "parallel"),
        name="final_norm",
    )(x, gain.reshape(1, d))


def _in_proj_kernel(h_ref, w_ref, o_ref):
    acc = _dot(h_ref[0], w_ref[...])
    for j in range(o_ref.shape[1]):
        o_ref[0, j] = acc[:, j * LANES:(j + 1) * LANES]


def _in_proj_call(h, w, n):
    b, t, d = h.shape
    tm = _tile(t, 1024, SUBLANES)
    tn = _tile(n, 512)
    return pl.pallas_call(
        _in_proj_kernel,
        grid=(b, t // tm, n // tn),
        in_specs=[pl.BlockSpec((1, tm, d), lambda bi, i, j: (bi, i, 0)),
                  pl.BlockSpec((d, tn), lambda bi, i, j: (0, j))],
        out_specs=pl.BlockSpec((1, tn // LANES, tm, LANES), lambda bi, i, j: (bi, j, i, 0)),
        out_shape=jax.ShapeDtypeStruct((b, n // LANES, t, LANES), F32),
        compiler_params=_params("parallel", "parallel", "arbitrary"),
        name="in_proj",
    )(h, w)


def _cumsum_rows(x, reverse):
    n = x.shape[0]
    row = lax.broadcasted_iota(jnp.int32, x.shape, 0)
    d = 1
    while d < n:
        if reverse:
            x = x + jnp.where(row < n - d, pltpu.roll(x, n - d, axis=0), 0.0)
        else:
            x = x + jnp.where(row >= d, pltpu.roll(x, d, axis=0), 0.0)
        d *= 2
    return x


def _block_row_bcast(x, block, offset):
    n, w = x.shape
    if block >= SUBLANES:
        xb = x.reshape(n // block, block, w)
        return jnp.broadcast_to(xb[:, offset:offset + 1, :], xb.shape).reshape(n, w)
    m = lax.broadcasted_iota(jnp.int32, x.shape, 0) & (block - 1)
    out = x
    for mm in range(block):
        if mm != offset:
            out = jnp.where(m == mm, pltpu.roll(x, (mm - offset) % n, axis=0), out)
    return out


def _hgrn_gates(z, lb):
    f = lb + (1.0 - lb) * jax.nn.sigmoid(z)
    return 1.0 - f, jnp.log(f)


def _state_update(s_ref, k, cum, v, reverse):
    n = cum.shape[0]
    edge = cum[0:1, :] if reverse else cum[n - 1:n, :]
    k_dec = (k * jnp.exp(edge - cum)).astype(BF16)
    s_ref[...] = s_ref[...] * jnp.exp(edge) + _dot_tn(v.astype(BF16), k_dec)


def _hgrn_kernel(i_ref, ff_ref, fb_ref, q_ref, g_ref, ci_ref, cff_ref, cfb_ref, lbl_ref, ng_ref, y_ref,
                 kf_s, kb_s, cf_s, cb_s, st_s, sf_s, sb_s):
    t_len, tc_len = i_ref.shape[2], ci_ref.shape[2]
    ch = HG_CHUNK
    cc = min(ch, tc_len)
    n, nc = t_len // ch, tc_len // cc

    l0, l1 = lbl_ref[0], lbl_ref[1]
    mx = jnp.maximum(l0, l1)
    e0, e1 = jnp.exp(l0 - mx), jnp.exp(l1 - mx)
    lb = e0 / (e0 + e1)
    lb_f, lb_b = lb[0:1], lb[1:2]

    sf_s[...] = jnp.zeros_like(sf_s)
    sb_s[...] = jnp.zeros_like(sb_s)

    def ctx_body(c, carry):
        rf = pl.multiple_of(c * cc, cc)
        rb = pl.multiple_of((nc - 1 - c) * cc, cc)
        k, lf = _hgrn_gates(cff_ref[0, 0, pl.ds(rf, cc), :], lb_f)
        _state_update(sf_s, k, _cumsum_rows(lf, False), ci_ref[0, 0, pl.ds(rf, cc), :], False)
        k, lf = _hgrn_gates(cfb_ref[0, 0, pl.ds(rb, cc), :], lb_b)
        _state_update(sb_s, k, _cumsum_rows(lf, True), ci_ref[0, 0, pl.ds(rb, cc), :], True)
        return carry

    lax.fori_loop(0, nc, ctx_body, 0)

    def state_body(c, carry):
        cb = n - 1 - c
        rf = pl.multiple_of(c * ch, ch)
        rb = pl.multiple_of(cb * ch, ch)
        k, lf = _hgrn_gates(ff_ref[0, 0, pl.ds(rf, ch), :], lb_f)
        cum = _cumsum_rows(lf, False)
        kf_s[pl.ds(rf, ch), :] = k
        cf_s[pl.ds(rf, ch), :] = cum
        st_s[c, :, 0:LANES] = sf_s[...].astype(BF16)
        _state_update(sf_s, k, cum, i_ref[0, 0, pl.ds(rf, ch), :], False)
        k, lf = _hgrn_gates(fb_ref[0, 0, pl.ds(rb, ch), :], lb_b)
        cum = _cumsum_rows(lf, True)
        kb_s[pl.ds(rb, ch), :] = k
        cb_s[pl.ds(rb, ch), :] = cum
        st_s[cb, :, LANES:2 * LANES] = sb_s[...].astype(BF16)
        _state_update(sb_s, k, cum, i_ref[0, 0, pl.ds(rb, ch), :], True)
        return carry

    lax.fori_loop(0, n, state_body, 0)

    pair_xor = (lax.broadcasted_iota(jnp.int32, (ch, ch), 0) ^ lax.broadcasted_iota(jnp.int32, (ch, ch), 1))
    row = lax.broadcasted_iota(jnp.int32, (ch, LANES), 0)
    q_scale = LANES ** -0.5

    def out_body(c, carry):
        r = pl.multiple_of(c * ch, ch)
        q = _silu(q_ref[0, 0, pl.ds(r, ch), :]) * q_scale
        kf, kb = kf_s[pl.ds(r, ch), :], kb_s[pl.ds(r, ch), :]
        cf, cb = cf_s[pl.ds(r, ch), :], cb_s[pl.ds(r, ch), :]
        v = i_ref[0, 0, pl.ds(r, ch), :].astype(BF16)
        a = jnp.where(pair_xor == 0, _dot_nt(q.astype(BF16), (kf + kb).astype(BF16)), 0.0)
        h, log_h = 1, 0
        while h < ch:
            second = (row & h) != 0
            df = cf - _block_row_bcast(cf, 2 * h, h - 1)
            db = cb - _block_row_bcast(cb, 2 * h, h)
            qm = q * jnp.exp(jnp.where(second, df, db))
            km = jnp.where(second, kb, kf) * jnp.exp(-jnp.where(second, db, df))
            a = jnp.where((pair_xor >> log_h) == 1, _dot_nt(qm.astype(BF16), km.astype(BF16)), a)
            h, log_h = 2 * h, log_h + 1
        q_in = jnp.concatenate([q * jnp.exp(cf), q * jnp.exp(cb)], axis=1).astype(BF16)
        o = _dot(a.astype(BF16), v) + _dot_nt(q_in, st_s[c])
        y = o * lax.rsqrt(jnp.mean(o * o, axis=-1, keepdims=True) + EPS) * ng_ref[...]
        y_ref[0, pl.ds(r, ch), :] = (y * _silu(g_ref[0, 0, pl.ds(r, ch), :])).astype(y_ref.dtype)
        return carry

    lax.fori_loop(0, n, out_body, 0)


def _hgrn_call(p_x, p_c, lb_logits, norm_g, heads):
    b, _, t, _ = p_x.shape
    tc = p_c.shape[2]

    def col(off):
        return pl.BlockSpec((1, 1, t, LANES), lambda bi, h: (bi, off * heads + h, 0, 0))

    def ctx_col(off):
        return pl.BlockSpec((1, 1, tc, LANES), lambda bi, h: (bi, off * heads + h, 0, 0))

    return pl.pallas_call(
        _hgrn_kernel,
        grid=(b, heads),
        in_specs=[col(0), col(1), col(2), col(5), col(6), ctx_col(0), ctx_col(1), ctx_col(2),
                  pl.BlockSpec((2, 2, LANES), lambda bi, h: (0, 0, h)),
                  pl.BlockSpec((1, LANES), lambda bi, h: (0, 0))],
        out_specs=pl.BlockSpec((1, t, LANES), lambda bi, h: (bi, 0, h)),
        out_shape=jax.ShapeDtypeStruct((b, t, heads * LANES), BF16),
        scratch_shapes=[pltpu.VMEM((t, LANES), F32), pltpu.VMEM((t, LANES), F32),
                        pltpu.VMEM((t, LANES), F32), pltpu.VMEM((t, LANES), F32),
                        pltpu.VMEM((t // HG_CHUNK, LANES, 2 * LANES), BF16),
                        pltpu.VMEM((LANES, LANES), F32), pltpu.VMEM((LANES, LANES), F32)],
        compiler_params=_params("parallel", "parallel"),
        name="hgrn",
    )(p_x, p_x, p_x, p_x, p_x, p_c, p_c, p_c, lb_logits, norm_g.reshape(1, LANES))


def _rope(a, cos, sin_signed):
    lane = lax.broadcasted_iota(jnp.int32, a.shape, 1)
    partner = jnp.where((lane & 1) == 0, pltpu.roll(a, LANES - 1, axis=1), pltpu.roll(a, 1, axis=1))
    return a * cos + partner * sin_signed


def _ret_state_update(s_ref, k, v, k_decay, s_decay):
    s_ref[...] = s_ref[...] * s_decay + _dot_tn(v.astype(BF16), (k * k_decay).astype(BF16))


def _ret_kernel(q_ref, k_ref, v_ref, g_ref, ck_ref, cv_ref, dl_ref, cos_ref, sin_ref, y_ref,
                kr_s, st_s, sf_s, sb_s):
    t_len, tc_len = q_ref.shape[2], ck_ref.shape[2]
    ch = RET_CHUNK
    cc = min(ch, tc_len)
    n, nc = t_len // ch, tc_len // cc
    k_scale = LANES ** -0.5

    dl = dl_ref[0]
    gam = jnp.minimum(dl, 0.0) - jnp.log1p(jnp.exp(-jnp.abs(dl)))
    gam_f, gam_b = gam[0:1], gam[1:2]

    def decays(m):
        pos = lax.broadcasted_iota(jnp.int32, (m, LANES), 0).astype(F32)
        return dict(kf=jnp.exp(gam_f * (m - 1.0 - pos)), kb=jnp.exp(gam_b * pos),
                    qf=jnp.exp(gam_f * (pos + 1.0)), qb=jnp.exp(gam_b * (m - pos)),
                    sf=jnp.exp(gam_f * m), sb=jnp.exp(gam_b * m))

    sf_s[...] = jnp.zeros_like(sf_s)
    sb_s[...] = jnp.zeros_like(sb_s)

    dc = decays(cc)

    def ctx_body(c, carry):
        rf = pl.multiple_of(c * cc, cc)
        rb = pl.multiple_of((nc - 1 - c) * cc, cc)
        _ret_state_update(sf_s, ck_ref[0, 0, pl.ds(rf, cc), :] * k_scale, cv_ref[0, 0, pl.ds(rf, cc), :],
                          dc["kf"], dc["sf"])
        _ret_state_update(sb_s, ck_ref[0, 0, pl.ds(rb, cc), :] * k_scale, cv_ref[0, 0, pl.ds(rb, cc), :],
                          dc["kb"], dc["sb"])
        return carry

    lax.fori_loop(0, nc, ctx_body, 0)

    def rope_body(c, carry):
        r = pl.multiple_of(c * ch, ch)
        kr_s[pl.ds(r, ch), :] = _rope(k_ref[0, 0, pl.ds(r, ch), :] * k_scale,
                                      cos_ref[pl.ds(r, ch), :], sin_ref[pl.ds(r, ch), :])
        return carry

    lax.fori_loop(0, n, rope_body, 0)

    dd = decays(ch)

    def state_body(c, carry):
        cb = n - 1 - c
        rf = pl.multiple_of(c * ch, ch)
        rb = pl.multiple_of(cb * ch, ch)
        st_s[c, :, 0:LANES] = sf_s[...].astype(BF16)
        _ret_state_update(sf_s, kr_s[pl.ds(rf, ch), :], v_ref[0, 0, pl.ds(rf, ch), :], dd["kf"], dd["sf"])
        st_s[cb, :, LANES:2 * LANES] = sb_s[...].astype(BF16)
        _ret_state_update(sb_s, kr_s[pl.ds(rb, ch), :], v_ref[0, 0, pl.ds(rb, ch), :], dd["kb"], dd["sb"])
        return carry

    lax.fori_loop(0, n, state_body, 0)

    ti = lax.broadcasted_iota(jnp.int32, (ch, ch), 0)
    si = lax.broadcasted_iota(jnp.int32, (ch, ch), 1)
    lag = (ti - si).astype(F32)
    g_f = jnp.broadcast_to(gam_f[:, 0:1], (ch, ch))
    g_b = jnp.broadcast_to(gam_b[:, 0:1], (ch, ch))
    decay = jnp.where(ti > si, jnp.exp(g_f * jnp.maximum(lag, 0.0)),
                      jnp.where(ti < si, jnp.exp(g_b * jnp.maximum(-lag, 0.0)), 2.0))

    def out_body(c, carry):
        r = pl.multiple_of(c * ch, ch)
        q = _rope(q_ref[0, 0, pl.ds(r, ch), :], cos_ref[pl.ds(r, ch), :], sin_ref[pl.ds(r, ch), :])
        k = kr_s[pl.ds(r, ch), :].astype(BF16)
        v = v_ref[0, 0, pl.ds(r, ch), :].astype(BF16)
        a = _dot_nt(q.astype(BF16), k) * decay
        q_in = jnp.concatenate([q * dd["qf"], q * dd["qb"]], axis=1).astype(BF16)
        o = _dot(a.astype(BF16), v) + _dot_nt(q_in, st_s[c])
        oc = o - jnp.mean(o, axis=-1, keepdims=True)
        y = oc * lax.rsqrt(jnp.mean(oc * oc, axis=-1, keepdims=True) + EPS)
        y_ref[0, pl.ds(r, ch), :] = (y * _silu(g_ref[0, 0, pl.ds(r, ch), :])).astype(y_ref.dtype)
        return carry

    lax.fori_loop(0, n, out_body, 0)


def _ret_call(p_x, p_c, decay_logit, cos, sin_signed, heads):
    b, _, t, _ = p_x.shape
    tc = p_c.shape[2]

    def col(off):
        return pl.BlockSpec((1, 1, t, LANES), lambda bi, h: (bi, off * heads + h, 0, 0))

    def ctx_col(off):
        return pl.BlockSpec((1, 1, tc, LANES), lambda bi, h: (bi, off * heads + h, 0, 0))

    return pl.pallas_call(
        _ret_kernel,
        grid=(b, heads),
        in_specs=[col(7), col(3), col(4), col(8), ctx_col(3), ctx_col(4),
                  pl.BlockSpec((1, 2, LANES), lambda bi, h: (h, 0, 0)),
                  pl.BlockSpec((t, LANES), lambda bi, h: (0, 0)),
                  pl.BlockSpec((t, LANES), lambda bi, h: (0, 0))],
        out_specs=pl.BlockSpec((1, t, LANES), lambda bi, h: (bi, 0, h)),
        out_shape=jax.ShapeDtypeStruct((b, t, heads * LANES), BF16),
        scratch_shapes=[pltpu.VMEM((t, LANES), F32),
                        pltpu.VMEM((t // RET_CHUNK, LANES, 2 * LANES), BF16),
                        pltpu.VMEM((LANES, LANES), F32), pltpu.VMEM((LANES, LANES), F32)],
        compiler_params=_params("parallel", "parallel"),
        name="ret",
    )(p_x, p_x, p_x, p_x, p_c, p_c, decay_logit, cos, sin_signed)


def _merge_kernel(yh_ref, yr_ref, wh_ref, wr_ref, gh_ref, gr_ref, o_ref):
    acc_h = _dot(yh_ref[0], wh_ref[...])
    acc_r = _dot(yr_ref[0], wr_ref[...])
    for j in range(gh_ref.shape[1]):
        sl = slice(j * LANES, (j + 1) * LANES)
        o_ref[0, :, sl] = (jax.nn.sigmoid(gh_ref[0, j]) * acc_h[:, sl]
                           + jax.nn.sigmoid(gr_ref[0, j]) * acc_r[:, sl]).astype(o_ref.dtype)


def _merge_call(y_hg, y_ret, w_bh, w_br, p_x, gate_h_blk, gate_r_blk):
    b, t, kw = y_hg.shape
    d = w_bh.shape[1]
    tm = _tile(t, 1024, SUBLANES)
    tn = _tile(d, 512)
    nb = tn // LANES
    return pl.pallas_call(
        _merge_kernel,
        grid=(b, t // tm, d // tn),
        in_specs=[pl.BlockSpec((1, tm, kw), lambda bi, i, j: (bi, i, 0)),
                  pl.BlockSpec((1, tm, kw), lambda bi, i, j: (bi, i, 0)),
                  pl.BlockSpec((kw, tn), lambda bi, i, j: (0, j)),
                  pl.BlockSpec((kw, tn), lambda bi, i, j: (0, j)),
                  pl.BlockSpec((1, nb, tm, LANES), lambda bi, i, j: (bi, gate_h_blk // nb + j, i, 0)),
                  pl.BlockSpec((1, nb, tm, LANES), lambda bi, i, j: (bi, gate_r_blk // nb + j, i, 0))],
        out_specs=pl.BlockSpec((1, tm, tn), lambda bi, i, j: (bi, i, j)),
        out_shape=jax.ShapeDtypeStruct((b, t, d), BF16),
        compiler_params=_params("parallel", "parallel", "arbitrary"),
        name="merge",
    )(y_hg, y_ret, w_bh, w_br, p_x, p_x)


def _gated_residual_kernel(a_ref, w_ref, x_ref, gate_ref, o_ref):
    o_ref[0] = x_ref[0] + gate_ref[0] * _dot(a_ref[0], w_ref[...])


def _out_proj_call(a, w, x, mod3, gate_blk_of):
    b, t, kw = a.shape
    d = w.shape[1]
    tm = _tile(t, 1024, SUBLANES)
    tn = _tile(d, 512)
    return pl.pallas_call(
        _gated_residual_kernel,
        grid=(b, t // tm, d // tn),
        in_specs=[pl.BlockSpec((1, tm, kw), lambda bi, i, j: (bi, i, 0)),
                  pl.BlockSpec((kw, tn), lambda bi, i, j: (0, j)),
                  pl.BlockSpec((1, tm, tn), lambda bi, i, j: (bi, i, j)),
                  pl.BlockSpec((1, 1, tn), lambda bi, i, j: (bi, 0, gate_blk_of(tn) + j))],
        out_specs=pl.BlockSpec((1, tm, tn), lambda bi, i, j: (bi, i, j)),
        out_shape=jax.ShapeDtypeStruct((b, t, d), F32),
        compiler_params=_params("parallel", "parallel", "arbitrary"),
        name="out_proj",
    )(a, w, x, mod3)


def _ff1_kernel(h_ref, w_ref, o_ref):
    z = jnp.maximum(_dot(h_ref[0], w_ref[...]), 0.0)
    o_ref[0] = (z * z).astype(o_ref.dtype)


def _ff1_call(h, w):
    b, t, d = h.shape
    n = w.shape[1]
    tm = _tile(t, 1024, SUBLANES)
    tn = _tile(n, 512)
    return pl.pallas_call(
        _ff1_kernel,
        grid=(b, t // tm, n // tn),
        in_specs=[pl.BlockSpec((1, tm, d), lambda bi, i, j: (bi, i, 0)),
                  pl.BlockSpec((d, tn), lambda bi, i, j: (0, j))],
        out_specs=pl.BlockSpec((1, tm, tn), lambda bi, i, j: (bi, i, j)),
        out_shape=jax.ShapeDtypeStruct((b, t, n), BF16),
        compiler_params=_params("parallel", "parallel", "arbitrary"),
        name="ff1",
    )(h, w)


def _ff2_kernel(a_ref, w_ref, x_ref, gate_ref, o_ref, acc_ref):
    k = pl.program_id(3)

    @pl.when(k == 0)
    def _():
        acc_ref[...] = jnp.zeros_like(acc_ref)

    acc_ref[...] += _dot(a_ref[0], w_ref[...])

    @pl.when(k == pl.num_programs(3) - 1)
    def _():
        o_ref[0] = x_ref[0] + gate_ref[0] * acc_ref[...]


def _ff2_call(a, w, x, mod3, gate_blk_of):
    b, t, kw = a.shape
    d = w.shape[1]
    tm = _tile(t, 1024, SUBLANES)
    tn = _tile(d, 1024)
    tk = _tile(kw, 2048)
    return pl.pallas_call(
        _ff2_kernel,
        grid=(b, t // tm, d // tn, kw // tk),
        in_specs=[pl.BlockSpec((1, tm, tk), lambda bi, i, j, k: (bi, i, k)),
                  pl.BlockSpec((tk, tn), lambda bi, i, j, k: (k, j)),
                  pl.BlockSpec((1, tm, tn), lambda bi, i, j, k: (bi, i, j)),
                  pl.BlockSpec((1, 1, tn), lambda bi, i, j, k: (bi, 0, gate_blk_of(tn) + j))],
        out_specs=pl.BlockSpec((1, tm, tn), lambda bi, i, j, k: (bi, i, j)),
        out_shape=jax.ShapeDtypeStruct((b, t, d), F32),
        scratch_shapes=[pltpu.VMEM((tm, tn), F32)],
        compiler_params=_params("parallel", "parallel", "parallel", "arbitrary"),
        name="ff2",
    )(a, w, x, mod3)


def _rope_tables(t_len):
    pos = jnp.arange(t_len)
    row = (pos // GRID_W).astype(F32)
    col = (pos % GRID_W).astype(F32)
    n_freq = LANES // 4
    inv_freq = ROPE_BASE ** (-jnp.arange(n_freq, dtype=F32) / n_freq)
    ang = jnp.concatenate([row[:, None] * inv_freq, col[:, None] * inv_freq], axis=-1)
    cos, sin = jnp.cos(ang), jnp.sin(ang)
    cos_rep = jnp.repeat(cos, 2, axis=-1)
    sin_signed = jnp.stack([-sin, sin], axis=-1).reshape(t_len, LANES)
    return cos_rep, sin_signed


def kernel(x, c, ctx, c_ctx, w_mod, b_mod, norm1_g, norm2_g, w_in, hg_lb_logits, hg_norm_g, ret_decay_logit,
           w_branch_hgrn, w_branch_ret, w_out, w_ff1, w_ff2, final_norm_g):
    b, t, d = x.shape
    assert w_mod.shape[0] == 1 and hg_lb_logits.shape[0] == 2, "one layer"
    assert b + 1 <= SUBLANES
    hw = w_branch_hgrn.shape[1]
    heads = hw // LANES
    assert w_branch_ret.shape[1] == hw and w_in.shape[2] == 9 * hw + 2 * d
    n_state = 5 * hw

    c_rows = jnp.zeros((SUBLANES, d), F32).at[:b].set(c).at[b].set(c_ctx)
    mod3 = _mod_call(c_rows, w_mod[0], b_mod[0]).reshape(SUBLANES, 1, 6 * d)

    w_in_b = w_in[0].astype(BF16)
    h_x = _norm_mod_call(x, norm1_g[0], mod3, lambda bi: bi, 0, 1)
    h_c = _norm_mod_call(ctx, norm1_g[0], mod3, lambda bi: b, 0, 1)
    p_x = _in_proj_call(h_x, w_in_b, w_in_b.shape[1])
    p_c = _in_proj_call(h_c, w_in_b, n_state)

    y_hg = _hgrn_call(p_x, p_c, hg_lb_logits, hg_norm_g[0], heads)
    cos_rep, sin_signed = _rope_tables(t)
    decay_logit = jnp.broadcast_to(ret_decay_logit[0].T[:, :, None], (heads, 2, LANES))
    y_ret = _ret_call(p_x, p_c, decay_logit, cos_rep, sin_signed, heads)

    merged = _merge_call(y_hg, y_ret, w_branch_hgrn[0].astype(BF16), w_branch_ret[0].astype(BF16), p_x,
                         9 * heads, 9 * heads + d // LANES)
    x1 = _out_proj_call(merged, w_out[0].astype(BF16), x, mod3, lambda tn: 2 * d // tn)
    h2 = _norm_mod_call(x1, norm2_g[0], mod3, lambda bi: bi, 3, 4)
    act = _ff1_call(h2, w_ff1[0].astype(BF16))
    x2 = _ff2_call(act, w_ff2[0].astype(BF16), x1, mod3, lambda tn: 5 * d // tn)
    return _final_norm_call(x2, final_norm_g)
```

```python
import jax
import jax.numpy as jnp
from jax import lax
from jax.experimental import pallas as pl
from jax.experimental.pallas import tpu as pltpu

F32 = jnp.float32
BF16 = jnp.bfloat16

LANES = 128
SUBLANES = 8
VMEM_LIMIT = 56 * 2 ** 20
EPS = 1e-6
ROPE_BASE = 10000.0
GRID_W = 64
HG_CHUNK = 64
RET_CHUNK = 256
HG_STATE_UNROLL = 4
HG_OUT_UNROLL = 4
RET_UNROLL = 2


def _params(*sem):
    return pltpu.CompilerParams(dimension_semantics=sem, vmem_limit_bytes=VMEM_LIMIT)


def _tile(n, target, unit=LANES):
    t = min(n, target) // unit * unit
    while n % t:
        t -= unit
    return t


def _silu(z):
    return z * jax.nn.sigmoid(z)


def _dot(a, b):
    return jnp.dot(a, b, preferred_element_type=F32)


def _dot_nt(a, b):
    return lax.dot_general(a, b, (((1,), (1,)), ((), ())), preferred_element_type=F32)


def _dot_tn(a, b):
    return lax.dot_general(a, b, (((0,), (0,)), ((), ())), preferred_element_type=F32)


def _mod_kernel(c_ref, w_ref, b_ref, o_ref):
    s = _silu(c_ref[...]).astype(BF16)
    o_ref[...] = _dot(s, w_ref[...].astype(BF16)) + b_ref[...]


def _mod_call(c_rows, w_mod, b_mod):
    r, d = c_rows.shape
    n = w_mod.shape[1]
    tn = _tile(n, 512)
    return pl.pallas_call(
        _mod_kernel,
        grid=(n // tn,),
        in_specs=[pl.BlockSpec((r, d), lambda j: (0, 0)),
                  pl.BlockSpec((d, tn), lambda j: (0, j)),
                  pl.BlockSpec((1, tn), lambda j: (0, j))],
        out_specs=pl.BlockSpec((r, tn), lambda j: (0, j)),
        out_shape=jax.ShapeDtypeStruct((r, n), F32),
        compiler_params=_params("parallel"),
        name="mod",
    )(c_rows, w_mod, b_mod.reshape(1, n))


def _norm_mod_kernel(x_ref, g_ref, scale_ref, shift_ref, o_ref):
    x = x_ref[0]
    y = x * lax.rsqrt(jnp.mean(x * x, axis=-1, keepdims=True) + EPS) * g_ref[...]
    o_ref[0] = (y * (1.0 + scale_ref[0]) + shift_ref[0]).astype(o_ref.dtype)


def _norm_mod_call(x, gain, mod3, row_of_batch, shift_blk, scale_blk):
    b, t, d = x.shape
    tm = _tile(t, 256, SUBLANES)
    return pl.pallas_call(
        _norm_mod_kernel,
        grid=(b, t // tm),
        in_specs=[pl.BlockSpec((1, tm, d), lambda bi, i: (bi, i, 0)),
                  pl.BlockSpec((1, d), lambda bi, i: (0, 0)),
                  pl.BlockSpec((1, 1, d), lambda bi, i: (row_of_batch(bi), 0, scale_blk)),
                  pl.BlockSpec((1, 1, d), lambda bi, i: (row_of_batch(bi), 0, shift_blk))],
        out_specs=pl.BlockSpec((1, tm, d), lambda bi, i: (bi, i, 0)),
        out_shape=jax.ShapeDtypeStruct((b, t, d), BF16),
        compiler_params=_params("parallel", "parallel"),
        name="norm_mod",
    )(x, gain.reshape(1, d), mod3, mod3)


def _final_norm_kernel(x_ref, g_ref, o_ref):
    x = x_ref[0]
    o_ref[0] = x * lax.rsqrt(jnp.mean(x * x, axis=-1, keepdims=True) + EPS) * g_ref[...]


def _final_norm_call(x, gain):
    b, t, d = x.shape
    tm = _tile(t, 256, SUBLANES)
    return pl.pallas_call(
        _final_norm_kernel,
        grid=(b, t // tm),
        in_specs=[pl.BlockSpec((1, tm, d), lambda bi, i: (bi, i, 0)),
                  pl.BlockSpec((1, d), lambda bi, i: (0, 0))],
        out_specs=pl.BlockSpec((1, tm, d), lambda bi, i: (bi, i, 0)),
        out_shape=jax.ShapeDtypeStruct((b, t, d), F32),
        compiler_params=_params("parallel", "parallel"),
        name="final_norm",
    )(x, gain.reshape(1, d))


def _in_proj_kernel(h_ref, w_ref, o_ref):
    acc = _dot(h_ref[0], w_ref[...])
    for j in range(o_ref.shape[1]):
        o_ref[0, j] = acc[:, j * LANES:(j + 1) * LANES]


def _in_proj_call(h, w, n):
    b, t, d = h.shape
    tm = _tile(t, 1024, SUBLANES)
    tn = _tile(n, 512)
    return pl.pallas_call(
        _in_proj_kernel,
        grid=(b, t // tm, n // tn),
        in_specs=[pl.BlockSpec((1, tm, d), lambda bi, i, j: (bi, i, 0)),
                  pl.BlockSpec((d, tn), lambda bi, i, j: (0, j))],
        out_specs=pl.BlockSpec((1, tn // LANES, tm, LANES), lambda bi, i, j: (bi, j, i, 0)),
        out_shape=jax.ShapeDtypeStruct((b, n // LANES, t, LANES), F32),
        compiler_params=_params("parallel", "parallel", "arbitrary"),
        name="in_proj",
    )(h, w)


def _cumsum_rows(x, reverse):
    n = x.shape[0]
    row = lax.broadcasted_iota(jnp.int32, x.shape, 0)
    d = 1
    while d < n:
        if reverse:
            x = x + jnp.where(row < n - d, pltpu.roll(x, n - d, axis=0), 0.0)
        else:
            x = x + jnp.where(row >= d, pltpu.roll(x, d, axis=0), 0.0)
        d *= 2
    return x


def _block_row_bcast(x, block, offset):
    n, w = x.shape
    if block >= SUBLANES:
        xb = x.reshape(n // block, block, w)
        return jnp.broadcast_to(xb[:, offset:offset + 1, :], xb.shape).reshape(n, w)
    m = lax.broadcasted_iota(jnp.int32, x.shape, 0) & (block - 1)
    out = x
    for mm in range(block):
        if mm != offset:
            out = jnp.where(m == mm, pltpu.roll(x, (mm - offset) % n, axis=0), out)
    return out


def _hgrn_gates(z, lb):
    f = lb + (1.0 - lb) * jax.nn.sigmoid(z)
    return 1.0 - f, jnp.log2(f)


def _state_update(s_ref, k, cum, v, reverse):
    n = cum.shape[0]
    edge = cum[0:1, :] if reverse else cum[n - 1:n, :]
    k_dec = (k * jnp.exp2(edge - cum)).astype(BF16)
    s_ref[...] = s_ref[...] * jnp.exp2(edge) + _dot_tn(v.astype(BF16), k_dec)


def _hgrn_kernel(i_ref, ff_ref, fb_ref, q_ref, g_ref, ci_ref, cff_ref, cfb_ref, lbl_ref, ng_ref, y_ref,
                 kf_s, kb_s, cf_s, cb_s, st_s, sf_s, sb_s):
    t_len, tc_len = i_ref.shape[2], ci_ref.shape[2]
    ch = HG_CHUNK
    cc = min(ch, tc_len)
    n, nc = t_len // ch, tc_len // cc

    l0, l1 = lbl_ref[0], lbl_ref[1]
    mx = jnp.maximum(l0, l1)
    e0, e1 = jnp.exp(l0 - mx), jnp.exp(l1 - mx)
    lb = e0 / (e0 + e1)
    lb_f, lb_b = lb[0:1], lb[1:2]

    sf_s[...] = jnp.zeros_like(sf_s)
    sb_s[...] = jnp.zeros_like(sb_s)

    def ctx_body(c, carry):
        rf = pl.multiple_of(c * cc, cc)
        rb = pl.multiple_of((nc - 1 - c) * cc, cc)
        k, lf = _hgrn_gates(cff_ref[0, 0, pl.ds(rf, cc), :], lb_f)
        _state_update(sf_s, k, _cumsum_rows(lf, False), ci_ref[0, 0, pl.ds(rf, cc), :], False)
        k, lf = _hgrn_gates(cfb_ref[0, 0, pl.ds(rb, cc), :], lb_b)
        _state_update(sb_s, k, _cumsum_rows(lf, True), ci_ref[0, 0, pl.ds(rb, cc), :], True)
        return carry

    lax.fori_loop(0, nc, ctx_body, 0)

    def state_body(c, carry):
        cb = n - 1 - c
        rf = pl.multiple_of(c * ch, ch)
        rb = pl.multiple_of(cb * ch, ch)
        k, lf = _hgrn_gates(ff_ref[0, 0, pl.ds(rf, ch), :], lb_f)
        cum = _cumsum_rows(lf, False)
        kf_s[pl.ds(rf, ch), :] = k
        cf_s[pl.ds(rf, ch), :] = cum
        st_s[c, :, 0:LANES] = sf_s[...].astype(BF16)
        _state_update(sf_s, k, cum, i_ref[0, 0, pl.ds(rf, ch), :], False)
        k, lf = _hgrn_gates(fb_ref[0, 0, pl.ds(rb, ch), :], lb_b)
        cum = _cumsum_rows(lf, True)
        kb_s[pl.ds(rb, ch), :] = k
        cb_s[pl.ds(rb, ch), :] = cum
        st_s[cb, :, LANES:2 * LANES] = sb_s[...].astype(BF16)
        _state_update(sb_s, k, cum, i_ref[0, 0, pl.ds(rb, ch), :], True)
        return carry

    lax.fori_loop(0, n, state_body, 0, unroll=HG_STATE_UNROLL)

    pair_xor = (lax.broadcasted_iota(jnp.int32, (ch, ch), 0) ^ lax.broadcasted_iota(jnp.int32, (ch, ch), 1))
    row = lax.broadcasted_iota(jnp.int32, (ch, LANES), 0)
    q_scale = LANES ** -0.5

    def out_body(c, carry):
        r = pl.multiple_of(c * ch, ch)
        q = _silu(q_ref[0, 0, pl.ds(r, ch), :]) * q_scale
        kf, kb = kf_s[pl.ds(r, ch), :], kb_s[pl.ds(r, ch), :]
        cf, cb = cf_s[pl.ds(r, ch), :], cb_s[pl.ds(r, ch), :]
        v = i_ref[0, 0, pl.ds(r, ch), :].astype(BF16)
        a = jnp.where(pair_xor == 0, _dot_nt(q.astype(BF16), (kf + kb).astype(BF16)), 0.0)
        odd = (row & 1) != 0
        qm = q * jnp.where(odd, 1.0 - kf, 1.0 - kb)
        a = jnp.where(pair_xor == 1, _dot_nt(qm.astype(BF16), jnp.where(odd, kb, kf).astype(BF16)), a)
        h, log_h = 2, 1
        while h < ch:
            second = (row & h) != 0
            df = cf - _block_row_bcast(cf, 2 * h, h - 1)
            db = cb - _block_row_bcast(cb, 2 * h, h)
            qm = q * jnp.exp2(jnp.where(second, df, db))
            km = jnp.where(second, kb, kf) * jnp.exp2(-jnp.where(second, db, df))
            a = jnp.where((pair_xor >> log_h) == 1, _dot_nt(qm.astype(BF16), km.astype(BF16)), a)
            h, log_h = 2 * h, log_h + 1
        q_in = jnp.concatenate([q * jnp.exp2(cf), q * jnp.exp2(cb)], axis=1).astype(BF16)
        o = _dot(a.astype(BF16), v) + _dot_nt(q_in, st_s[c])
        y = o * lax.rsqrt(jnp.mean(o * o, axis=-1, keepdims=True) + EPS) * ng_ref[...]
        y_ref[0, pl.ds(r, ch), :] = (y * _silu(g_ref[0, 0, pl.ds(r, ch), :])).astype(y_ref.dtype)
        return carry

    lax.fori_loop(0, n, out_body, 0, unroll=HG_OUT_UNROLL)


def _hgrn_call(p_x, p_c, lb_logits, norm_g, heads):
    b, _, t, _ = p_x.shape
    tc = p_c.shape[2]

    def col(off):
        return pl.BlockSpec((1, 1, t, LANES), lambda bi, h: (bi, off * heads + h, 0, 0))

    def ctx_col(off):
        return pl.BlockSpec((1, 1, tc, LANES), lambda bi, h: (bi, off * heads + h, 0, 0))

    return pl.pallas_call(
        _hgrn_kernel,
        grid=(b, heads),
        in_specs=[col(0), col(1), col(2), col(5), col(6), ctx_col(0), ctx_col(1), ctx_col(2),
                  pl.BlockSpec((2, 2, LANES), lambda bi, h: (0, 0, h)),
                  pl.BlockSpec((1, LANES), lambda bi, h: (0, 0))],
        out_specs=pl.BlockSpec((1, t, LANES), lambda bi, h: (bi, 0, h)),
        out_shape=jax.ShapeDtypeStruct((b, t, heads * LANES), BF16),
        scratch_shapes=[pltpu.VMEM((t, LANES), F32), pltpu.VMEM((t, LANES), F32),
                        pltpu.VMEM((t, LANES), F32), pltpu.VMEM((t, LANES), F32),
                        pltpu.VMEM((t // HG_CHUNK, LANES, 2 * LANES), BF16),
                        pltpu.VMEM((LANES, LANES), F32), pltpu.VMEM((LANES, LANES), F32)],
        compiler_params=_params("parallel", "parallel"),
        name="hgrn",
    )(p_x, p_x, p_x, p_x, p_x, p_c, p_c, p_c, lb_logits, norm_g.reshape(1, LANES))


def _rope(a, cos, sin_signed):
    lane = lax.broadcasted_iota(jnp.int32, a.shape, 1)
    partner = jnp.where((lane & 1) == 0, pltpu.roll(a, LANES - 1, axis=1), pltpu.roll(a, 1, axis=1))
    return a * cos + partner * sin_signed


def _ret_state_update(s_ref, k, v, k_decay, s_decay):
    s_ref[...] = s_ref[...] * s_decay + _dot_tn(v.astype(BF16), (k * k_decay).astype(BF16))


def _ret_kernel(q_ref, k_ref, v_ref, g_ref, ck_ref, cv_ref, dl_ref, cos_ref, sin_ref, y_ref,
                kr_s, st_s, sf_s, sb_s):
    t_len, tc_len = q_ref.shape[2], ck_ref.shape[2]
    ch = RET_CHUNK
    cc = min(ch, tc_len)
    n, nc = t_len // ch, tc_len // cc
    k_scale = LANES ** -0.5

    dl = dl_ref[0]
    gam = jnp.minimum(dl, 0.0) - jnp.log1p(jnp.exp(-jnp.abs(dl)))
    gam_f, gam_b = gam[0:1], gam[1:2]

    def decays(m):
        pos = lax.broadcasted_iota(jnp.int32, (m, LANES), 0).astype(F32)
        return dict(kf=jnp.exp(gam_f * (m - 1.0 - pos)), kb=jnp.exp(gam_b * pos),
                    qf=jnp.exp(gam_f * (pos + 1.0)), qb=jnp.exp(gam_b * (m - pos)),
                    sf=jnp.exp(gam_f * m), sb=jnp.exp(gam_b * m))

    sf_s[...] = jnp.zeros_like(sf_s)
    sb_s[...] = jnp.zeros_like(sb_s)

    dc = decays(cc)

    def ctx_body(c, carry):
        rf = pl.multiple_of(c * cc, cc)
        rb = pl.multiple_of((nc - 1 - c) * cc, cc)
        _ret_state_update(sf_s, ck_ref[0, 0, pl.ds(rf, cc), :] * k_scale, cv_ref[0, 0, pl.ds(rf, cc), :],
                          dc["kf"], dc["sf"])
        _ret_state_update(sb_s, ck_ref[0, 0, pl.ds(rb, cc), :] * k_scale, cv_ref[0, 0, pl.ds(rb, cc), :],
                          dc["kb"], dc["sb"])
        return carry

    lax.fori_loop(0, nc, ctx_body, 0)

    def rope_body(c, carry):
        r = pl.multiple_of(c * ch, ch)
        kr_s[pl.ds(r, ch), :] = _rope(k_ref[0, 0, pl.ds(r, ch), :] * k_scale,
                                      cos_ref[pl.ds(r, ch), :], sin_ref[pl.ds(r, ch), :])
        return carry

    lax.fori_loop(0, n, rope_body, 0, unroll=RET_UNROLL)

    dd = decays(ch)

    def state_body(c, carry):
        cb = n - 1 - c
        rf = pl.multiple_of(c * ch, ch)
        rb = pl.multiple_of(cb * ch, ch)
        st_s[c, :, 0:LANES] = sf_s[...].astype(BF16)
        _ret_state_update(sf_s, kr_s[pl.ds(rf, ch), :], v_ref[0, 0, pl.ds(rf, ch), :], dd["kf"], dd["sf"])
        st_s[cb, :, LANES:2 * LANES] = sb_s[...].astype(BF16)
        _ret_state_update(sb_s, kr_s[pl.ds(rb, ch), :], v_ref[0, 0, pl.ds(rb, ch), :], dd["kb"], dd["sb"])
        return carry

    lax.fori_loop(0, n, state_body, 0, unroll=RET_UNROLL)

    ti = lax.broadcasted_iota(jnp.int32, (ch, ch), 0)
    si = lax.broadcasted_iota(jnp.int32, (ch, ch), 1)
    lag = (ti - si).astype(F32)
    g_f = jnp.broadcast_to(gam_f[:, 0:1], (ch, ch))
    g_b = jnp.broadcast_to(gam_b[:, 0:1], (ch, ch))
    decay = jnp.where(ti > si, jnp.exp(g_f * jnp.maximum(lag, 0.0)),
                      jnp.where(ti < si, jnp.exp(g_b * jnp.maximum(-lag, 0.0)), 2.0))

    def out_body(c, carry):
        r = pl.multiple_of(c * ch, ch)
        q = _rope(q_ref[0, 0, pl.ds(r, ch), :], cos_ref[pl.ds(r, ch), :], sin_ref[pl.ds(r, ch), :])
        k = kr_s[pl.ds(r, ch), :].astype(BF16)
        v = v_ref[0, 0, pl.ds(r, ch), :].astype(BF16)
        a = _dot_nt(q.astype(BF16), k) * decay
        q_in = jnp.concatenate([q * dd["qf"], q * dd["qb"]], axis=1).astype(BF16)
        o = _dot(a.astype(BF16), v) + _dot_nt(q_in, st_s[c])
        oc = o - jnp.mean(o, axis=-1, keepdims=True)
        y = oc * lax.rsqrt(jnp.mean(oc * oc, axis=-1, keepdims=True) + EPS)
        y_ref[0, pl.ds(r, ch), :] = (y * _silu(g_ref[0, 0, pl.ds(r, ch), :])).astype(y_ref.dtype)
        return carry

    lax.fori_loop(0, n, out_body, 0, unroll=RET_UNROLL)


def _ret_call(p_x, p_c, decay_logit, cos, sin_signed, heads):
    b, _, t, _ = p_x.shape
    tc = p_c.shape[2]

    def col(off):
        return pl.BlockSpec((1, 1, t, LANES), lambda bi, h: (bi, off * heads + h, 0, 0))

    def ctx_col(off):
        return pl.BlockSpec((1, 1, tc, LANES), lambda bi, h: (bi, off * heads + h, 0, 0))

    return pl.pallas_call(
        _ret_kernel,
        grid=(b, heads),
        in_specs=[col(7), col(3), col(4), col(8), ctx_col(3), ctx_col(4),
                  pl.BlockSpec((1, 2, LANES), lambda bi, h: (h, 0, 0)),
                  pl.BlockSpec((t, LANES), lambda bi, h: (0, 0)),
                  pl.BlockSpec((t, LANES), lambda bi, h: (0, 0))],
        out_specs=pl.BlockSpec((1, t, LANES), lambda bi, h: (bi, 0, h)),
        out_shape=jax.ShapeDtypeStruct((b, t, heads * LANES), BF16),
        scratch_shapes=[pltpu.VMEM((t, LANES), F32),
                        pltpu.VMEM((t // RET_CHUNK, LANES, 2 * LANES), BF16),
                        pltpu.VMEM((LANES, LANES), F32), pltpu.VMEM((LANES, LANES), F32)],
        compiler_params=_params("parallel", "parallel"),
        name="ret",
    )(p_x, p_x, p_x, p_x, p_c, p_c, decay_logit, cos, sin_signed)


def _merge_kernel(yh_ref, yr_ref, wh_ref, wr_ref, gh_ref, gr_ref, o_ref):
    acc_h = _dot(yh_ref[0], wh_ref[...])
    acc_r = _dot(yr_ref[0], wr_ref[...])
    for j in range(gh_ref.shape[1]):
        sl = slice(j * LANES, (j + 1) * LANES)
        o_ref[0, :, sl] = (jax.nn.sigmoid(gh_ref[0, j]) * acc_h[:, sl]
                           + jax.nn.sigmoid(gr_ref[0, j]) * acc_r[:, sl]).astype(o_ref.dtype)


def _merge_call(y_hg, y_ret, w_bh, w_br, p_x, gate_h_blk, gate_r_blk):
    b, t, kw = y_hg.shape
    d = w_bh.shape[1]
    tm = _tile(t, 1024, SUBLANES)
    tn = _tile(d, 512)
    nb = tn // LANES
    return pl.pallas_call(
        _merge_kernel,
        grid=(b, t // tm, d // tn),
        in_specs=[pl.BlockSpec((1, tm, kw), lambda bi, i, j: (bi, i, 0)),
                  pl.BlockSpec((1, tm, kw), lambda bi, i, j: (bi, i, 0)),
                  pl.BlockSpec((kw, tn), lambda bi, i, j: (0, j)),
                  pl.BlockSpec((kw, tn), lambda bi, i, j: (0, j)),
                  pl.BlockSpec((1, nb, tm, LANES), lambda bi, i, j: (bi, gate_h_blk // nb + j, i, 0)),
                  pl.BlockSpec((1, nb, tm, LANES), lambda bi, i, j: (bi, gate_r_blk // nb + j, i, 0))],
        out_specs=pl.BlockSpec((1, tm, tn), lambda bi, i, j: (bi, i, j)),
        out_shape=jax.ShapeDtypeStruct((b, t, d), BF16),
        compiler_params=_params("parallel", "parallel", "arbitrary"),
        name="merge",
    )(y_hg, y_ret, w_bh, w_br, p_x, p_x)


def _gated_residual_kernel(a_ref, w_ref, x_ref, gate_ref, o_ref):
    o_ref[0] = x_ref[0] + gate_ref[0] * _dot(a_ref[0], w_ref[...])


def _out_proj_call(a, w, x, mod3, gate_blk_of):
    b, t, kw = a.shape
    d = w.shape[1]
    tm = _tile(t, 1024, SUBLANES)
    tn = _tile(d, 512)
    return pl.pallas_call(
        _gated_residual_kernel,
        grid=(b, t // tm, d // tn),
        in_specs=[pl.BlockSpec((1, tm, kw), lambda bi, i, j: (bi, i, 0)),
                  pl.BlockSpec((kw, tn), lambda bi, i, j: (0, j)),
                  pl.BlockSpec((1, tm, tn), lambda bi, i, j: (bi, i, j)),
                  pl.BlockSpec((1, 1, tn), lambda bi, i, j: (bi, 0, gate_blk_of(tn) + j))],
        out_specs=pl.BlockSpec((1, tm, tn), lambda bi, i, j: (bi, i, j)),
        out_shape=jax.ShapeDtypeStruct((b, t, d), F32),
        compiler_params=_params("parallel", "parallel", "arbitrary"),
        name="out_proj",
    )(a, w, x, mod3)


def _ff1_kernel(h_ref, w_ref, o_ref):
    z = jnp.maximum(_dot(h_ref[0], w_ref[...]), 0.0)
    o_ref[0] = (z * z).astype(o_ref.dtype)


def _ff1_call(h, w):
    b, t, d = h.shape
    n = w.shape[1]
    tm = _tile(t, 1024, SUBLANES)
    tn = _tile(n, 512)
    return pl.pallas_call(
        _ff1_kernel,
        grid=(b, t // tm, n // tn),
        in_specs=[pl.BlockSpec((1, tm, d), lambda bi, i, j: (bi, i, 0)),
                  pl.BlockSpec((d, tn), lambda bi, i, j: (0, j))],
        out_specs=pl.BlockSpec((1, tm, tn), lambda bi, i, j: (bi, i, j)),
        out_shape=jax.ShapeDtypeStruct((b, t, n), BF16),
        compiler_params=_params("parallel", "parallel", "arbitrary"),
        name="ff1",
    )(h, w)


def _ff2_kernel(a_ref, w_ref, x_ref, gate_ref, o_ref, acc_ref):
    k = pl.program_id(3)

    @pl.when(k == 0)
    def _():
        acc_ref[...] = jnp.zeros_like(acc_ref)

    acc_ref[...] += _dot(a_ref[0], w_ref[...])

    @pl.when(k == pl.num_programs(3) - 1)
    def _():
        o_ref[0] = x_ref[0] + gate_ref[0] * acc_ref[...]


def _ff2_call(a, w, x, mod3, gate_blk_of):
    b, t, kw = a.shape
    d = w.shape[1]
    tm = _tile(t, 1024, SUBLANES)
    tn = _tile(d, 1024)
    tk = _tile(kw, 2048)
    return pl.pallas_call(
        _ff2_kernel,
        grid=(b, t // tm, d // tn, kw // tk),
        in_specs=[pl.BlockSpec((1, tm, tk), lambda bi, i, j, k: (bi, i, k)),
                  pl.BlockSpec((tk, tn), lambda bi, i, j, k: (k, j)),
                  pl.BlockSpec((1, tm, tn), lambda bi, i, j, k: (bi, i, j)),
                  pl.BlockSpec((1, 1, tn), lambda bi, i, j, k: (bi, 0, gate_blk_of(tn) + j))],
        out_specs=pl.BlockSpec((1, tm, tn), lambda bi, i, j, k: (bi, i, j)),
        out_shape=jax.ShapeDtypeStruct((b, t, d), F32),
        scratch_shapes=[pltpu.VMEM((tm, tn), F32)],
        compiler_params=_params("parallel", "parallel", "parallel", "arbitrary"),
        name="ff2",
    )(a, w, x, mod3)


def _rope_tables(t_len):
    pos = jnp.arange(t_len)
    row = (pos // GRID_W).astype(F32)
    col = (pos % GRID_W).astype(F32)
    n_freq = LANES // 4
    inv_freq = ROPE_BASE ** (-jnp.arange(n_freq, dtype=F32) / n_freq)
    ang = jnp.concatenate([row[:, None] * inv_freq, col[:, None] * inv_freq], axis=-1)
    cos, sin = jnp.cos(ang), jnp.sin(ang)
    cos_rep = jnp.repeat(cos, 2, axis=-1)
    sin_signed = jnp.stack([-sin, sin], axis=-1).reshape(t_len, LANES)
    return cos_rep, sin_signed


def kernel(x, c, ctx, c_ctx, w_mod, b_mod, norm1_g, norm2_g, w_in, hg_lb_logits, hg_norm_g, ret_decay_logit,
           w_branch_hgrn, w_branch_ret, w_out, w_ff1, w_ff2, final_norm_g):
    b, t, d = x.shape
    assert w_mod.shape[0] == 1 and hg_lb_logits.shape[0] == 2, "one layer"
    assert b + 1 <= SUBLANES
    hw = w_branch_hgrn.shape[1]
    heads = hw // LANES
    assert w_branch_ret.shape[1] == hw and w_in.shape[2] == 9 * hw + 2 * d
    n_state = 5 * hw

    c_rows = jnp.zeros((SUBLANES, d), F32).at[:b].set(c).at[b].set(c_ctx)
    mod3 = _mod_call(c_rows, w_mod[0], b_mod[0]).reshape(SUBLANES, 1, 6 * d)

    w_in_b = w_in[0].astype(BF16)
    h_x = _norm_mod_call(x, norm1_g[0], mod3, lambda bi: bi, 0, 1)
    h_c = _norm_mod_call(ctx, norm1_g[0], mod3, lambda bi: b, 0, 1)
    p_x = _in_proj_call(h_x, w_in_b, w_in_b.shape[1])
    p_c = _in_proj_call(h_c, w_in_b, n_state)

    y_hg = _hgrn_call(p_x, p_c, hg_lb_logits, hg_norm_g[0], heads)
    cos_rep, sin_signed = _rope_tables(t)
    decay_logit = jnp.broadcast_to(ret_decay_logit[0].T[:, :, None], (heads, 2, LANES))
    y_ret = _ret_call(p_x, p_c, decay_logit, cos_rep, sin_signed, heads)

    merged = _merge_call(y_hg, y_ret, w_branch_hgrn[0].astype(BF16), w_branch_ret[0].astype(BF16), p_x,
                         9 * heads, 9 * heads + d // LANES)
    x1 = _out_proj_call(merged, w_out[0].astype(BF16), x, mod3, lambda tn: 2 * d // tn)
    h2 = _norm_mod_call(x1, norm2_g[0], mod3, lambda bi: bi, 3, 4)
    act = _ff1_call(h2, w_ff1[0].astype(BF16))
    x2 = _ff2_call(act, w_ff2[0].astype(BF16), x1, mod3, lambda tn: 5 * d // tn)
    return _final_norm_call(x2, final_norm_g)
```

```python
import functools

import jax
import jax.numpy as jnp
import numpy as np
from jax import lax
from jax.experimental import pallas as pl
from jax.experimental.pallas import tpu as pltpu

F32 = jnp.float32
BF16 = jnp.bfloat16

LANES = 128
SUBLANES = 8
BF16_ROWS = 16
VMEM_LIMIT = 56 * 2 ** 20
VMEM_LIMIT_IN_PROJ = 60 * 2 ** 20
EPS = 1e-6
ROPE_BASE = 10000.0
GRID_W = 64
HG_CHUNK = 64
RET_CHUNK = 256
HG_STATE_UNROLL = 4
HG_OUT_UNROLL = 4
RET_UNROLL = 2


def _params(*sem):
    return pltpu.CompilerParams(dimension_semantics=sem, vmem_limit_bytes=VMEM_LIMIT)


def _tile(n, target, unit=LANES):
    t = min(n, target) // unit * unit
    while n % t:
        t -= unit
    return t


def _silu(z):
    return z * jax.nn.sigmoid(z)


def _dot(a, b):
    return jnp.dot(a, b, preferred_element_type=F32)


def _dot_nt(a, b):
    return lax.dot_general(a, b, (((1,), (1,)), ((), ())), preferred_element_type=F32)


def _dot_tn(a, b):
    return lax.dot_general(a, b, (((0,), (0,)), ((), ())), preferred_element_type=F32)


def _grid_step(rank):
    step = pl.program_id(0)
    for axis in range(1, rank):
        step = step * pl.num_programs(axis) + pl.program_id(axis)
    return step


def _linear_step(ids, sizes):
    step = ids[0]
    for i, n in zip(ids[1:], sizes[1:]):
        step = step * n + i
    return step


def _cast_specs(src, grid):
    rows, cols = src.shape
    steps = int(np.prod(grid))
    n_blocks = 1
    while n_blocks * 2 <= steps and rows % (n_blocks * 2) == 0 and rows // (n_blocks * 2) >= BF16_ROWS:
        n_blocks *= 2

    def idx(*ids):
        return (jnp.minimum(_linear_step(ids, grid), n_blocks - 1), 0)

    spec = pl.BlockSpec((rows // n_blocks, cols), idx)
    return n_blocks, spec, spec, jax.ShapeDtypeStruct(src.shape, BF16)


def _cast_block(src_ref, dst_ref, step, n_blocks):
    @pl.when(step < n_blocks)
    def _():
        dst_ref[...] = src_ref[...].astype(dst_ref.dtype)


def _mod_kernel(c_ref, w_ref, b_ref, o_ref):
    s = _silu(c_ref[...]).astype(BF16)
    o_ref[...] = _dot(s, w_ref[...].astype(BF16)) + b_ref[...]


def _mod_call(c_rows, w_mod, b_mod):
    r, d = c_rows.shape
    n = w_mod.shape[1]
    tn = _tile(n, 512)
    return pl.pallas_call(
        _mod_kernel,
        grid=(n // tn,),
        in_specs=[pl.BlockSpec((r, d), lambda j: (0, 0)),
                  pl.BlockSpec((d, tn), lambda j: (0, j)),
                  pl.BlockSpec((1, tn), lambda j: (0, j))],
        out_specs=pl.BlockSpec((r, tn), lambda j: (0, j)),
        out_shape=jax.ShapeDtypeStruct((r, n), F32),
        compiler_params=_params("parallel"),
        name="mod",
    )(c_rows, w_mod, b_mod.reshape(1, n))


def _norm_mod_kernel(x_ref, g_ref, scale_ref, shift_ref, o_ref):
    x = x_ref[0]
    y = x * lax.rsqrt(jnp.mean(x * x, axis=-1, keepdims=True) + EPS) * g_ref[...]
    o_ref[0] = (y * (1.0 + scale_ref[0]) + shift_ref[0]).astype(o_ref.dtype)


def _norm_mod_call(x, gain, mod3, row_of_batch, shift_blk, scale_blk):
    b, t, d = x.shape
    tm = _tile(t, 256, SUBLANES)
    return pl.pallas_call(
        _norm_mod_kernel,
        grid=(b, t // tm),
        in_specs=[pl.BlockSpec((1, tm, d), lambda bi, i: (bi, i, 0)),
                  pl.BlockSpec((1, d), lambda bi, i: (0, 0)),
                  pl.BlockSpec((1, 1, d), lambda bi, i: (row_of_batch(bi), 0, scale_blk)),
                  pl.BlockSpec((1, 1, d), lambda bi, i: (row_of_batch(bi), 0, shift_blk))],
        out_specs=pl.BlockSpec((1, tm, d), lambda bi, i: (bi, i, 0)),
        out_shape=jax.ShapeDtypeStruct((b, t, d), BF16),
        compiler_params=_params("parallel", "parallel"),
        name="norm_mod",
    )(x, gain.reshape(1, d), mod3, mod3)


def _final_norm_kernel(x_ref, g_ref, o_ref):
    x = x_ref[0]
    o_ref[0] = x * lax.rsqrt(jnp.mean(x * x, axis=-1, keepdims=True) + EPS) * g_ref[...]


def _final_norm_call(x, gain):
    b, t, d = x.shape
    tm = _tile(t, 256, SUBLANES)
    return pl.pallas_call(
        _final_norm_kernel,
        grid=(b, t // tm),
        in_specs=[pl.BlockSpec((1, tm, d), lambda bi, i: (bi, i, 0)),
                  pl.BlockSpec((1, d), lambda bi, i: (0, 0))],
        out_specs=pl.BlockSpec((1, tm, d), lambda bi, i: (bi, i, 0)),
        out_shape=jax.ShapeDtypeStruct((b, t, d), F32),
        compiler_params=_params("parallel", "parallel"),
        name="final_norm",
    )(x, gain.reshape(1, d))


def _in_proj_tile(h_ref, w_ref, o_ref, wb_s):
    @pl.when((pl.program_id(1) == 0) & (pl.program_id(2) == 0))
    def _():
        wb_s[...] = w_ref[...].astype(BF16)

    acc = _dot(h_ref[0], wb_s[...])
    for j in range(o_ref.shape[1]):
        o_ref[0, j] = acc[:, j * LANES:(j + 1) * LANES]


def _in_proj_kernel(h_ref, w_ref, o_ref, wb_s):
    _in_proj_tile(h_ref, w_ref, o_ref, wb_s)


def _in_proj_cast_kernel(n_cast, h_ref, w_ref, c_ref, o_ref, co_ref, wb_s):
    _in_proj_tile(h_ref, w_ref, o_ref, wb_s)
    _cast_block(c_ref, co_ref, _grid_step(3), n_cast)


def _in_proj_call(h, w, n, tm_target, cast_src=None):
    b, t, d = h.shape
    tm = _tile(t, tm_target, SUBLANES)
    tn = _tile(n, 1024)
    grid = (n // tn, b, t // tm)
    in_specs = [pl.BlockSpec((1, tm, d), lambda j, bi, i: (bi, i, 0)),
                pl.BlockSpec((d, tn), lambda j, bi, i: (0, j))]
    out_specs = pl.BlockSpec((1, tn // LANES, tm, LANES), lambda j, bi, i: (bi, j, i, 0))
    out_shape = jax.ShapeDtypeStruct((b, n // LANES, t, LANES), F32)
    common = dict(grid=grid, scratch_shapes=[pltpu.VMEM((d, tn), BF16)],
                  compiler_params=pltpu.CompilerParams(dimension_semantics=("arbitrary",) * 3,
                                                       vmem_limit_bytes=VMEM_LIMIT_IN_PROJ),
                  name="in_proj")
    if cast_src is None:
        return pl.pallas_call(_in_proj_kernel, in_specs=in_specs, out_specs=out_specs, out_shape=out_shape,
                              **common)(h, w)
    n_cast, c_in, c_out, c_shape = _cast_specs(cast_src, grid)
    return pl.pallas_call(functools.partial(_in_proj_cast_kernel, n_cast), in_specs=in_specs + [c_in],
                          out_specs=[out_specs, c_out], out_shape=[out_shape, c_shape], **common)(h, w, cast_src)


def _cumsum_rows(x, reverse):
    n = x.shape[0]
    row = lax.broadcasted_iota(jnp.int32, x.shape, 0)
    d = 1
    while d < n:
        if reverse:
            x = x + jnp.where(row < n - d, pltpu.roll(x, n - d, axis=0), 0.0)
        else:
            x = x + jnp.where(row >= d, pltpu.roll(x, d, axis=0), 0.0)
        d *= 2
    return x


def _block_row_bcast(x, block, offset):
    n, w = x.shape
    if block >= SUBLANES:
        xb = x.reshape(n // block, block, w)
        return jnp.broadcast_to(xb[:, offset:offset + 1, :], xb.shape).reshape(n, w)
    m = lax.broadcasted_iota(jnp.int32, x.shape, 0) & (block - 1)
    out = x
    for mm in range(block):
        if mm != offset:
            out = jnp.where(m == mm, pltpu.roll(x, (mm - offset) % n, axis=0), out)
    return out


def _hgrn_gates(z, lb):
    f = lb + (1.0 - lb) * jax.nn.sigmoid(z)
    return 1.0 - f, jnp.log2(f)


def _state_update(s_ref, k, cum, v, reverse):
    n = cum.shape[0]
    edge = cum[0:1, :] if reverse else cum[n - 1:n, :]
    k_dec = (k * jnp.exp2(edge - cum)).astype(BF16)
    s_ref[...] = s_ref[...] * jnp.exp2(edge) + _dot_tn(v.astype(BF16), k_dec)


def _hgrn_kernel(n_cast, i_ref, ff_ref, fb_ref, q_ref, g_ref, ci_ref, cff_ref, cfb_ref, lbl_ref, ng_ref, c_ref,
                 y_ref, co_ref, kf_s, kb_s, cf_s, cb_s, st_s, sf_s, sb_s):
    _cast_block(c_ref, co_ref, _grid_step(2), n_cast)
    t_len, tc_len = i_ref.shape[2], ci_ref.shape[2]
    ch = HG_CHUNK
    cc = min(ch, tc_len)
    n, nc = t_len // ch, tc_len // cc

    l0, l1 = lbl_ref[0], lbl_ref[1]
    mx = jnp.maximum(l0, l1)
    e0, e1 = jnp.exp(l0 - mx), jnp.exp(l1 - mx)
    lb = e0 / (e0 + e1)
    lb_f, lb_b = lb[0:1], lb[1:2]

    sf_s[...] = jnp.zeros_like(sf_s)
    sb_s[...] = jnp.zeros_like(sb_s)

    def ctx_body(c, carry):
        rf = pl.multiple_of(c * cc, cc)
        rb = pl.multiple_of((nc - 1 - c) * cc, cc)
        k, lf = _hgrn_gates(cff_ref[0, 0, pl.ds(rf, cc), :], lb_f)
        _state_update(sf_s, k, _cumsum_rows(lf, False), ci_ref[0, 0, pl.ds(rf, cc), :], False)
        k, lf = _hgrn_gates(cfb_ref[0, 0, pl.ds(rb, cc), :], lb_b)
        _state_update(sb_s, k, _cumsum_rows(lf, True), ci_ref[0, 0, pl.ds(rb, cc), :], True)
        return carry

    lax.fori_loop(0, nc, ctx_body, 0)

    def state_body(c, carry):
        cb = n - 1 - c
        rf = pl.multiple_of(c * ch, ch)
        rb = pl.multiple_of(cb * ch, ch)
        k, lf = _hgrn_gates(ff_ref[0, 0, pl.ds(rf, ch), :], lb_f)
        cum = _cumsum_rows(lf, False)
        kf_s[pl.ds(rf, ch), :] = k
        cf_s[pl.ds(rf, ch), :] = cum
        st_s[c, :, 0:LANES] = sf_s[...].astype(BF16)
        _state_update(sf_s, k, cum, i_ref[0, 0, pl.ds(rf, ch), :], False)
        k, lf = _hgrn_gates(fb_ref[0, 0, pl.ds(rb, ch), :], lb_b)
        cum = _cumsum_rows(lf, True)
        kb_s[pl.ds(rb, ch), :] = k
        cb_s[pl.ds(rb, ch), :] = cum
        st_s[cb, :, LANES:2 * LANES] = sb_s[...].astype(BF16)
        _state_update(sb_s, k, cum, i_ref[0, 0, pl.ds(rb, ch), :], True)
        return carry

    lax.fori_loop(0, n, state_body, 0, unroll=HG_STATE_UNROLL)

    pair_xor = (lax.broadcasted_iota(jnp.int32, (ch, ch), 0) ^ lax.broadcasted_iota(jnp.int32, (ch, ch), 1))
    row = lax.broadcasted_iota(jnp.int32, (ch, LANES), 0)
    q_scale = LANES ** -0.5

    def out_body(c, carry):
        r = pl.multiple_of(c * ch, ch)
        q = _silu(q_ref[0, 0, pl.ds(r, ch), :]) * q_scale
        kf, kb = kf_s[pl.ds(r, ch), :], kb_s[pl.ds(r, ch), :]
        cf, cb = cf_s[pl.ds(r, ch), :], cb_s[pl.ds(r, ch), :]
        v = i_ref[0, 0, pl.ds(r, ch), :].astype(BF16)
        a = jnp.where(pair_xor == 0, _dot_nt(q.astype(BF16), (kf + kb).astype(BF16)), 0.0)
        odd = (row & 1) != 0
        qm = q * jnp.where(odd, 1.0 - kf, 1.0 - kb)
        a = jnp.where(pair_xor == 1, _dot_nt(qm.astype(BF16), jnp.where(odd, kb, kf).astype(BF16)), a)
        h, log_h = 2, 1
        while h < ch:
            second = (row & h) != 0
            df = cf - _block_row_bcast(cf, 2 * h, h - 1)
            db = cb - _block_row_bcast(cb, 2 * h, h)
            qm = q * jnp.exp2(jnp.where(second, df, db))
            km = jnp.where(second, kb, kf) * jnp.exp2(-jnp.where(second, db, df))
            a = jnp.where((pair_xor >> log_h) == 1, _dot_nt(qm.astype(BF16), km.astype(BF16)), a)
            h, log_h = 2 * h, log_h + 1
        q_in = jnp.concatenate([q * jnp.exp2(cf), q * jnp.exp2(cb)], axis=1).astype(BF16)
        o = _dot(a.astype(BF16), v) + _dot_nt(q_in, st_s[c])
        y = o * lax.rsqrt(jnp.mean(o * o, axis=-1, keepdims=True) + EPS) * ng_ref[...]
        y_ref[0, pl.ds(r, ch), :] = (y * _silu(g_ref[0, 0, pl.ds(r, ch), :])).astype(y_ref.dtype)
        return carry

    lax.fori_loop(0, n, out_body, 0, unroll=HG_OUT_UNROLL)


def _hgrn_call(p_x, p_c, lb_logits, norm_g, heads, cast_src):
    b, _, t, _ = p_x.shape
    tc = p_c.shape[2]
    grid = (b, heads)
    n_cast, c_in, c_out, c_shape = _cast_specs(cast_src, grid)

    def col(off):
        return pl.BlockSpec((1, 1, t, LANES), lambda bi, h: (bi, off * heads + h, 0, 0))

    def ctx_col(off):
        return pl.BlockSpec((1, 1, tc, LANES), lambda bi, h: (bi, off * heads + h, 0, 0))

    return pl.pallas_call(
        functools.partial(_hgrn_kernel, n_cast),
        grid=grid,
        in_specs=[col(0), col(1), col(2), col(5), col(6), ctx_col(0), ctx_col(1), ctx_col(2),
                  pl.BlockSpec((2, 2, LANES), lambda bi, h: (0, 0, h)),
                  pl.BlockSpec((1, LANES), lambda bi, h: (0, 0)),
                  c_in],
        out_specs=[pl.BlockSpec((1, t, LANES), lambda bi, h: (bi, 0, h)), c_out],
        out_shape=[jax.ShapeDtypeStruct((b, t, heads * LANES), BF16), c_shape],
        scratch_shapes=[pltpu.VMEM((t, LANES), F32), pltpu.VMEM((t, LANES), F32),
                        pltpu.VMEM((t, LANES), F32), pltpu.VMEM((t, LANES), F32),
                        pltpu.VMEM((t // HG_CHUNK, LANES, 2 * LANES), BF16),
                        pltpu.VMEM((LANES, LANES), F32), pltpu.VMEM((LANES, LANES), F32)],
        compiler_params=_params("arbitrary", "arbitrary"),
        name="hgrn",
    )(p_x, p_x, p_x, p_x, p_x, p_c, p_c, p_c, lb_logits, norm_g.reshape(1, LANES), cast_src)


def _rope(a, cos, sin_signed):
    lane = lax.broadcasted_iota(jnp.int32, a.shape, 1)
    partner = jnp.where((lane & 1) == 0, pltpu.roll(a, LANES - 1, axis=1), pltpu.roll(a, 1, axis=1))
    return a * cos + partner * sin_signed


def _ret_state_update(s_ref, k, v, k_decay, s_decay):
    s_ref[...] = s_ref[...] * s_decay + _dot_tn(v.astype(BF16), (k * k_decay).astype(BF16))


def _ret_kernel(n_cast, q_ref, k_ref, v_ref, g_ref, ck_ref, cv_ref, dl_ref, cos_ref, sin_ref, c1_ref, c2_ref,
                y_ref, co1_ref, co2_ref, kr_s, st_s, sf_s, sb_s):
    _cast_block(c1_ref, co1_ref, _grid_step(2), n_cast)
    _cast_block(c2_ref, co2_ref, _grid_step(2), n_cast)
    t_len, tc_len = q_ref.shape[2], ck_ref.shape[2]
    ch = RET_CHUNK
    cc = min(ch, tc_len)
    n, nc = t_len // ch, tc_len // cc
    k_scale = LANES ** -0.5

    dl = dl_ref[0]
    gam = jnp.minimum(dl, 0.0) - jnp.log1p(jnp.exp(-jnp.abs(dl)))
    gam_f, gam_b = gam[0:1], gam[1:2]

    def decays(m):
        pos = lax.broadcasted_iota(jnp.int32, (m, LANES), 0).astype(F32)
        return dict(kf=jnp.exp(gam_f * (m - 1.0 - pos)), kb=jnp.exp(gam_b * pos),
                    qf=jnp.exp(gam_f * (pos + 1.0)), qb=jnp.exp(gam_b * (m - pos)),
                    sf=jnp.exp(gam_f * m), sb=jnp.exp(gam_b * m))

    sf_s[...] = jnp.zeros_like(sf_s)
    sb_s[...] = jnp.zeros_like(sb_s)

    dc = decays(cc)

    def ctx_body(c, carry):
        rf = pl.multiple_of(c * cc, cc)
        rb = pl.multiple_of((nc - 1 - c) * cc, cc)
        _ret_state_update(sf_s, ck_ref[0, 0, pl.ds(rf, cc), :] * k_scale, cv_ref[0, 0, pl.ds(rf, cc), :],
                          dc["kf"], dc["sf"])
        _ret_state_update(sb_s, ck_ref[0, 0, pl.ds(rb, cc), :] * k_scale, cv_ref[0, 0, pl.ds(rb, cc), :],
                          dc["kb"], dc["sb"])
        return carry

    lax.fori_loop(0, nc, ctx_body, 0)

    def rope_body(c, carry):
        r = pl.multiple_of(c * ch, ch)
        kr_s[pl.ds(r, ch), :] = _rope(k_ref[0, 0, pl.ds(r, ch), :] * k_scale,
                                      cos_ref[pl.ds(r, ch), :], sin_ref[pl.ds(r, ch), :])
        return carry

    lax.fori_loop(0, n, rope_body, 0, unroll=RET_UNROLL)

    dd = decays(ch)

    def state_body(c, carry):
        cb = n - 1 - c
        rf = pl.multiple_of(c * ch, ch)
        rb = pl.multiple_of(cb * ch, ch)
        st_s[c, :, 0:LANES] = sf_s[...].astype(BF16)
        _ret_state_update(sf_s, kr_s[pl.ds(rf, ch), :], v_ref[0, 0, pl.ds(rf, ch), :], dd["kf"], dd["sf"])
        st_s[cb, :, LANES:2 * LANES] = sb_s[...].astype(BF16)
        _ret_state_update(sb_s, kr_s[pl.ds(rb, ch), :], v_ref[0, 0, pl.ds(rb, ch), :], dd["kb"], dd["sb"])
        return carry

    lax.fori_loop(0, n, state_body, 0, unroll=RET_UNROLL)

    ti = lax.broadcasted_iota(jnp.int32, (ch, ch), 0)
    si = lax.broadcasted_iota(jnp.int32, (ch, ch), 1)
    lag = (ti - si).astype(F32)
    g_f = jnp.broadcast_to(gam_f[:, 0:1], (ch, ch))
    g_b = jnp.broadcast_to(gam_b[:, 0:1], (ch, ch))
    decay = jnp.where(ti > si, jnp.exp(g_f * jnp.maximum(lag, 0.0)),
                      jnp.where(ti < si, jnp.exp(g_b * jnp.maximum(-lag, 0.0)), 2.0))

    def out_body(c, carry):
        r = pl.multiple_of(c * ch, ch)
        q = _rope(q_ref[0, 0, pl.ds(r, ch), :], cos_ref[pl.ds(r, ch), :], sin_ref[pl.ds(r, ch), :])
        k = kr_s[pl.ds(r, ch), :].astype(BF16)
        v = v_ref[0, 0, pl.ds(r, ch), :].astype(BF16)
        a = _dot_nt(q.astype(BF16), k) * decay
        q_in = jnp.concatenate([q * dd["qf"], q * dd["qb"]], axis=1).astype(BF16)
        o = _dot(a.astype(BF16), v) + _dot_nt(q_in, st_s[c])
        oc = o - jnp.mean(o, axis=-1, keepdims=True)
        y = oc * lax.rsqrt(jnp.mean(oc * oc, axis=-1, keepdims=True) + EPS)
        y_ref[0, pl.ds(r, ch), :] = (y * _silu(g_ref[0, 0, pl.ds(r, ch), :])).astype(y_ref.dtype)
        return carry

    lax.fori_loop(0, n, out_body, 0, unroll=RET_UNROLL)


def _ret_call(p_x, p_c, decay_logit, cos, sin_signed, heads, cast_src1, cast_src2):
    b, _, t, _ = p_x.shape
    tc = p_c.shape[2]
    grid = (b, heads)
    assert cast_src1.shape == cast_src2.shape
    n_cast, c_in, c_out, c_shape = _cast_specs(cast_src1, grid)

    def col(off):
        return pl.BlockSpec((1, 1, t, LANES), lambda bi, h: (bi, off * heads + h, 0, 0))

    def ctx_col(off):
        return pl.BlockSpec((1, 1, tc, LANES), lambda bi, h: (bi, off * heads + h, 0, 0))

    return pl.pallas_call(
        functools.partial(_ret_kernel, n_cast),
        grid=grid,
        in_specs=[col(7), col(3), col(4), col(8), ctx_col(3), ctx_col(4),
                  pl.BlockSpec((1, 2, LANES), lambda bi, h: (h, 0, 0)),
                  pl.BlockSpec((t, LANES), lambda bi, h: (0, 0)),
                  pl.BlockSpec((t, LANES), lambda bi, h: (0, 0)),
                  c_in, c_in],
        out_specs=[pl.BlockSpec((1, t, LANES), lambda bi, h: (bi, 0, h)), c_out, c_out],
        out_shape=[jax.ShapeDtypeStruct((b, t, heads * LANES), BF16), c_shape, c_shape],
        scratch_shapes=[pltpu.VMEM((t, LANES), F32),
                        pltpu.VMEM((t // RET_CHUNK, LANES, 2 * LANES), BF16),
                        pltpu.VMEM((LANES, LANES), F32), pltpu.VMEM((LANES, LANES), F32)],
        compiler_params=_params("arbitrary", "arbitrary"),
        name="ret",
    )(p_x, p_x, p_x, p_x, p_c, p_c, decay_logit, cos, sin_signed, cast_src1, cast_src2)


def _merge_kernel(yh_ref, yr_ref, wh_ref, wr_ref, gh_ref, gr_ref, o_ref):
    acc_h = _dot(yh_ref[0], wh_ref[...])
    acc_r = _dot(yr_ref[0], wr_ref[...])
    for j in range(gh_ref.shape[1]):
        sl = slice(j * LANES, (j + 1) * LANES)
        o_ref[0, :, sl] = (jax.nn.sigmoid(gh_ref[0, j]) * acc_h[:, sl]
                           + jax.nn.sigmoid(gr_ref[0, j]) * acc_r[:, sl]).astype(o_ref.dtype)


def _merge_call(y_hg, y_ret, w_bh, w_br, p_x, gate_h_blk, gate_r_blk):
    b, t, kw = y_hg.shape
    d = w_bh.shape[1]
    tm = _tile(t, 1024, SUBLANES)
    tn = _tile(d, 512)
    nb = tn // LANES
    return pl.pallas_call(
        _merge_kernel,
        grid=(b, t // tm, d // tn),
        in_specs=[pl.BlockSpec((1, tm, kw), lambda bi, i, j: (bi, i, 0)),
                  pl.BlockSpec((1, tm, kw), lambda bi, i, j: (bi, i, 0)),
                  pl.BlockSpec((kw, tn), lambda bi, i, j: (0, j)),
                  pl.BlockSpec((kw, tn), lambda bi, i, j: (0, j)),
                  pl.BlockSpec((1, nb, tm, LANES), lambda bi, i, j: (bi, gate_h_blk // nb + j, i, 0)),
                  pl.BlockSpec((1, nb, tm, LANES), lambda bi, i, j: (bi, gate_r_blk // nb + j, i, 0))],
        out_specs=pl.BlockSpec((1, tm, tn), lambda bi, i, j: (bi, i, j)),
        out_shape=jax.ShapeDtypeStruct((b, t, d), BF16),
        compiler_params=_params("parallel", "parallel", "arbitrary"),
        name="merge",
    )(y_hg, y_ret, w_bh, w_br, p_x, p_x)


def _gated_residual_kernel(a_ref, w_ref, x_ref, gate_ref, o_ref):
    o_ref[0] = x_ref[0] + gate_ref[0] * _dot(a_ref[0], w_ref[...])


def _out_proj_call(a, w, x, mod3, gate_blk_of):
    b, t, kw = a.shape
    d = w.shape[1]
    tm = _tile(t, 1024, SUBLANES)
    tn = _tile(d, 512)
    return pl.pallas_call(
        _gated_residual_kernel,
        grid=(b, t // tm, d // tn),
        in_specs=[pl.BlockSpec((1, tm, kw), lambda bi, i, j: (bi, i, 0)),
                  pl.BlockSpec((kw, tn), lambda bi, i, j: (0, j)),
                  pl.BlockSpec((1, tm, tn), lambda bi, i, j: (bi, i, j)),
                  pl.BlockSpec((1, 1, tn), lambda bi, i, j: (bi, 0, gate_blk_of(tn) + j))],
        out_specs=pl.BlockSpec((1, tm, tn), lambda bi, i, j: (bi, i, j)),
        out_shape=jax.ShapeDtypeStruct((b, t, d), F32),
        compiler_params=_params("parallel", "parallel", "arbitrary"),
        name="out_proj",
    )(a, w, x, mod3)


def _ff1_kernel(n_cast, h_ref, w_ref, c_ref, o_ref, co_ref):
    z = jnp.maximum(_dot(h_ref[0], w_ref[...]), 0.0)
    o_ref[0] = (z * z).astype(o_ref.dtype)
    _cast_block(c_ref, co_ref, _grid_step(3), n_cast)


def _ff1_call(h, w, cast_src):
    b, t, d = h.shape
    n = w.shape[1]
    tm = _tile(t, 1024, SUBLANES)
    tn = _tile(n, 512)
    grid = (b, t // tm, n // tn)
    n_cast, c_in, c_out, c_shape = _cast_specs(cast_src, grid)
    return pl.pallas_call(
        functools.partial(_ff1_kernel, n_cast),
        grid=grid,
        in_specs=[pl.BlockSpec((1, tm, d), lambda bi, i, j: (bi, i, 0)),
                  pl.BlockSpec((d, tn), lambda bi, i, j: (0, j)),
                  c_in],
        out_specs=[pl.BlockSpec((1, tm, tn), lambda bi, i, j: (bi, i, j)), c_out],
        out_shape=[jax.ShapeDtypeStruct((b, t, n), BF16), c_shape],
        compiler_params=_params("arbitrary", "arbitrary", "arbitrary"),
        name="ff1",
    )(h, w, cast_src)


def _ff2_kernel(a_ref, w_ref, x_ref, gate_ref, o_ref, acc_ref):
    k = pl.program_id(3)

    @pl.when(k == 0)
    def _():
        acc_ref[...] = jnp.zeros_like(acc_ref)

    acc_ref[...] += _dot(a_ref[0], w_ref[...])

    @pl.when(k == pl.num_programs(3) - 1)
    def _():
        o_ref[0] = x_ref[0] + gate_ref[0] * acc_ref[...]


def _ff2_call(a, w, x, mod3, gate_blk_of):
    b, t, kw = a.shape
    d = w.shape[1]
    tm = _tile(t, 1024, SUBLANES)
    tn = _tile(d, 1024)
    tk = _tile(kw, 2048)
    return pl.pallas_call(
        _ff2_kernel,
        grid=(b, t // tm, d // tn, kw // tk),
        in_specs=[pl.BlockSpec((1, tm, tk), lambda bi, i, j, k: (bi, i, k)),
                  pl.BlockSpec((tk, tn), lambda bi, i, j, k: (k, j)),
                  pl.BlockSpec((1, tm, tn), lambda bi, i, j, k: (bi, i, j)),
                  pl.BlockSpec((1, 1, tn), lambda bi, i, j, k: (bi, 0, gate_blk_of(tn) + j))],
        out_specs=pl.BlockSpec((1, tm, tn), lambda bi, i, j, k: (bi, i, j)),
        out_shape=jax.ShapeDtypeStruct((b, t, d), F32),
        scratch_shapes=[pltpu.VMEM((tm, tn), F32)],
        compiler_params=_params("parallel", "parallel", "parallel", "arbitrary"),
        name="ff2",
    )(a, w, x, mod3)


def _rope_tables(t_len):
    pos = jnp.arange(t_len)
    row = (pos // GRID_W).astype(F32)
    col = (pos % GRID_W).astype(F32)
    n_freq = LANES // 4
    inv_freq = ROPE_BASE ** (-jnp.arange(n_freq, dtype=F32) / n_freq)
    ang = jnp.concatenate([row[:, None] * inv_freq, col[:, None] * inv_freq], axis=-1)
    cos, sin = jnp.cos(ang), jnp.sin(ang)
    cos_rep = jnp.repeat(cos, 2, axis=-1)
    sin_signed = jnp.stack([-sin, sin], axis=-1).reshape(t_len, LANES)
    return cos_rep, sin_signed


def kernel(x, c, ctx, c_ctx, w_mod, b_mod, norm1_g, norm2_g, w_in, hg_lb_logits, hg_norm_g, ret_decay_logit,
           w_branch_hgrn, w_branch_ret, w_out, w_ff1, w_ff2, final_norm_g):
    b, t, d = x.shape
    assert w_mod.shape[0] == 1 and hg_lb_logits.shape[0] == 2, "one layer"
    assert b + 1 <= SUBLANES
    hw = w_branch_hgrn.shape[1]
    heads = hw // LANES
    assert w_branch_ret.shape[1] == hw and w_in.shape[2] == 9 * hw + 2 * d
    n_state = 5 * hw

    c_rows = jnp.zeros((SUBLANES, d), F32).at[:b].set(c).at[b].set(c_ctx)
    mod3 = _mod_call(c_rows, w_mod[0], b_mod[0]).reshape(SUBLANES, 1, 6 * d)

    h_x = _norm_mod_call(x, norm1_g[0], mod3, lambda bi: bi, 0, 1)
    h_c = _norm_mod_call(ctx, norm1_g[0], mod3, lambda bi: b, 0, 1)
    p_x, w_ff1_b = _in_proj_call(h_x, w_in[0], w_in.shape[2], 512, cast_src=w_ff1[0])
    p_c = _in_proj_call(h_c, w_in[0], n_state, 256)

    y_hg, w_out_b = _hgrn_call(p_x, p_c, hg_lb_logits, hg_norm_g[0], heads, w_out[0])
    cos_rep, sin_signed = _rope_tables(t)
    decay_logit = jnp.broadcast_to(ret_decay_logit[0].T[:, :, None], (heads, 2, LANES))
    y_ret, w_bh_b, w_br_b = _ret_call(p_x, p_c, decay_logit, cos_rep, sin_signed, heads,
                                      w_branch_hgrn[0], w_branch_ret[0])

    merged = _merge_call(y_hg, y_ret, w_bh_b, w_br_b, p_x, 9 * heads, 9 * heads + d // LANES)
    x1 = _out_proj_call(merged, w_out_b, x, mod3, lambda tn: 2 * d // tn)
    h2 = _norm_mod_call(x1, norm2_g[0], mod3, lambda bi: bi, 3, 4)
    act, w_ff2_b = _ff1_call(h2, w_ff1_b, w_ff2[0])
    x2 = _ff2_call(act, w_ff2_b, x1, mod3, lambda tn: 5 * d // tn)
    return _final_norm_call(x2, final_norm_g)
```

```python
import functools

import jax
import jax.numpy as jnp
import numpy as np
from jax import lax
from jax.experimental import pallas as pl
from jax.experimental.pallas import tpu as pltpu

F32 = jnp.float32
BF16 = jnp.bfloat16

LANES = 128
SUBLANES = 8
BF16_ROWS = 16
VMEM_LIMIT = 56 * 2 ** 20
VMEM_LIMIT_IN_PROJ = 60 * 2 ** 20
EPS = 1e-6
ROPE_BASE = 10000.0
GRID_W = 64
HG_CHUNK = 64
RET_CHUNK = 256
HG_STATE_UNROLL = 4
HG_OUT_UNROLL = 8
RET_UNROLL = 4


def _params(*sem):
    return pltpu.CompilerParams(dimension_semantics=sem, vmem_limit_bytes=VMEM_LIMIT)


def _tile(n, target, unit=LANES):
    t = min(n, target) // unit * unit
    while n % t:
        t -= unit
    return t


def _silu(z):
    return z * jax.nn.sigmoid(z)


def _dot(a, b):
    return jnp.dot(a, b, preferred_element_type=F32)


def _dot_nt(a, b):
    return lax.dot_general(a, b, (((1,), (1,)), ((), ())), preferred_element_type=F32)


def _dot_tn(a, b):
    return lax.dot_general(a, b, (((0,), (0,)), ((), ())), preferred_element_type=F32)


def _grid_step(rank):
    step = pl.program_id(0)
    for axis in range(1, rank):
        step = step * pl.num_programs(axis) + pl.program_id(axis)
    return step


def _linear_step(ids, sizes):
    step = ids[0]
    for i, n in zip(ids[1:], sizes[1:]):
        step = step * n + i
    return step


def _cast_specs(src, grid):
    rows, cols = src.shape
    steps = int(np.prod(grid))
    n_blocks = 1
    while n_blocks * 2 <= steps and rows % (n_blocks * 2) == 0 and rows // (n_blocks * 2) >= BF16_ROWS:
        n_blocks *= 2

    def idx(*ids):
        return (jnp.minimum(_linear_step(ids, grid), n_blocks - 1), 0)

    spec = pl.BlockSpec((rows // n_blocks, cols), idx)
    return n_blocks, spec, spec, jax.ShapeDtypeStruct(src.shape, BF16)


def _cast_block(src_ref, dst_ref, step, n_blocks):
    @pl.when(step < n_blocks)
    def _():
        dst_ref[...] = src_ref[...].astype(dst_ref.dtype)


def _mod_kernel(c_ref, w_ref, b_ref, o_ref):
    s = _silu(c_ref[...]).astype(BF16)
    o_ref[...] = _dot(s, w_ref[...].astype(BF16)) + b_ref[...]


def _mod_call(c_rows, w_mod, b_mod):
    r, d = c_rows.shape
    n = w_mod.shape[1]
    tn = _tile(n, 512)
    return pl.pallas_call(
        _mod_kernel,
        grid=(n // tn,),
        in_specs=[pl.BlockSpec((r, d), lambda j: (0, 0)),
                  pl.BlockSpec((d, tn), lambda j: (0, j)),
                  pl.BlockSpec((1, tn), lambda j: (0, j))],
        out_specs=pl.BlockSpec((r, tn), lambda j: (0, j)),
        out_shape=jax.ShapeDtypeStruct((r, n), F32),
        compiler_params=_params("parallel"),
        name="mod",
    )(c_rows, w_mod, b_mod.reshape(1, n))


def _norm_mod_kernel(x_ref, g_ref, scale_ref, shift_ref, o_ref):
    x = x_ref[0]
    y = x * lax.rsqrt(jnp.mean(x * x, axis=-1, keepdims=True) + EPS) * g_ref[...]
    o_ref[0] = (y * (1.0 + scale_ref[0]) + shift_ref[0]).astype(o_ref.dtype)


def _norm_mod_call(x, gain, mod3, row_of_batch, shift_blk, scale_blk):
    b, t, d = x.shape
    tm = _tile(t, 256, SUBLANES)
    return pl.pallas_call(
        _norm_mod_kernel,
        grid=(b, t // tm),
        in_specs=[pl.BlockSpec((1, tm, d), lambda bi, i: (bi, i, 0)),
                  pl.BlockSpec((1, d), lambda bi, i: (0, 0)),
                  pl.BlockSpec((1, 1, d), lambda bi, i: (row_of_batch(bi), 0, scale_blk)),
                  pl.BlockSpec((1, 1, d), lambda bi, i: (row_of_batch(bi), 0, shift_blk))],
        out_specs=pl.BlockSpec((1, tm, d), lambda bi, i: (bi, i, 0)),
        out_shape=jax.ShapeDtypeStruct((b, t, d), BF16),
        compiler_params=_params("parallel", "parallel"),
        name="norm_mod",
    )(x, gain.reshape(1, d), mod3, mod3)


def _final_norm_kernel(x_ref, g_ref, o_ref):
    x = x_ref[0]
    o_ref[0] = x * lax.rsqrt(jnp.mean(x * x, axis=-1, keepdims=True) + EPS) * g_ref[...]


def _final_norm_call(x, gain):
    b, t, d = x.shape
    tm = _tile(t, 256, SUBLANES)
    return pl.pallas_call(
        _final_norm_kernel,
        grid=(b, t // tm),
        in_specs=[pl.BlockSpec((1, tm, d), lambda bi, i: (bi, i, 0)),
                  pl.BlockSpec((1, d), lambda bi, i: (0, 0))],
        out_specs=pl.BlockSpec((1, tm, d), lambda bi, i: (bi, i, 0)),
        out_shape=jax.ShapeDtypeStruct((b, t, d), F32),
        compiler_params=_params("parallel", "parallel"),
        name="final_norm",
    )(x, gain.reshape(1, d))


def _in_proj_tile(h_ref, w_ref, o_ref, wb_s):
    @pl.when((pl.program_id(1) == 0) & (pl.program_id(2) == 0))
    def _():
        wb_s[...] = w_ref[...].astype(BF16)

    acc = _dot(h_ref[0], wb_s[...])
    for j in range(o_ref.shape[1]):
        o_ref[0, j] = acc[:, j * LANES:(j + 1) * LANES]


def _in_proj_kernel(h_ref, w_ref, o_ref, wb_s):
    _in_proj_tile(h_ref, w_ref, o_ref, wb_s)


def _in_proj_cast_kernel(n_cast, h_ref, w_ref, c_ref, o_ref, co_ref, wb_s):
    _in_proj_tile(h_ref, w_ref, o_ref, wb_s)
    _cast_block(c_ref, co_ref, _grid_step(3), n_cast)


def _in_proj_call(h, w, n, tm_target, cast_src=None):
    b, t, d = h.shape
    tm = _tile(t, tm_target, SUBLANES)
    tn = _tile(n, 1024)
    grid = (n // tn, b, t // tm)
    in_specs = [pl.BlockSpec((1, tm, d), lambda j, bi, i: (bi, i, 0)),
                pl.BlockSpec((d, tn), lambda j, bi, i: (0, j))]
    out_specs = pl.BlockSpec((1, tn // LANES, tm, LANES), lambda j, bi, i: (bi, j, i, 0))
    out_shape = jax.ShapeDtypeStruct((b, n // LANES, t, LANES), F32)
    common = dict(grid=grid, scratch_shapes=[pltpu.VMEM((d, tn), BF16)],
                  compiler_params=pltpu.CompilerParams(dimension_semantics=("arbitrary",) * 3,
                                                       vmem_limit_bytes=VMEM_LIMIT_IN_PROJ),
                  name="in_proj")
    if cast_src is None:
        return pl.pallas_call(_in_proj_kernel, in_specs=in_specs, out_specs=out_specs, out_shape=out_shape,
                              **common)(h, w)
    n_cast, c_in, c_out, c_shape = _cast_specs(cast_src, grid)
    return pl.pallas_call(functools.partial(_in_proj_cast_kernel, n_cast), in_specs=in_specs + [c_in],
                          out_specs=[out_specs, c_out], out_shape=[out_shape, c_shape], **common)(h, w, cast_src)


def _cumsum_rows(x, reverse):
    n = x.shape[0]
    row = lax.broadcasted_iota(jnp.int32, x.shape, 0)
    d = 1
    while d < n:
        if reverse:
            x = x + jnp.where(row < n - d, pltpu.roll(x, n - d, axis=0), 0.0)
        else:
            x = x + jnp.where(row >= d, pltpu.roll(x, d, axis=0), 0.0)
        d *= 2
    return x


def _block_row_bcast(x, block, offset):
    n, w = x.shape
    if block >= SUBLANES:
        xb = x.reshape(n // block, block, w)
        return jnp.broadcast_to(xb[:, offset:offset + 1, :], xb.shape).reshape(n, w)
    m = lax.broadcasted_iota(jnp.int32, x.shape, 0) & (block - 1)
    out = x
    for mm in range(block):
        if mm != offset:
            out = jnp.where(m == mm, pltpu.roll(x, (mm - offset) % n, axis=0), out)
    return out


def _hgrn_gates(z, lb):
    f = lb + (1.0 - lb) * jax.nn.sigmoid(z)
    return 1.0 - f, jnp.log2(f)


def _state_update(s_ref, k, cum, v, reverse):
    n = cum.shape[0]
    edge = cum[0:1, :] if reverse else cum[n - 1:n, :]
    k_dec = (k * jnp.exp2(edge - cum)).astype(BF16)
    s_ref[...] = s_ref[...] * jnp.exp2(edge) + _dot_tn(v.astype(BF16), k_dec)


def _hgrn_kernel(n_cast, i_ref, ff_ref, fb_ref, q_ref, g_ref, ci_ref, cff_ref, cfb_ref, lbl_ref, ng_ref, c_ref,
                 y_ref, co_ref, kf_s, kb_s, cf_s, cb_s, st_s, sf_s, sb_s):
    _cast_block(c_ref, co_ref, _grid_step(2), n_cast)
    t_len, tc_len = i_ref.shape[2], ci_ref.shape[2]
    ch = HG_CHUNK
    cc = min(ch, tc_len)
    n, nc = t_len // ch, tc_len // cc

    l0, l1 = lbl_ref[0], lbl_ref[1]
    mx = jnp.maximum(l0, l1)
    e0, e1 = jnp.exp(l0 - mx), jnp.exp(l1 - mx)
    lb = e0 / (e0 + e1)
    lb_f, lb_b = lb[0:1], lb[1:2]

    sf_s[...] = jnp.zeros_like(sf_s)
    sb_s[...] = jnp.zeros_like(sb_s)

    def ctx_body(c, carry):
        rf = pl.multiple_of(c * cc, cc)
        rb = pl.multiple_of((nc - 1 - c) * cc, cc)
        k, lf = _hgrn_gates(cff_ref[0, 0, pl.ds(rf, cc), :], lb_f)
        _state_update(sf_s, k, _cumsum_rows(lf, False), ci_ref[0, 0, pl.ds(rf, cc), :], False)
        k, lf = _hgrn_gates(cfb_ref[0, 0, pl.ds(rb, cc), :], lb_b)
        _state_update(sb_s, k, _cumsum_rows(lf, True), ci_ref[0, 0, pl.ds(rb, cc), :], True)
        return carry

    lax.fori_loop(0, nc, ctx_body, 0)

    def state_body(c, carry):
        cb = n - 1 - c
        rf = pl.multiple_of(c * ch, ch)
        rb = pl.multiple_of(cb * ch, ch)
        k, lf = _hgrn_gates(ff_ref[0, 0, pl.ds(rf, ch), :], lb_f)
        cum = _cumsum_rows(lf, False)
        kf_s[pl.ds(rf, ch), :] = k
        cf_s[pl.ds(rf, ch), :] = cum
        st_s[c, :, 0:LANES] = sf_s[...].astype(BF16)
        _state_update(sf_s, k, cum, i_ref[0, 0, pl.ds(rf, ch), :], False)
        k, lf = _hgrn_gates(fb_ref[0, 0, pl.ds(rb, ch), :], lb_b)
        cum = _cumsum_rows(lf, True)
        kb_s[pl.ds(rb, ch), :] = k
        cb_s[pl.ds(rb, ch), :] = cum
        st_s[cb, :, LANES:2 * LANES] = sb_s[...].astype(BF16)
        _state_update(sb_s, k, cum, i_ref[0, 0, pl.ds(rb, ch), :], True)
        return carry

    lax.fori_loop(0, n, state_body, 0, unroll=HG_STATE_UNROLL)

    pair_xor = (lax.broadcasted_iota(jnp.int32, (ch, ch), 0) ^ lax.broadcasted_iota(jnp.int32, (ch, ch), 1))
    row = lax.broadcasted_iota(jnp.int32, (ch, LANES), 0)
    q_scale = LANES ** -0.5

    def out_body(c, carry):
        r = pl.multiple_of(c * ch, ch)
        q = _silu(q_ref[0, 0, pl.ds(r, ch), :]) * q_scale
        kf, kb = kf_s[pl.ds(r, ch), :], kb_s[pl.ds(r, ch), :]
        cf, cb = cf_s[pl.ds(r, ch), :], cb_s[pl.ds(r, ch), :]
        v = i_ref[0, 0, pl.ds(r, ch), :].astype(BF16)
        a = jnp.where(pair_xor == 0, _dot_nt(q.astype(BF16), (kf + kb).astype(BF16)), 0.0)
        odd = (row & 1) != 0
        qm = q * jnp.where(odd, 1.0 - kf, 1.0 - kb)
        a = jnp.where(pair_xor == 1, _dot_nt(qm.astype(BF16), jnp.where(odd, kb, kf).astype(BF16)), a)
        h, log_h = 2, 1
        while h < ch:
            second = (row & h) != 0
            df = cf - _block_row_bcast(cf, 2 * h, h - 1)
            db = cb - _block_row_bcast(cb, 2 * h, h)
            qm = q * jnp.exp2(jnp.where(second, df, db))
            km = jnp.where(second, kb, kf) * jnp.exp2(-jnp.where(second, db, df))
            a = jnp.where((pair_xor >> log_h) == 1, _dot_nt(qm.astype(BF16), km.astype(BF16)), a)
            h, log_h = 2 * h, log_h + 1
        q_in = jnp.concatenate([q * jnp.exp2(cf), q * jnp.exp2(cb)], axis=1).astype(BF16)
        o = _dot(a.astype(BF16), v) + _dot_nt(q_in, st_s[c])
        y = o * lax.rsqrt(jnp.mean(o * o, axis=-1, keepdims=True) + EPS) * ng_ref[...]
        y_ref[0, pl.ds(r, ch), :] = (y * _silu(g_ref[0, 0, pl.ds(r, ch), :])).astype(y_ref.dtype)
        return carry

    lax.fori_loop(0, n, out_body, 0, unroll=HG_OUT_UNROLL)


def _hgrn_call(p_x, p_c, lb_logits, norm_g, heads, cast_src):
    b, _, t, _ = p_x.shape
    tc = p_c.shape[2]
    grid = (b, heads)
    n_cast, c_in, c_out, c_shape = _cast_specs(cast_src, grid)

    def col(off):
        return pl.BlockSpec((1, 1, t, LANES), lambda bi, h: (bi, off * heads + h, 0, 0))

    def ctx_col(off):
        return pl.BlockSpec((1, 1, tc, LANES), lambda bi, h: (bi, off * heads + h, 0, 0))

    return pl.pallas_call(
        functools.partial(_hgrn_kernel, n_cast),
        grid=grid,
        in_specs=[col(0), col(1), col(2), col(5), col(6), ctx_col(0), ctx_col(1), ctx_col(2),
                  pl.BlockSpec((2, 2, LANES), lambda bi, h: (0, 0, h)),
                  pl.BlockSpec((1, LANES), lambda bi, h: (0, 0)),
                  c_in],
        out_specs=[pl.BlockSpec((1, t, LANES), lambda bi, h: (bi, 0, h)), c_out],
        out_shape=[jax.ShapeDtypeStruct((b, t, heads * LANES), BF16), c_shape],
        scratch_shapes=[pltpu.VMEM((t, LANES), F32), pltpu.VMEM((t, LANES), F32),
                        pltpu.VMEM((t, LANES), F32), pltpu.VMEM((t, LANES), F32),
                        pltpu.VMEM((t // HG_CHUNK, LANES, 2 * LANES), BF16),
                        pltpu.VMEM((LANES, LANES), F32), pltpu.VMEM((LANES, LANES), F32)],
        compiler_params=_params("arbitrary", "arbitrary"),
        name="hgrn",
    )(p_x, p_x, p_x, p_x, p_x, p_c, p_c, p_c, lb_logits, norm_g.reshape(1, LANES), cast_src)


def _rope(a, cos, sin_signed, swap=None):
    if swap is None:
        lane = lax.broadcasted_iota(jnp.int32, a.shape, 1)
        partner = jnp.where((lane & 1) == 0, pltpu.roll(a, LANES - 1, axis=1), pltpu.roll(a, 1, axis=1))
    else:
        partner = _dot(a.astype(BF16), swap)
    return a * cos + partner * sin_signed


def _ret_state_update(s_ref, k, v, k_decay, s_decay):
    s_ref[...] = s_ref[...] * s_decay + _dot_tn(v.astype(BF16), (k * k_decay).astype(BF16))


def _ret_kernel(n_cast, q_ref, k_ref, v_ref, g_ref, ck_ref, cv_ref, dl_ref, cos_ref, sin_ref, sw_ref, c1_ref, c2_ref,
                y_ref, co1_ref, co2_ref, kr_s, st_s, sf_s, sb_s):
    _cast_block(c1_ref, co1_ref, _grid_step(2), n_cast)
    _cast_block(c2_ref, co2_ref, _grid_step(2), n_cast)
    t_len, tc_len = q_ref.shape[2], ck_ref.shape[2]
    ch = RET_CHUNK
    cc = min(ch, tc_len)
    n, nc = t_len // ch, tc_len // cc
    k_scale = LANES ** -0.5

    dl = dl_ref[0]
    gam = jnp.minimum(dl, 0.0) - jnp.log1p(jnp.exp(-jnp.abs(dl)))
    gam_f, gam_b = gam[0:1], gam[1:2]

    def decays(m):
        pos = lax.broadcasted_iota(jnp.int32, (m, LANES), 0).astype(F32)
        return dict(kf=jnp.exp(gam_f * (m - 1.0 - pos)), kb=jnp.exp(gam_b * pos),
                    qf=jnp.exp(gam_f * (pos + 1.0)), qb=jnp.exp(gam_b * (m - pos)),
                    sf=jnp.exp(gam_f * m), sb=jnp.exp(gam_b * m))

    sf_s[...] = jnp.zeros_like(sf_s)
    sb_s[...] = jnp.zeros_like(sb_s)

    dc = decays(cc)

    def ctx_body(c, carry):
        rf = pl.multiple_of(c * cc, cc)
        rb = pl.multiple_of((nc - 1 - c) * cc, cc)
        _ret_state_update(sf_s, ck_ref[0, 0, pl.ds(rf, cc), :] * k_scale, cv_ref[0, 0, pl.ds(rf, cc), :],
                          dc["kf"], dc["sf"])
        _ret_state_update(sb_s, ck_ref[0, 0, pl.ds(rb, cc), :] * k_scale, cv_ref[0, 0, pl.ds(rb, cc), :],
                          dc["kb"], dc["sb"])
        return carry

    lax.fori_loop(0, nc, ctx_body, 0)

    def rope_body(c, carry):
        r = pl.multiple_of(c * ch, ch)
        kr_s[pl.ds(r, ch), :] = _rope(k_ref[0, 0, pl.ds(r, ch), :] * k_scale,
                                      cos_ref[pl.ds(r, ch), :], sin_ref[pl.ds(r, ch), :], sw_ref[...])
        return carry

    lax.fori_loop(0, n, rope_body, 0, unroll=RET_UNROLL)

    dd = decays(ch)

    def state_body(c, carry):
        cb = n - 1 - c
        rf = pl.multiple_of(c * ch, ch)
        rb = pl.multiple_of(cb * ch, ch)
        st_s[c, :, 0:LANES] = sf_s[...].astype(BF16)
        _ret_state_update(sf_s, kr_s[pl.ds(rf, ch), :], v_ref[0, 0, pl.ds(rf, ch), :], dd["kf"], dd["sf"])
        st_s[cb, :, LANES:2 * LANES] = sb_s[...].astype(BF16)
        _ret_state_update(sb_s, kr_s[pl.ds(rb, ch), :], v_ref[0, 0, pl.ds(rb, ch), :], dd["kb"], dd["sb"])
        return carry

    lax.fori_loop(0, n, state_body, 0, unroll=RET_UNROLL)

    ti = lax.broadcasted_iota(jnp.int32, (ch, ch), 0)
    si = lax.broadcasted_iota(jnp.int32, (ch, ch), 1)
    lag = (ti - si).astype(F32)
    g_f = jnp.broadcast_to(gam_f[:, 0:1], (ch, ch))
    g_b = jnp.broadcast_to(gam_b[:, 0:1], (ch, ch))
    decay = jnp.where(ti > si, jnp.exp(g_f * jnp.maximum(lag, 0.0)),
                      jnp.where(ti < si, jnp.exp(g_b * jnp.maximum(-lag, 0.0)), 2.0))

    def out_body(c, carry):
        r = pl.multiple_of(c * ch, ch)
        q = _rope(q_ref[0, 0, pl.ds(r, ch), :], cos_ref[pl.ds(r, ch), :], sin_ref[pl.ds(r, ch), :])
        k = kr_s[pl.ds(r, ch), :].astype(BF16)
        v = v_ref[0, 0, pl.ds(r, ch), :].astype(BF16)
        a = _dot_nt(q.astype(BF16), k) * decay
        q_in = jnp.concatenate([q * dd["qf"], q * dd["qb"]], axis=1).astype(BF16)
        o = _dot(a.astype(BF16), v) + _dot_nt(q_in, st_s[c])
        oc = o - jnp.mean(o, axis=-1, keepdims=True)
        y = oc * lax.rsqrt(jnp.mean(oc * oc, axis=-1, keepdims=True) + EPS)
        y_ref[0, pl.ds(r, ch), :] = (y * _silu(g_ref[0, 0, pl.ds(r, ch), :])).astype(y_ref.dtype)
        return carry

    lax.fori_loop(0, n, out_body, 0, unroll=RET_UNROLL)


def _ret_call(p_x, p_c, decay_logit, cos, sin_signed, heads, cast_src1, cast_src2):
    b, _, t, _ = p_x.shape
    tc = p_c.shape[2]
    grid = (b, heads)
    assert cast_src1.shape == cast_src2.shape
    n_cast, c_in, c_out, c_shape = _cast_specs(cast_src1, grid)
    swap = jnp.asarray(np.eye(LANES, dtype=np.float32)[:, np.arange(LANES) ^ 1], BF16)

    def col(off):
        return pl.BlockSpec((1, 1, t, LANES), lambda bi, h: (bi, off * heads + h, 0, 0))

    def ctx_col(off):
        return pl.BlockSpec((1, 1, tc, LANES), lambda bi, h: (bi, off * heads + h, 0, 0))

    return pl.pallas_call(
        functools.partial(_ret_kernel, n_cast),
        grid=grid,
        in_specs=[col(7), col(3), col(4), col(8), ctx_col(3), ctx_col(4),
                  pl.BlockSpec((1, 2, LANES), lambda bi, h: (h, 0, 0)),
                  pl.BlockSpec((t, LANES), lambda bi, h: (0, 0)),
                  pl.BlockSpec((t, LANES), lambda bi, h: (0, 0)),
                  pl.BlockSpec((LANES, LANES), lambda bi, h: (0, 0)),
                  c_in, c_in],
        out_specs=[pl.BlockSpec((1, t, LANES), lambda bi, h: (bi, 0, h)), c_out, c_out],
        out_shape=[jax.ShapeDtypeStruct((b, t, heads * LANES), BF16), c_shape, c_shape],
        scratch_shapes=[pltpu.VMEM((t, LANES), F32),
                        pltpu.VMEM((t // RET_CHUNK, LANES, 2 * LANES), BF16),
                        pltpu.VMEM((LANES, LANES), F32), pltpu.VMEM((LANES, LANES), F32)],
        compiler_params=_params("arbitrary", "arbitrary"),
        name="ret",
    )(p_x, p_x, p_x, p_x, p_c, p_c, decay_logit, cos, sin_signed, swap, cast_src1, cast_src2)


def _merge_kernel(yh_ref, yr_ref, wh_ref, wr_ref, gh_ref, gr_ref, o_ref):
    acc_h = _dot(yh_ref[0], wh_ref[...])
    acc_r = _dot(yr_ref[0], wr_ref[...])
    for j in range(gh_ref.shape[1]):
        sl = slice(j * LANES, (j + 1) * LANES)
        o_ref[0, :, sl] = (jax.nn.sigmoid(gh_ref[0, j]) * acc_h[:, sl]
                           + jax.nn.sigmoid(gr_ref[0, j]) * acc_r[:, sl]).astype(o_ref.dtype)


def _merge_call(y_hg, y_ret, w_bh, w_br, p_x, gate_h_blk, gate_r_blk):
    b, t, kw = y_hg.shape
    d = w_bh.shape[1]
    tm = _tile(t, 1024, SUBLANES)
    tn = _tile(d, 512)
    nb = tn // LANES
    return pl.pallas_call(
        _merge_kernel,
        grid=(b, t // tm, d // tn),
        in_specs=[pl.BlockSpec((1, tm, kw), lambda bi, i, j: (bi, i, 0)),
                  pl.BlockSpec((1, tm, kw), lambda bi, i, j: (bi, i, 0)),
                  pl.BlockSpec((kw, tn), lambda bi, i, j: (0, j)),
                  pl.BlockSpec((kw, tn), lambda bi, i, j: (0, j)),
                  pl.BlockSpec((1, nb, tm, LANES), lambda bi, i, j: (bi, gate_h_blk // nb + j, i, 0)),
                  pl.BlockSpec((1, nb, tm, LANES), lambda bi, i, j: (bi, gate_r_blk // nb + j, i, 0))],
        out_specs=pl.BlockSpec((1, tm, tn), lambda bi, i, j: (bi, i, j)),
        out_shape=jax.ShapeDtypeStruct((b, t, d), BF16),
        compiler_params=_params("parallel", "parallel", "arbitrary"),
        name="merge",
    )(y_hg, y_ret, w_bh, w_br, p_x, p_x)


def _gated_residual_kernel(a_ref, w_ref, x_ref, gate_ref, o_ref):
    o_ref[0] = x_ref[0] + gate_ref[0] * _dot(a_ref[0], w_ref[...])


def _out_proj_call(a, w, x, mod3, gate_blk_of):
    b, t, kw = a.shape
    d = w.shape[1]
    tm = _tile(t, 1024, SUBLANES)
    tn = _tile(d, 512)
    return pl.pallas_call(
        _gated_residual_kernel,
        grid=(b, t // tm, d // tn),
        in_specs=[pl.BlockSpec((1, tm, kw), lambda bi, i, j: (bi, i, 0)),
                  pl.BlockSpec((kw, tn), lambda bi, i, j: (0, j)),
                  pl.BlockSpec((1, tm, tn), lambda bi, i, j: (bi, i, j)),
                  pl.BlockSpec((1, 1, tn), lambda bi, i, j: (bi, 0, gate_blk_of(tn) + j))],
        out_specs=pl.BlockSpec((1, tm, tn), lambda bi, i, j: (bi, i, j)),
        out_shape=jax.ShapeDtypeStruct((b, t, d), F32),
        compiler_params=_params("parallel", "parallel", "arbitrary"),
        name="out_proj",
    )(a, w, x, mod3)


def _ff1_kernel(n_cast, h_ref, w_ref, c_ref, o_ref, co_ref):
    z = jnp.maximum(_dot(h_ref[0], w_ref[...]), 0.0)
    o_ref[0] = (z * z).astype(o_ref.dtype)
    _cast_block(c_ref, co_ref, _grid_step(3), n_cast)


def _ff1_call(h, w, cast_src):
    b, t, d = h.shape
    n = w.shape[1]
    tm = _tile(t, 1024, SUBLANES)
    tn = _tile(n, 512)
    grid = (b, t // tm, n // tn)
    n_cast, c_in, c_out, c_shape = _cast_specs(cast_src, grid)
    return pl.pallas_call(
        functools.partial(_ff1_kernel, n_cast),
        grid=grid,
        in_specs=[pl.BlockSpec((1, tm, d), lambda bi, i, j: (bi, i, 0)),
                  pl.BlockSpec((d, tn), lambda bi, i, j: (0, j)),
                  c_in],
        out_specs=[pl.BlockSpec((1, tm, tn), lambda bi, i, j: (bi, i, j)), c_out],
        out_shape=[jax.ShapeDtypeStruct((b, t, n), BF16), c_shape],
        compiler_params=_params("arbitrary", "arbitrary", "arbitrary"),
        name="ff1",
    )(h, w, cast_src)


def _ff2_kernel(a_ref, w_ref, x_ref, gate_ref, o_ref, acc_ref):
    k = pl.program_id(3)

    @pl.when(k == 0)
    def _():
        acc_ref[...] = jnp.zeros_like(acc_ref)

    acc_ref[...] += _dot(a_ref[0], w_ref[...])

    @pl.when(k == pl.num_programs(3) - 1)
    def _():
        o_ref[0] = x_ref[0] + gate_ref[0] * acc_ref[...]


def _ff2_call(a, w, x, mod3, gate_blk_of):
    b, t, kw = a.shape
    d = w.shape[1]
    tm = _tile(t, 1024, SUBLANES)
    tn = _tile(d, 1024)
    tk = _tile(kw, 2048)
    return pl.pallas_call(
        _ff2_kernel,
        grid=(b, t // tm, d // tn, kw // tk),
        in_specs=[pl.BlockSpec((1, tm, tk), lambda bi, i, j, k: (bi, i, k)),
                  pl.BlockSpec((tk, tn), lambda bi, i, j, k: (k, j)),
                  pl.BlockSpec((1, tm, tn), lambda bi, i, j, k: (bi, i, j)),
                  pl.BlockSpec((1, 1, tn), lambda bi, i, j, k: (bi, 0, gate_blk_of(tn) + j))],
        out_specs=pl.BlockSpec((1, tm, tn), lambda bi, i, j, k: (bi, i, j)),
        out_shape=jax.ShapeDtypeStruct((b, t, d), F32),
        scratch_shapes=[pltpu.VMEM((tm, tn), F32)],
        compiler_params=_params("parallel", "parallel", "parallel", "arbitrary"),
        name="ff2",
    )(a, w, x, mod3)


def _rope_tables(t_len):
    pos = jnp.arange(t_len)
    row = (pos // GRID_W).astype(F32)
    col = (pos % GRID_W).astype(F32)
    n_freq = LANES // 4
    inv_freq = ROPE_BASE ** (-jnp.arange(n_freq, dtype=F32) / n_freq)
    ang = jnp.concatenate([row[:, None] * inv_freq, col[:, None] * inv_freq], axis=-1)
    cos, sin = jnp.cos(ang), jnp.sin(ang)
    cos_rep = jnp.repeat(cos, 2, axis=-1)
    sin_signed = jnp.stack([-sin, sin], axis=-1).reshape(t_len, LANES)
    return cos_rep, sin_signed


def kernel(x, c, ctx, c_ctx, w_mod, b_mod, norm1_g, norm2_g, w_in, hg_lb_logits, hg_norm_g, ret_decay_logit,
           w_branch_hgrn, w_branch_ret, w_out, w_ff1, w_ff2, final_norm_g):
    b, t, d = x.shape
    assert w_mod.shape[0] == 1 and hg_lb_logits.shape[0] == 2, "one layer"
    assert b + 1 <= SUBLANES
    hw = w_branch_hgrn.shape[1]
    heads = hw // LANES
    assert w_branch_ret.shape[1] == hw and w_in.shape[2] == 9 * hw + 2 * d
    n_state = 5 * hw

    c_rows = jnp.zeros((SUBLANES, d), F32).at[:b].set(c).at[b].set(c_ctx)
    mod3 = _mod_call(c_rows, w_mod[0], b_mod[0]).reshape(SUBLANES, 1, 6 * d)

    h_x = _norm_mod_call(x, norm1_g[0], mod3, lambda bi: bi, 0, 1)
    h_c = _norm_mod_call(ctx, norm1_g[0], mod3, lambda bi: b, 0, 1)
    p_x, w_ff1_b = _in_proj_call(h_x, w_in[0], w_in.shape[2], 512, cast_src=w_ff1[0])
    p_c = _in_proj_call(h_c, w_in[0], n_state, 256)

    y_hg, w_out_b = _hgrn_call(p_x, p_c, hg_lb_logits, hg_norm_g[0], heads, w_out[0])
    cos_rep, sin_signed = _rope_tables(t)
    decay_logit = jnp.broadcast_to(ret_decay_logit[0].T[:, :, None], (heads, 2, LANES))
    y_ret, w_bh_b, w_br_b = _ret_call(p_x, p_c, decay_logit, cos_rep, sin_signed, heads,
                                      w_branch_hgrn[0], w_branch_ret[0])

    merged = _merge_call(y_hg, y_ret, w_bh_b, w_br_b, p_x, 9 * heads, 9 * heads + d // LANES)
    x1 = _out_proj_call(merged, w_out_b, x, mod3, lambda tn: 2 * d // tn)
    h2 = _norm_mod_call(x1, norm2_g[0], mod3, lambda bi: bi, 3, 4)
    act, w_ff2_b = _ff1_call(h2, w_ff1_b, w_ff2[0])
    x2 = _ff2_call(act, w_ff2_b, x1, mod3, lambda tn: 5 * d // tn)
    return _final_norm_call(x2, final_norm_g)
```

```python
import functools

import jax
import jax.numpy as jnp
import numpy as np
from jax import lax
from jax.experimental import pallas as pl
from jax.experimental.pallas import tpu as pltpu

F32 = jnp.float32
BF16 = jnp.bfloat16

LANES = 128
SUBLANES = 8
BF16_ROWS = 16
VMEM_LIMIT = 60 * 2 ** 20
EPS = 1e-6
ROPE_BASE = 10000.0
GRID_W = 64
HG_CHUNK = 64
RET_CHUNK = 256
HG_STATE_UNROLL = 4
HG_OUT_UNROLL = 8
RET_UNROLL = 4
W_IN_SPLIT = 4


def _params(*sem):
    return pltpu.CompilerParams(dimension_semantics=sem, vmem_limit_bytes=VMEM_LIMIT)


def _tile(n, target, unit=LANES):
    t = min(n, target) // unit * unit
    while n % t:
        t -= unit
    return t


def _silu(z):
    return z * jax.nn.sigmoid(z)


def _dot(a, b):
    return jnp.dot(a, b, preferred_element_type=F32)


def _dot_nt(a, b):
    return lax.dot_general(a, b, (((1,), (1,)), ((), ())), preferred_element_type=F32)


def _dot_tn(a, b):
    return lax.dot_general(a, b, (((0,), (0,)), ((), ())), preferred_element_type=F32)


def _grid_step(rank):
    step = pl.program_id(0)
    for axis in range(1, rank):
        step = step * pl.num_programs(axis) + pl.program_id(axis)
    return step


def _linear_step(ids, sizes):
    step = ids[0]
    for i, n in zip(ids[1:], sizes[1:]):
        step = step * n + i
    return step


def _cast_specs(src, grid):
    rows, cols = src.shape
    steps = int(np.prod(grid))
    n_blocks = 1
    while n_blocks * 2 <= steps and rows % (n_blocks * 2) == 0 and rows // (n_blocks * 2) >= BF16_ROWS:
        n_blocks *= 2

    def idx(*ids):
        return (jnp.minimum(_linear_step(ids, grid), n_blocks - 1), 0)

    spec = pl.BlockSpec((rows // n_blocks, cols), idx)
    return n_blocks, spec, spec, jax.ShapeDtypeStruct(src.shape, BF16)


def _cast_block(src_ref, dst_ref, step, n_blocks):
    @pl.when(step < n_blocks)
    def _():
        dst_ref[...] = src_ref[...].astype(dst_ref.dtype)


def _mod_kernel(c_ref, w_ref, b_ref, o_ref):
    s = _silu(c_ref[...]).astype(BF16)
    o_ref[...] = _dot(s, w_ref[...].astype(BF16)) + b_ref[...]


def _mod_call(c_rows, w_mod, b_mod):
    r, d = c_rows.shape
    n = w_mod.shape[1]
    tn = _tile(n, 512)
    return pl.pallas_call(
        _mod_kernel,
        grid=(n // tn,),
        in_specs=[pl.BlockSpec((r, d), lambda j: (0, 0)),
                  pl.BlockSpec((d, tn), lambda j: (0, j)),
                  pl.BlockSpec((1, tn), lambda j: (0, j))],
        out_specs=pl.BlockSpec((r, tn), lambda j: (0, j)),
        out_shape=jax.ShapeDtypeStruct((r, n), F32),
        compiler_params=_params("parallel"),
        name="mod",
    )(c_rows, w_mod, b_mod.reshape(1, n))


def _norm_mod_kernel(x_ref, g_ref, scale_ref, shift_ref, o_ref):
    x = x_ref[0]
    y = x * lax.rsqrt(jnp.mean(x * x, axis=-1, keepdims=True) + EPS) * g_ref[...]
    o_ref[0] = (y * (1.0 + scale_ref[0]) + shift_ref[0]).astype(o_ref.dtype)


def _norm_mod_call(x, gain, mod3, row_of_batch, shift_blk, scale_blk):
    b, t, d = x.shape
    tm = _tile(t, 512, SUBLANES)
    return pl.pallas_call(
        _norm_mod_kernel,
        grid=(b, t // tm),
        in_specs=[pl.BlockSpec((1, tm, d), lambda bi, i: (bi, i, 0)),
                  pl.BlockSpec((1, d), lambda bi, i: (0, 0)),
                  pl.BlockSpec((1, 1, d), lambda bi, i: (row_of_batch(bi), 0, scale_blk)),
                  pl.BlockSpec((1, 1, d), lambda bi, i: (row_of_batch(bi), 0, shift_blk))],
        out_specs=pl.BlockSpec((1, tm, d), lambda bi, i: (bi, i, 0)),
        out_shape=jax.ShapeDtypeStruct((b, t, d), BF16),
        compiler_params=_params("parallel", "parallel"),
        name="norm_mod",
    )(x, gain.reshape(1, d), mod3, mod3)


def _final_norm_kernel(x_ref, g_ref, o_ref):
    x = x_ref[0]
    o_ref[0] = x * lax.rsqrt(jnp.mean(x * x, axis=-1, keepdims=True) + EPS) * g_ref[...]


def _final_norm_call(x, gain):
    b, t, d = x.shape
    tm = _tile(t, 512, SUBLANES)
    return pl.pallas_call(
        _final_norm_kernel,
        grid=(b, t // tm),
        in_specs=[pl.BlockSpec((1, tm, d), lambda bi, i: (bi, i, 0)),
                  pl.BlockSpec((1, d), lambda bi, i: (0, 0))],
        out_specs=pl.BlockSpec((1, tm, d), lambda bi, i: (bi, i, 0)),
        out_shape=jax.ShapeDtypeStruct((b, t, d), F32),
        compiler_params=_params("parallel", "parallel"),
        name="final_norm",
    )(x, gain.reshape(1, d))


def _in_proj_tile(h_ref, w_refs, o_ref, wb_s):
    @pl.when((pl.program_id(1) == 0) & (pl.program_id(2) == 0))
    def _():
        rows = w_refs[0].shape[0]
        for q, w_ref in enumerate(w_refs):
            wb_s[q * rows:(q + 1) * rows, :] = w_ref[...].astype(BF16)

    acc = _dot(h_ref[0], wb_s[...])
    for j in range(o_ref.shape[1]):
        o_ref[0, j] = acc[:, j * LANES:(j + 1) * LANES]


def _in_proj_kernel(h_ref, *refs):
    w_refs, (o_ref, wb_s) = refs[:W_IN_SPLIT], refs[W_IN_SPLIT:]
    _in_proj_tile(h_ref, w_refs, o_ref, wb_s)


def _in_proj_cast_kernel(n_cast, h_ref, *refs):
    w_refs, (c_ref, o_ref, co_ref, wb_s) = refs[:W_IN_SPLIT], refs[W_IN_SPLIT:]
    _in_proj_tile(h_ref, w_refs, o_ref, wb_s)
    _cast_block(c_ref, co_ref, _grid_step(3), n_cast)


def _in_proj_call(h, w, n, tm_target, cast_src=None):
    b, t, d = h.shape
    tm = _tile(t, tm_target, SUBLANES)
    tn = _tile(n, 1024)
    nj, ni = n // tn, t // tm
    grid = (nj, b, ni)
    dk = d // W_IN_SPLIT
    period = b * ni

    def w_spec(q):
        switch = 1 + q * max(1, (period - 2) // W_IN_SPLIT)
        return pl.BlockSpec((dk, tn), lambda j, bi, i: (q, jnp.minimum(j + (bi * ni + i >= switch), nj - 1)))

    in_specs = [pl.BlockSpec((1, tm, d), lambda j, bi, i: (bi, i, 0))] + [w_spec(q) for q in range(W_IN_SPLIT)]
    out_specs = pl.BlockSpec((1, tn // LANES, tm, LANES), lambda j, bi, i: (bi, j, i, 0))
    out_shape = jax.ShapeDtypeStruct((b, n // LANES, t, LANES), F32)
    common = dict(grid=grid, scratch_shapes=[pltpu.VMEM((d, tn), BF16)],
                  compiler_params=_params("arbitrary", "arbitrary", "arbitrary"),
                  name="in_proj")
    ws = (w,) * W_IN_SPLIT
    if cast_src is None:
        return pl.pallas_call(_in_proj_kernel, in_specs=in_specs, out_specs=out_specs, out_shape=out_shape,
                              **common)(h, *ws)
    n_cast, c_in, c_out, c_shape = _cast_specs(cast_src, grid)
    return pl.pallas_call(functools.partial(_in_proj_cast_kernel, n_cast), in_specs=in_specs + [c_in],
                          out_specs=[out_specs, c_out], out_shape=[out_shape, c_shape],
                          **common)(h, *ws, cast_src)


def _cumsum_rows(x, reverse):
    n = x.shape[0]
    row = lax.broadcasted_iota(jnp.int32, x.shape, 0)
    d = 1
    while d < n:
        if reverse:
            x = x + jnp.where(row < n - d, pltpu.roll(x, n - d, axis=0), 0.0)
        else:
            x = x + jnp.where(row >= d, pltpu.roll(x, d, axis=0), 0.0)
        d *= 2
    return x


def _block_row_bcast(x, block, offset):
    n, w = x.shape
    if block >= SUBLANES:
        xb = x.reshape(n // block, block, w)
        return jnp.broadcast_to(xb[:, offset:offset + 1, :], xb.shape).reshape(n, w)
    m = lax.broadcasted_iota(jnp.int32, x.shape, 0) & (block - 1)
    out = x
    for mm in range(block):
        if mm != offset:
            out = jnp.where(m == mm, pltpu.roll(x, (mm - offset) % n, axis=0), out)
    return out


def _hgrn_gates(z, lb):
    f = lb + (1.0 - lb) * jax.nn.sigmoid(z)
    return 1.0 - f, jnp.log2(f)


def _state_update(s_ref, k, cum, v, reverse):
    n = cum.shape[0]
    edge = cum[0:1, :] if reverse else cum[n - 1:n, :]
    k_dec = (k * jnp.exp2(edge - cum)).astype(BF16)
    s_ref[...] = s_ref[...] * jnp.exp2(edge) + _dot_tn(v.astype(BF16), k_dec)


def _hgrn_kernel(n_cast, i_ref, ff_ref, fb_ref, q_ref, g_ref, ci_ref, cff_ref, cfb_ref, lbl_ref, ng_ref, c_ref,
                 y_ref, co_ref, kf_s, kb_s, cf_s, cb_s, st_s, sf_s, sb_s):
    _cast_block(c_ref, co_ref, _grid_step(2), n_cast)
    t_len, tc_len = i_ref.shape[2], ci_ref.shape[2]
    ch = HG_CHUNK
    cc = min(ch, tc_len)
    n, nc = t_len // ch, tc_len // cc

    l0, l1 = lbl_ref[0], lbl_ref[1]
    mx = jnp.maximum(l0, l1)
    e0, e1 = jnp.exp(l0 - mx), jnp.exp(l1 - mx)
    lb = e0 / (e0 + e1)
    lb_f, lb_b = lb[0:1], lb[1:2]

    sf_s[...] = jnp.zeros_like(sf_s)
    sb_s[...] = jnp.zeros_like(sb_s)

    def ctx_body(c, carry):
        rf = pl.multiple_of(c * cc, cc)
        rb = pl.multiple_of((nc - 1 - c) * cc, cc)
        k, lf = _hgrn_gates(cff_ref[0, 0, pl.ds(rf, cc), :], lb_f)
        _state_update(sf_s, k, _cumsum_rows(lf, False), ci_ref[0, 0, pl.ds(rf, cc), :], False)
        k, lf = _hgrn_gates(cfb_ref[0, 0, pl.ds(rb, cc), :], lb_b)
        _state_update(sb_s, k, _cumsum_rows(lf, True), ci_ref[0, 0, pl.ds(rb, cc), :], True)
        return carry

    lax.fori_loop(0, nc, ctx_body, 0)

    def state_body(c, carry):
        cb = n - 1 - c
        rf = pl.multiple_of(c * ch, ch)
        rb = pl.multiple_of(cb * ch, ch)
        k, lf = _hgrn_gates(ff_ref[0, 0, pl.ds(rf, ch), :], lb_f)
        cum = _cumsum_rows(lf, False)
        kf_s[pl.ds(rf, ch), :] = k
        cf_s[pl.ds(rf, ch), :] = cum
        st_s[c, :, 0:LANES] = sf_s[...].astype(BF16)
        _state_update(sf_s, k, cum, i_ref[0, 0, pl.ds(rf, ch), :], False)
        k, lf = _hgrn_gates(fb_ref[0, 0, pl.ds(rb, ch), :], lb_b)
        cum = _cumsum_rows(lf, True)
        kb_s[pl.ds(rb, ch), :] = k
        cb_s[pl.ds(rb, ch), :] = cum
        st_s[cb, :, LANES:2 * LANES] = sb_s[...].astype(BF16)
        _state_update(sb_s, k, cum, i_ref[0, 0, pl.ds(rb, ch), :], True)
        return carry

    lax.fori_loop(0, n, state_body, 0, unroll=HG_STATE_UNROLL)

    pair_xor = (lax.broadcasted_iota(jnp.int32, (ch, ch), 0) ^ lax.broadcasted_iota(jnp.int32, (ch, ch), 1))
    row = lax.broadcasted_iota(jnp.int32, (ch, LANES), 0)
    q_scale = LANES ** -0.5

    def out_body(c, carry):
        r = pl.multiple_of(c * ch, ch)
        q = _silu(q_ref[0, 0, pl.ds(r, ch), :]) * q_scale
        kf, kb = kf_s[pl.ds(r, ch), :], kb_s[pl.ds(r, ch), :]
        cf, cb = cf_s[pl.ds(r, ch), :], cb_s[pl.ds(r, ch), :]
        v = i_ref[0, 0, pl.ds(r, ch), :].astype(BF16)
        a = jnp.where(pair_xor == 0, _dot_nt(q.astype(BF16), (kf + kb).astype(BF16)), 0.0)
        odd = (row & 1) != 0
        qm = q * jnp.where(odd, 1.0 - kf, 1.0 - kb)
        a = jnp.where(pair_xor == 1, _dot_nt(qm.astype(BF16), jnp.where(odd, kb, kf).astype(BF16)), a)
        h, log_h = 2, 1
        while h < ch:
            second = (row & h) != 0
            df = cf - _block_row_bcast(cf, 2 * h, h - 1)
            db = cb - _block_row_bcast(cb, 2 * h, h)
            qm = q * jnp.exp2(jnp.where(second, df, db))
            km = jnp.where(second, kb, kf) * jnp.exp2(-jnp.where(second, db, df))
            a = jnp.where((pair_xor >> log_h) == 1, _dot_nt(qm.astype(BF16), km.astype(BF16)), a)
            h, log_h = 2 * h, log_h + 1
        q_in = jnp.concatenate([q * jnp.exp2(cf), q * jnp.exp2(cb)], axis=1).astype(BF16)
        o = _dot(a.astype(BF16), v) + _dot_nt(q_in, st_s[c])
        y = o * lax.rsqrt(jnp.mean(o * o, axis=-1, keepdims=True) + EPS) * ng_ref[...]
        y_ref[0, pl.ds(r, ch), :] = (y * _silu(g_ref[0, 0, pl.ds(r, ch), :])).astype(y_ref.dtype)
        return carry

    lax.fori_loop(0, n, out_body, 0, unroll=HG_OUT_UNROLL)


def _hgrn_call(p_x, p_c, lb_logits, norm_g, heads, cast_src):
    b, _, t, _ = p_x.shape
    tc = p_c.shape[2]
    grid = (b, heads)
    n_cast, c_in, c_out, c_shape = _cast_specs(cast_src, grid)

    def col(off):
        return pl.BlockSpec((1, 1, t, LANES), lambda bi, h: (bi, off * heads + h, 0, 0))

    def ctx_col(off):
        return pl.BlockSpec((1, 1, tc, LANES), lambda bi, h: (bi, off * heads + h, 0, 0))

    return pl.pallas_call(
        functools.partial(_hgrn_kernel, n_cast),
        grid=grid,
        in_specs=[col(0), col(1), col(2), col(5), col(6), ctx_col(0), ctx_col(1), ctx_col(2),
                  pl.BlockSpec((2, 2, LANES), lambda bi, h: (0, 0, h)),
                  pl.BlockSpec((1, LANES), lambda bi, h: (0, 0)),
                  c_in],
        out_specs=[pl.BlockSpec((1, t, LANES), lambda bi, h: (bi, 0, h)), c_out],
        out_shape=[jax.ShapeDtypeStruct((b, t, heads * LANES), BF16), c_shape],
        scratch_shapes=[pltpu.VMEM((t, LANES), F32), pltpu.VMEM((t, LANES), F32),
                        pltpu.VMEM((t, LANES), F32), pltpu.VMEM((t, LANES), F32),
                        pltpu.VMEM((t // HG_CHUNK, LANES, 2 * LANES), BF16),
                        pltpu.VMEM((LANES, LANES), F32), pltpu.VMEM((LANES, LANES), F32)],
        compiler_params=_params("arbitrary", "arbitrary"),
        name="hgrn",
    )(p_x, p_x, p_x, p_x, p_x, p_c, p_c, p_c, lb_logits, norm_g.reshape(1, LANES), cast_src)


def _rope(a, cos, sin_signed, swap=None):
    if swap is None:
        lane = lax.broadcasted_iota(jnp.int32, a.shape, 1)
        partner = jnp.where((lane & 1) == 0, pltpu.roll(a, LANES - 1, axis=1), pltpu.roll(a, 1, axis=1))
    else:
        partner = _dot(a.astype(BF16), swap)
    return a * cos + partner * sin_signed


def _ret_state_update(s_ref, k, v, k_decay, s_decay):
    s_ref[...] = s_ref[...] * s_decay + _dot_tn(v.astype(BF16), (k * k_decay).astype(BF16))


def _ret_kernel(n_cast, q_ref, k_ref, v_ref, g_ref, ck_ref, cv_ref, dl_ref, cos_ref, sin_ref, sw_ref, c1_ref, c2_ref,
                y_ref, co1_ref, co2_ref, kr_s, st_s, sf_s, sb_s):
    _cast_block(c1_ref, co1_ref, _grid_step(2), n_cast)
    _cast_block(c2_ref, co2_ref, _grid_step(2), n_cast)
    t_len, tc_len = q_ref.shape[2], ck_ref.shape[2]
    ch = RET_CHUNK
    cc = min(ch, tc_len)
    n, nc = t_len // ch, tc_len // cc
    k_scale = LANES ** -0.5

    dl = dl_ref[0]
    gam = jnp.minimum(dl, 0.0) - jnp.log1p(jnp.exp(-jnp.abs(dl)))
    gam_f, gam_b = gam[0:1], gam[1:2]

    def decays(m):
        pos = lax.broadcasted_iota(jnp.int32, (m, LANES), 0).astype(F32)
        return dict(kf=jnp.exp(gam_f * (m - 1.0 - pos)), kb=jnp.exp(gam_b * pos),
                    qf=jnp.exp(gam_f * (pos + 1.0)), qb=jnp.exp(gam_b * (m - pos)),
                    sf=jnp.exp(gam_f * m), sb=jnp.exp(gam_b * m))

    sf_s[...] = jnp.zeros_like(sf_s)
    sb_s[...] = jnp.zeros_like(sb_s)

    dc = decays(cc)

    def ctx_body(c, carry):
        rf = pl.multiple_of(c * cc, cc)
        rb = pl.multiple_of((nc - 1 - c) * cc, cc)
        _ret_state_update(sf_s, ck_ref[0, 0, pl.ds(rf, cc), :] * k_scale, cv_ref[0, 0, pl.ds(rf, cc), :],
                          dc["kf"], dc["sf"])
        _ret_state_update(sb_s, ck_ref[0, 0, pl.ds(rb, cc), :] * k_scale, cv_ref[0, 0, pl.ds(rb, cc), :],
                          dc["kb"], dc["sb"])
        return carry

    lax.fori_loop(0, nc, ctx_body, 0)

    def rope_body(c, carry):
        r = pl.multiple_of(c * ch, ch)
        kr_s[pl.ds(r, ch), :] = _rope(k_ref[0, 0, pl.ds(r, ch), :] * k_scale,
                                      cos_ref[pl.ds(r, ch), :], sin_ref[pl.ds(r, ch), :], sw_ref[...])
        return carry

    lax.fori_loop(0, n, rope_body, 0, unroll=RET_UNROLL)

    dd = decays(ch)

    def state_body(c, carry):
        cb = n - 1 - c
        rf = pl.multiple_of(c * ch, ch)
        rb = pl.multiple_of(cb * ch, ch)
        st_s[c, :, 0:LANES] = sf_s[...].astype(BF16)
        _ret_state_update(sf_s, kr_s[pl.ds(rf, ch), :], v_ref[0, 0, pl.ds(rf, ch), :], dd["kf"], dd["sf"])
        st_s[cb, :, LANES:2 * LANES] = sb_s[...].astype(BF16)
        _ret_state_update(sb_s, kr_s[pl.ds(rb, ch), :], v_ref[0, 0, pl.ds(rb, ch), :], dd["kb"], dd["sb"])
        return carry

    lax.fori_loop(0, n, state_body, 0, unroll=RET_UNROLL)

    ti = lax.broadcasted_iota(jnp.int32, (ch, ch), 0)
    si = lax.broadcasted_iota(jnp.int32, (ch, ch), 1)
    lag = (ti - si).astype(F32)
    g_f = jnp.broadcast_to(gam_f[:, 0:1], (ch, ch))
    g_b = jnp.broadcast_to(gam_b[:, 0:1], (ch, ch))
    decay = jnp.where(ti > si, jnp.exp(g_f * jnp.maximum(lag, 0.0)),
                      jnp.where(ti < si, jnp.exp(g_b * jnp.maximum(-lag, 0.0)), 2.0))

    def out_body(c, carry):
        r = pl.multiple_of(c * ch, ch)
        q = _rope(q_ref[0, 0, pl.ds(r, ch), :], cos_ref[pl.ds(r, ch), :], sin_ref[pl.ds(r, ch), :])
        k = kr_s[pl.ds(r, ch), :].astype(BF16)
        v = v_ref[0, 0, pl.ds(r, ch), :].astype(BF16)
        a = _dot_nt(q.astype(BF16), k) * decay
        q_in = jnp.concatenate([q * dd["qf"], q * dd["qb"]], axis=1).astype(BF16)
        o = _dot(a.astype(BF16), v) + _dot_nt(q_in, st_s[c])
        oc = o - jnp.mean(o, axis=-1, keepdims=True)
        y = oc * lax.rsqrt(jnp.mean(oc * oc, axis=-1, keepdims=True) + EPS)
        y_ref[0, pl.ds(r, ch), :] = (y * _silu(g_ref[0, 0, pl.ds(r, ch), :])).astype(y_ref.dtype)
        return carry

    lax.fori_loop(0, n, out_body, 0, unroll=RET_UNROLL)


def _ret_call(p_x, p_c, decay_logit, cos, sin_signed, heads, cast_src1, cast_src2):
    b, _, t, _ = p_x.shape
    tc = p_c.shape[2]
    grid = (b, heads)
    assert cast_src1.shape == cast_src2.shape
    n_cast, c_in, c_out, c_shape = _cast_specs(cast_src1, grid)
    swap = jnp.asarray(np.eye(LANES, dtype=np.float32)[:, np.arange(LANES) ^ 1], BF16)

    def col(off):
        return pl.BlockSpec((1, 1, t, LANES), lambda bi, h: (bi, off * heads + h, 0, 0))

    def ctx_col(off):
        return pl.BlockSpec((1, 1, tc, LANES), lambda bi, h: (bi, off * heads + h, 0, 0))

    return pl.pallas_call(
        functools.partial(_ret_kernel, n_cast),
        grid=grid,
        in_specs=[col(7), col(3), col(4), col(8), ctx_col(3), ctx_col(4),
                  pl.BlockSpec((1, 2, LANES), lambda bi, h: (h, 0, 0)),
                  pl.BlockSpec((t, LANES), lambda bi, h: (0, 0)),
                  pl.BlockSpec((t, LANES), lambda bi, h: (0, 0)),
                  pl.BlockSpec((LANES, LANES), lambda bi, h: (0, 0)),
                  c_in, c_in],
        out_specs=[pl.BlockSpec((1, t, LANES), lambda bi, h: (bi, 0, h)), c_out, c_out],
        out_shape=[jax.ShapeDtypeStruct((b, t, heads * LANES), BF16), c_shape, c_shape],
        scratch_shapes=[pltpu.VMEM((t, LANES), F32),
                        pltpu.VMEM((t // RET_CHUNK, LANES, 2 * LANES), BF16),
                        pltpu.VMEM((LANES, LANES), F32), pltpu.VMEM((LANES, LANES), F32)],
        compiler_params=_params("arbitrary", "arbitrary"),
        name="ret",
    )(p_x, p_x, p_x, p_x, p_c, p_c, decay_logit, cos, sin_signed, swap, cast_src1, cast_src2)


def _merge_kernel(yh_ref, yr_ref, wh_ref, wr_ref, gh_ref, gr_ref, o_ref):
    acc_h = _dot(yh_ref[0], wh_ref[...])
    acc_r = _dot(yr_ref[0], wr_ref[...])
    for j in range(gh_ref.shape[1]):
        sl = slice(j * LANES, (j + 1) * LANES)
        o_ref[0, :, sl] = (jax.nn.sigmoid(gh_ref[0, j]) * acc_h[:, sl]
                           + jax.nn.sigmoid(gr_ref[0, j]) * acc_r[:, sl]).astype(o_ref.dtype)


def _merge_call(y_hg, y_ret, w_bh, w_br, p_x, gate_h_blk, gate_r_blk):
    b, t, kw = y_hg.shape
    d = w_bh.shape[1]
    tm = _tile(t, 1024, SUBLANES)
    tn = _tile(d, 512)
    nb = tn // LANES
    return pl.pallas_call(
        _merge_kernel,
        grid=(b, t // tm, d // tn),
        in_specs=[pl.BlockSpec((1, tm, kw), lambda bi, i, j: (bi, i, 0)),
                  pl.BlockSpec((1, tm, kw), lambda bi, i, j: (bi, i, 0)),
                  pl.BlockSpec((kw, tn), lambda bi, i, j: (0, j)),
                  pl.BlockSpec((kw, tn), lambda bi, i, j: (0, j)),
                  pl.BlockSpec((1, nb, tm, LANES), lambda bi, i, j: (bi, gate_h_blk // nb + j, i, 0)),
                  pl.BlockSpec((1, nb, tm, LANES), lambda bi, i, j: (bi, gate_r_blk // nb + j, i, 0))],
        out_specs=pl.BlockSpec((1, tm, tn), lambda bi, i, j: (bi, i, j)),
        out_shape=jax.ShapeDtypeStruct((b, t, d), BF16),
        compiler_params=_params("parallel", "parallel", "arbitrary"),
        name="merge",
    )(y_hg, y_ret, w_bh, w_br, p_x, p_x)


def _gated_residual_kernel(a_ref, w_ref, x_ref, gate_ref, o_ref):
    o_ref[0] = x_ref[0] + gate_ref[0] * _dot(a_ref[0], w_ref[...])


def _out_proj_call(a, w, x, mod3, gate_blk_of):
    b, t, kw = a.shape
    d = w.shape[1]
    tm = _tile(t, 1024, SUBLANES)
    tn = _tile(d, 1024)
    return pl.pallas_call(
        _gated_residual_kernel,
        grid=(b, t // tm, d // tn),
        in_specs=[pl.BlockSpec((1, tm, kw), lambda bi, i, j: (bi, i, 0)),
                  pl.BlockSpec((kw, tn), lambda bi, i, j: (0, j)),
                  pl.BlockSpec((1, tm, tn), lambda bi, i, j: (bi, i, j)),
                  pl.BlockSpec((1, 1, tn), lambda bi, i, j: (bi, 0, gate_blk_of(tn) + j))],
        out_specs=pl.BlockSpec((1, tm, tn), lambda bi, i, j: (bi, i, j)),
        out_shape=jax.ShapeDtypeStruct((b, t, d), F32),
        compiler_params=_params("parallel", "parallel", "arbitrary"),
        name="out_proj",
    )(a, w, x, mod3)


def _ff1_kernel(n_cast, h_ref, w_ref, c_ref, o_ref, co_ref):
    z = jnp.maximum(_dot(h_ref[0], w_ref[...]), 0.0)
    o_ref[0] = (z * z).astype(o_ref.dtype)
    _cast_block(c_ref, co_ref, _grid_step(3), n_cast)


def _ff1_call(h, w, cast_src):
    b, t, d = h.shape
    n = w.shape[1]
    tm = _tile(t, 1024, SUBLANES)
    tn = _tile(n, 1024)
    grid = (b, t // tm, n // tn)
    n_cast, c_in, c_out, c_shape = _cast_specs(cast_src, grid)
    return pl.pallas_call(
        functools.partial(_ff1_kernel, n_cast),
        grid=grid,
        in_specs=[pl.BlockSpec((1, tm, d), lambda bi, i, j: (bi, i, 0)),
                  pl.BlockSpec((d, tn), lambda bi, i, j: (0, j)),
                  c_in],
        out_specs=[pl.BlockSpec((1, tm, tn), lambda bi, i, j: (bi, i, j)), c_out],
        out_shape=[jax.ShapeDtypeStruct((b, t, n), BF16), c_shape],
        compiler_params=_params("arbitrary", "arbitrary", "arbitrary"),
        name="ff1",
    )(h, w, cast_src)


def _ff2_kernel(a_ref, w_ref, x_ref, gate_ref, o_ref, acc_ref):
    k = pl.program_id(3)

    @pl.when(k == 0)
    def _():
        acc_ref[...] = jnp.zeros_like(acc_ref)

    acc_ref[...] += _dot(a_ref[0], w_ref[...])

    @pl.when(k == pl.num_programs(3) - 1)
    def _():
        o_ref[0] = x_ref[0] + gate_ref[0] * acc_ref[...]


def _ff2_call(a, w, x, mod3, gate_blk_of):
    b, t, kw = a.shape
    d = w.shape[1]
    tm = _tile(t, 1024, SUBLANES)
    tn = _tile(d, 1024)
    tk = _tile(kw, 4096)
    return pl.pallas_call(
        _ff2_kernel,
        grid=(b, t // tm, d // tn, kw // tk),
        in_specs=[pl.BlockSpec((1, tm, tk), lambda bi, i, j, k: (bi, i, k)),
                  pl.BlockSpec((tk, tn), lambda bi, i, j, k: (k, j)),
                  pl.BlockSpec((1, tm, tn), lambda bi, i, j, k: (bi, i, j)),
                  pl.BlockSpec((1, 1, tn), lambda bi, i, j, k: (bi, 0, gate_blk_of(tn) + j))],
        out_specs=pl.BlockSpec((1, tm, tn), lambda bi, i, j, k: (bi, i, j)),
        out_shape=jax.ShapeDtypeStruct((b, t, d), F32),
        scratch_shapes=[pltpu.VMEM((tm, tn), F32)],
        compiler_params=_params("parallel", "parallel", "parallel", "arbitrary"),
        name="ff2",
    )(a, w, x, mod3)


def _rope_tables(t_len):
    pos = jnp.arange(t_len)
    row = (pos // GRID_W).astype(F32)
    col = (pos % GRID_W).astype(F32)
    n_freq = LANES // 4
    inv_freq = ROPE_BASE ** (-jnp.arange(n_freq, dtype=F32) / n_freq)
    ang = jnp.concatenate([row[:, None] * inv_freq, col[:, None] * inv_freq], axis=-1)
    cos, sin = jnp.cos(ang), jnp.sin(ang)
    cos_rep = jnp.repeat(cos, 2, axis=-1)
    sin_signed = jnp.stack([-sin, sin], axis=-1).reshape(t_len, LANES)
    return cos_rep, sin_signed


def kernel(x, c, ctx, c_ctx, w_mod, b_mod, norm1_g, norm2_g, w_in, hg_lb_logits, hg_norm_g, ret_decay_logit,
           w_branch_hgrn, w_branch_ret, w_out, w_ff1, w_ff2, final_norm_g):
    b, t, d = x.shape
    assert w_mod.shape[0] == 1 and hg_lb_logits.shape[0] == 2, "one layer"
    assert b + 1 <= SUBLANES
    hw = w_branch_hgrn.shape[1]
    heads = hw // LANES
    assert w_branch_ret.shape[1] == hw and w_in.shape[2] == 9 * hw + 2 * d
    n_state = 5 * hw

    c_rows = jnp.zeros((SUBLANES, d), F32).at[:b].set(c).at[b].set(c_ctx)
    mod3 = _mod_call(c_rows, w_mod[0], b_mod[0]).reshape(SUBLANES, 1, 6 * d)

    h_x = _norm_mod_call(x, norm1_g[0], mod3, lambda bi: bi, 0, 1)
    h_c = _norm_mod_call(ctx, norm1_g[0], mod3, lambda bi: b, 0, 1)
    p_x, w_ff1_b = _in_proj_call(h_x, w_in[0], w_in.shape[2], 512, cast_src=w_ff1[0])
    p_c = _in_proj_call(h_c, w_in[0], n_state, 256)

    y_hg, w_out_b = _hgrn_call(p_x, p_c, hg_lb_logits, hg_norm_g[0], heads, w_out[0])
    cos_rep, sin_signed = _rope_tables(t)
    decay_logit = jnp.broadcast_to(ret_decay_logit[0].T[:, :, None], (heads, 2, LANES))
    y_ret, w_bh_b, w_br_b = _ret_call(p_x, p_c, decay_logit, cos_rep, sin_signed, heads,
                                      w_branch_hgrn[0], w_branch_ret[0])

    merged = _merge_call(y_hg, y_ret, w_bh_b, w_br_b, p_x, 9 * heads, 9 * heads + d // LANES)
    x1 = _out_proj_call(merged, w_out_b, x, mod3, lambda tn: 2 * d // tn)
    h2 = _norm_mod_call(x1, norm2_g[0], mod3, lambda bi: bi, 3, 4)
    act, w_ff2_b = _ff1_call(h2, w_ff1_b, w_ff2[0])
    x2 = _ff2_call(act, w_ff2_b, x1, mod3, lambda tn: 5 * d // tn)
    return _final_norm_call(x2, final_norm_g)
```

```python
import functools

import jax
import jax.numpy as jnp
import numpy as np
from jax import lax
from jax.experimental import pallas as pl
from jax.experimental.pallas import tpu as pltpu

F32 = jnp.float32
BF16 = jnp.bfloat16

LANES = 128
SUBLANES = 8
BF16_ROWS = 16
VMEM_LIMIT = 60 * 2 ** 20
EPS = 1e-6
ROPE_BASE = 10000.0
GRID_W = 64
HG_CHUNK = 64
RET_CHUNK = 256
HG_STATE_UNROLL = 4
HG_OUT_UNROLL = 8
RET_UNROLL = 4
W_IN_SPLIT = 4


def _params(*sem):
    return pltpu.CompilerParams(dimension_semantics=sem, vmem_limit_bytes=VMEM_LIMIT)


def _tile(n, target, unit=LANES):
    t = min(n, target) // unit * unit
    while n % t:
        t -= unit
    return t


def _silu(z):
    return z * jax.nn.sigmoid(z)


def _dot(a, b):
    return jnp.dot(a, b, preferred_element_type=F32)


def _dot_nt(a, b):
    return lax.dot_general(a, b, (((1,), (1,)), ((), ())), preferred_element_type=F32)


def _dot_tn(a, b):
    return lax.dot_general(a, b, (((0,), (0,)), ((), ())), preferred_element_type=F32)


def _grid_step(rank):
    step = pl.program_id(0)
    for axis in range(1, rank):
        step = step * pl.num_programs(axis) + pl.program_id(axis)
    return step


def _linear_step(ids, sizes):
    step = ids[0]
    for i, n in zip(ids[1:], sizes[1:]):
        step = step * n + i
    return step


def _cast_specs(src, grid):
    rows, cols = src.shape
    steps = int(np.prod(grid))
    n_blocks = 1
    while n_blocks * 2 <= steps and rows % (n_blocks * 2) == 0 and rows // (n_blocks * 2) >= BF16_ROWS:
        n_blocks *= 2

    def idx(*ids):
        return (jnp.minimum(_linear_step(ids, grid), n_blocks - 1), 0)

    spec = pl.BlockSpec((rows // n_blocks, cols), idx)
    return n_blocks, spec, spec, jax.ShapeDtypeStruct(src.shape, BF16)


def _cast_block(src_ref, dst_ref, step, n_blocks):
    @pl.when(step < n_blocks)
    def _():
        dst_ref[...] = src_ref[...].astype(dst_ref.dtype)


def _mod_kernel(c_ref, w_ref, b_ref, o_ref):
    s = _silu(c_ref[...]).astype(BF16)
    o_ref[...] = _dot(s, w_ref[...].astype(BF16)) + b_ref[...]


def _mod_call(c_rows, w_mod, b_mod, n):
    r, d = c_rows.shape
    tn = _tile(n, 512)
    return pl.pallas_call(
        _mod_kernel,
        grid=(n // tn,),
        in_specs=[pl.BlockSpec((r, d), lambda j: (0, 0)),
                  pl.BlockSpec((d, tn), lambda j: (0, j)),
                  pl.BlockSpec((1, tn), lambda j: (0, j))],
        out_specs=pl.BlockSpec((r, tn), lambda j: (0, j)),
        out_shape=jax.ShapeDtypeStruct((r, n), F32),
        compiler_params=_params("parallel"),
        name="mod",
    )(c_rows, w_mod, b_mod.reshape(1, -1))


def _norm_mod_kernel(x_ref, g_ref, scale_ref, shift_ref, o_ref):
    x = x_ref[0]
    y = x * lax.rsqrt(jnp.mean(x * x, axis=-1, keepdims=True) + EPS) * g_ref[...]
    o_ref[0] = (y * (1.0 + scale_ref[0]) + shift_ref[0]).astype(o_ref.dtype)


def _norm_mod_call(x, gain, mod3, row_of_batch, shift_blk, scale_blk):
    b, t, d = x.shape
    tm = _tile(t, 512, SUBLANES)
    return pl.pallas_call(
        _norm_mod_kernel,
        grid=(b, t // tm),
        in_specs=[pl.BlockSpec((1, tm, d), lambda bi, i: (bi, i, 0)),
                  pl.BlockSpec((1, d), lambda bi, i: (0, 0)),
                  pl.BlockSpec((1, 1, d), lambda bi, i: (row_of_batch(bi), 0, scale_blk)),
                  pl.BlockSpec((1, 1, d), lambda bi, i: (row_of_batch(bi), 0, shift_blk))],
        out_specs=pl.BlockSpec((1, tm, d), lambda bi, i: (bi, i, 0)),
        out_shape=jax.ShapeDtypeStruct((b, t, d), BF16),
        compiler_params=_params("parallel", "parallel"),
        name="norm_mod",
    )(x, gain.reshape(1, d), mod3, mod3)


def _final_norm_kernel(x_ref, g_ref, o_ref):
    x = x_ref[0]
    o_ref[0] = x * lax.rsqrt(jnp.mean(x * x, axis=-1, keepdims=True) + EPS) * g_ref[...]


def _final_norm_call(x, gain):
    b, t, d = x.shape
    tm = _tile(t, 512, SUBLANES)
    return pl.pallas_call(
        _final_norm_kernel,
        grid=(b, t // tm),
        in_specs=[pl.BlockSpec((1, tm, d), lambda bi, i: (bi, i, 0)),
                  pl.BlockSpec((1, d), lambda bi, i: (0, 0))],
        out_specs=pl.BlockSpec((1, tm, d), lambda bi, i: (bi, i, 0)),
        out_shape=jax.ShapeDtypeStruct((b, t, d), F32),
        compiler_params=_params("parallel", "parallel"),
        name="final_norm",
    )(x, gain.reshape(1, d))


def _in_proj_tile(h_ref, w_refs, o_ref, wb_s):
    @pl.when((pl.program_id(1) == 0) & (pl.program_id(2) == 0))
    def _():
        rows = w_refs[0].shape[0]
        for q, w_ref in enumerate(w_refs):
            wb_s[q * rows:(q + 1) * rows, :] = w_ref[...].astype(BF16)

    acc = _dot(h_ref[0], wb_s[...])
    for j in range(o_ref.shape[1]):
        o_ref[0, j] = acc[:, j * LANES:(j + 1) * LANES]


def _in_proj_kernel(h_ref, *refs):
    w_refs, (o_ref, wb_s) = refs[:W_IN_SPLIT], refs[W_IN_SPLIT:]
    _in_proj_tile(h_ref, w_refs, o_ref, wb_s)


def _in_proj_cast_kernel(n_cast, h_ref, *refs):
    w_refs, (c_ref, o_ref, co_ref, wb_s) = refs[:W_IN_SPLIT], refs[W_IN_SPLIT:]
    _in_proj_tile(h_ref, w_refs, o_ref, wb_s)
    _cast_block(c_ref, co_ref, _grid_step(3), n_cast)


def _in_proj_call(h, w, n, tm_target, cast_src=None):
    b, t, d = h.shape
    tm = _tile(t, tm_target, SUBLANES)
    tn = _tile(n, 1024)
    nj, ni = n // tn, t // tm
    grid = (nj, b, ni)
    dk = d // W_IN_SPLIT
    period = b * ni

    def w_spec(q):
        switch = 1 + q * max(1, (period - 2) // W_IN_SPLIT)
        return pl.BlockSpec((dk, tn), lambda j, bi, i: (q, jnp.minimum(j + (bi * ni + i >= switch), nj - 1)))

    in_specs = [pl.BlockSpec((1, tm, d), lambda j, bi, i: (bi, i, 0))] + [w_spec(q) for q in range(W_IN_SPLIT)]
    out_specs = pl.BlockSpec((1, tn // LANES, tm, LANES), lambda j, bi, i: (bi, j, i, 0))
    out_shape = jax.ShapeDtypeStruct((b, n // LANES, t, LANES), F32)
    common = dict(grid=grid, scratch_shapes=[pltpu.VMEM((d, tn), BF16)],
                  compiler_params=_params("arbitrary", "arbitrary", "arbitrary"),
                  name="in_proj")
    ws = (w,) * W_IN_SPLIT
    if cast_src is None:
        return pl.pallas_call(_in_proj_kernel, in_specs=in_specs, out_specs=out_specs, out_shape=out_shape,
                              **common)(h, *ws)
    n_cast, c_in, c_out, c_shape = _cast_specs(cast_src, grid)
    return pl.pallas_call(functools.partial(_in_proj_cast_kernel, n_cast), in_specs=in_specs + [c_in],
                          out_specs=[out_specs, c_out], out_shape=[out_shape, c_shape],
                          **common)(h, *ws, cast_src)


def _cumsum_rows(x, reverse):
    n = x.shape[0]
    row = lax.broadcasted_iota(jnp.int32, x.shape, 0)
    d = 1
    while d < n:
        if reverse:
            x = x + jnp.where(row < n - d, pltpu.roll(x, n - d, axis=0), 0.0)
        else:
            x = x + jnp.where(row >= d, pltpu.roll(x, d, axis=0), 0.0)
        d *= 2
    return x


def _block_row_bcast(x, block, offset):
    n, w = x.shape
    if block >= SUBLANES:
        xb = x.reshape(n // block, block, w)
        return jnp.broadcast_to(xb[:, offset:offset + 1, :], xb.shape).reshape(n, w)
    m = lax.broadcasted_iota(jnp.int32, x.shape, 0) & (block - 1)
    out = x
    for mm in range(block):
        if mm != offset:
            out = jnp.where(m == mm, pltpu.roll(x, (mm - offset) % n, axis=0), out)
    return out


def _hgrn_gates(z, lb):
    f = lb + (1.0 - lb) * jax.nn.sigmoid(z)
    return 1.0 - f, jnp.log2(f)


def _state_update(s_ref, k, cum, v, reverse):
    n = cum.shape[0]
    edge = cum[0:1, :] if reverse else cum[n - 1:n, :]
    k_dec = (k * jnp.exp2(edge - cum)).astype(BF16)
    s_ref[...] = s_ref[...] * jnp.exp2(edge) + _dot_tn(v.astype(BF16), k_dec)


def _hgrn_kernel(n_cast, i_ref, ff_ref, fb_ref, q_ref, g_ref, ci_ref, cff_ref, cfb_ref, lbl_ref, ng_ref, c_ref,
                 y_ref, co_ref, kf_s, kb_s, cf_s, cb_s, st_s, sf_s, sb_s):
    _cast_block(c_ref, co_ref, _grid_step(2), n_cast)
    t_len, tc_len = i_ref.shape[2], ci_ref.shape[2]
    ch = HG_CHUNK
    cc = min(ch, tc_len)
    n, nc = t_len // ch, tc_len // cc

    l0, l1 = lbl_ref[0], lbl_ref[1]
    mx = jnp.maximum(l0, l1)
    e0, e1 = jnp.exp(l0 - mx), jnp.exp(l1 - mx)
    lb = e0 / (e0 + e1)
    lb_f, lb_b = lb[0:1], lb[1:2]

    sf_s[...] = jnp.zeros_like(sf_s)
    sb_s[...] = jnp.zeros_like(sb_s)

    def ctx_body(c, carry):
        rf = pl.multiple_of(c * cc, cc)
        rb = pl.multiple_of((nc - 1 - c) * cc, cc)
        k, lf = _hgrn_gates(cff_ref[0, 0, pl.ds(rf, cc), :], lb_f)
        _state_update(sf_s, k, _cumsum_rows(lf, False), ci_ref[0, 0, pl.ds(rf, cc), :], False)
        k, lf = _hgrn_gates(cfb_ref[0, 0, pl.ds(rb, cc), :], lb_b)
        _state_update(sb_s, k, _cumsum_rows(lf, True), ci_ref[0, 0, pl.ds(rb, cc), :], True)
        return carry

    lax.fori_loop(0, nc, ctx_body, 0)

    def state_body(c, carry):
        cb = n - 1 - c
        rf = pl.multiple_of(c * ch, ch)
        rb = pl.multiple_of(cb * ch, ch)
        k, lf = _hgrn_gates(ff_ref[0, 0, pl.ds(rf, ch), :], lb_f)
        cum = _cumsum_rows(lf, False)
        kf_s[pl.ds(rf, ch), :] = k
        cf_s[pl.ds(rf, ch), :] = cum
        st_s[c, :, 0:LANES] = sf_s[...].astype(BF16)
        _state_update(sf_s, k, cum, i_ref[0, 0, pl.ds(rf, ch), :], False)
        k, lf = _hgrn_gates(fb_ref[0, 0, pl.ds(rb, ch), :], lb_b)
        cum = _cumsum_rows(lf, True)
        kb_s[pl.ds(rb, ch), :] = k
        cb_s[pl.ds(rb, ch), :] = cum
        st_s[cb, :, LANES:2 * LANES] = sb_s[...].astype(BF16)
        _state_update(sb_s, k, cum, i_ref[0, 0, pl.ds(rb, ch), :], True)
        return carry

    lax.fori_loop(0, n, state_body, 0, unroll=HG_STATE_UNROLL)

    pair_xor = (lax.broadcasted_iota(jnp.int32, (ch, ch), 0) ^ lax.broadcasted_iota(jnp.int32, (ch, ch), 1))
    row = lax.broadcasted_iota(jnp.int32, (ch, LANES), 0)
    q_scale = LANES ** -0.5

    def out_body(c, carry):
        r = pl.multiple_of(c * ch, ch)
        q = _silu(q_ref[0, 0, pl.ds(r, ch), :]) * q_scale
        kf, kb = kf_s[pl.ds(r, ch), :], kb_s[pl.ds(r, ch), :]
        cf, cb = cf_s[pl.ds(r, ch), :], cb_s[pl.ds(r, ch), :]
        v = i_ref[0, 0, pl.ds(r, ch), :].astype(BF16)
        a = jnp.where(pair_xor == 0, _dot_nt(q.astype(BF16), (kf + kb).astype(BF16)), 0.0)
        odd = (row & 1) != 0
        qm = q * jnp.where(odd, 1.0 - kf, 1.0 - kb)
        a = jnp.where(pair_xor == 1, _dot_nt(qm.astype(BF16), jnp.where(odd, kb, kf).astype(BF16)), a)
        h, log_h = 2, 1
        while h < ch:
            second = (row & h) != 0
            df = cf - _block_row_bcast(cf, 2 * h, h - 1)
            db = cb - _block_row_bcast(cb, 2 * h, h)
            qm = q * jnp.exp2(jnp.where(second, df, db))
            km = jnp.where(second, kb, kf) * jnp.exp2(-jnp.where(second, db, df))
            a = jnp.where((pair_xor >> log_h) == 1, _dot_nt(qm.astype(BF16), km.astype(BF16)), a)
            h, log_h = 2 * h, log_h + 1
        q_in = jnp.concatenate([q * jnp.exp2(cf), q * jnp.exp2(cb)], axis=1).astype(BF16)
        o = _dot(a.astype(BF16), v) + _dot_nt(q_in, st_s[c])
        y = o * lax.rsqrt(jnp.mean(o * o, axis=-1, keepdims=True) + EPS) * ng_ref[...]
        y_ref[0, pl.ds(r, ch), :] = (y * _silu(g_ref[0, 0, pl.ds(r, ch), :])).astype(y_ref.dtype)
        return carry

    lax.fori_loop(0, n, out_body, 0, unroll=HG_OUT_UNROLL)


def _hgrn_call(p_x, p_c, lb_logits, norm_g, heads, cast_src):
    b, _, t, _ = p_x.shape
    tc = p_c.shape[2]
    grid = (b, heads)
    n_cast, c_in, c_out, c_shape = _cast_specs(cast_src, grid)

    def col(off):
        return pl.BlockSpec((1, 1, t, LANES), lambda bi, h: (bi, off * heads + h, 0, 0))

    def ctx_col(off):
        return pl.BlockSpec((1, 1, tc, LANES), lambda bi, h: (bi, off * heads + h, 0, 0))

    return pl.pallas_call(
        functools.partial(_hgrn_kernel, n_cast),
        grid=grid,
        in_specs=[col(0), col(1), col(2), col(5), col(6), ctx_col(0), ctx_col(1), ctx_col(2),
                  pl.BlockSpec((2, 2, LANES), lambda bi, h: (0, 0, h)),
                  pl.BlockSpec((1, LANES), lambda bi, h: (0, 0)),
                  c_in],
        out_specs=[pl.BlockSpec((1, t, LANES), lambda bi, h: (bi, 0, h)), c_out],
        out_shape=[jax.ShapeDtypeStruct((b, t, heads * LANES), BF16), c_shape],
        scratch_shapes=[pltpu.VMEM((t, LANES), F32), pltpu.VMEM((t, LANES), F32),
                        pltpu.VMEM((t, LANES), F32), pltpu.VMEM((t, LANES), F32),
                        pltpu.VMEM((t // HG_CHUNK, LANES, 2 * LANES), BF16),
                        pltpu.VMEM((LANES, LANES), F32), pltpu.VMEM((LANES, LANES), F32)],
        compiler_params=_params("arbitrary", "arbitrary"),
        name="hgrn",
    )(p_x, p_x, p_x, p_x, p_x, p_c, p_c, p_c, lb_logits, norm_g.reshape(1, LANES), cast_src)


def _rope(a, cos, sin_signed, swap=None):
    if swap is None:
        lane = lax.broadcasted_iota(jnp.int32, a.shape, 1)
        partner = jnp.where((lane & 1) == 0, pltpu.roll(a, LANES - 1, axis=1), pltpu.roll(a, 1, axis=1))
    else:
        partner = _dot(a.astype(BF16), swap)
    return a * cos + partner * sin_signed


def _ret_state_update(s_ref, k, v, k_decay, s_decay):
    s_ref[...] = s_ref[...] * s_decay + _dot_tn(v.astype(BF16), (k * k_decay).astype(BF16))


def _ret_kernel(n_cast, q_ref, k_ref, v_ref, g_ref, ck_ref, cv_ref, dl_ref, cos_ref, sin_ref, sw_ref, c1_ref, c2_ref,
                mc_ref, mw_ref, mb_ref, y_ref, co1_ref, co2_ref, mo_ref, kr_s, st_s, sf_s, sb_s):
    _cast_block(c1_ref, co1_ref, _grid_step(2), n_cast)
    _cast_block(c2_ref, co2_ref, _grid_step(2), n_cast)
    _mod_kernel(mc_ref, mw_ref, mb_ref, mo_ref)
    t_len, tc_len = q_ref.shape[2], ck_ref.shape[2]
    ch = RET_CHUNK
    cc = min(ch, tc_len)
    n, nc = t_len // ch, tc_len // cc
    k_scale = LANES ** -0.5

    dl = dl_ref[0]
    gam = jnp.minimum(dl, 0.0) - jnp.log1p(jnp.exp(-jnp.abs(dl)))
    gam_f, gam_b = gam[0:1], gam[1:2]

    def decays(m):
        pos = lax.broadcasted_iota(jnp.int32, (m, LANES), 0).astype(F32)
        return dict(kf=jnp.exp(gam_f * (m - 1.0 - pos)), kb=jnp.exp(gam_b * pos),
                    qf=jnp.exp(gam_f * (pos + 1.0)), qb=jnp.exp(gam_b * (m - pos)),
                    sf=jnp.exp(gam_f * m), sb=jnp.exp(gam_b * m))

    sf_s[...] = jnp.zeros_like(sf_s)
    sb_s[...] = jnp.zeros_like(sb_s)

    dc = decays(cc)

    def ctx_body(c, carry):
        rf = pl.multiple_of(c * cc, cc)
        rb = pl.multiple_of((nc - 1 - c) * cc, cc)
        _ret_state_update(sf_s, ck_ref[0, 0, pl.ds(rf, cc), :] * k_scale, cv_ref[0, 0, pl.ds(rf, cc), :],
                          dc["kf"], dc["sf"])
        _ret_state_update(sb_s, ck_ref[0, 0, pl.ds(rb, cc), :] * k_scale, cv_ref[0, 0, pl.ds(rb, cc), :],
                          dc["kb"], dc["sb"])
        return carry

    lax.fori_loop(0, nc, ctx_body, 0)

    def rope_body(c, carry):
        r = pl.multiple_of(c * ch, ch)
        kr_s[pl.ds(r, ch), :] = _rope(k_ref[0, 0, pl.ds(r, ch), :] * k_scale,
                                      cos_ref[pl.ds(r, ch), :], sin_ref[pl.ds(r, ch), :], sw_ref[...])
        return carry

    lax.fori_loop(0, n, rope_body, 0, unroll=RET_UNROLL)

    dd = decays(ch)

    def state_body(c, carry):
        cb = n - 1 - c
        rf = pl.multiple_of(c * ch, ch)
        rb = pl.multiple_of(cb * ch, ch)
        st_s[c, :, 0:LANES] = sf_s[...].astype(BF16)
        _ret_state_update(sf_s, kr_s[pl.ds(rf, ch), :], v_ref[0, 0, pl.ds(rf, ch), :], dd["kf"], dd["sf"])
        st_s[cb, :, LANES:2 * LANES] = sb_s[...].astype(BF16)
        _ret_state_update(sb_s, kr_s[pl.ds(rb, ch), :], v_ref[0, 0, pl.ds(rb, ch), :], dd["kb"], dd["sb"])
        return carry

    lax.fori_loop(0, n, state_body, 0, unroll=RET_UNROLL)

    ti = lax.broadcasted_iota(jnp.int32, (ch, ch), 0)
    si = lax.broadcasted_iota(jnp.int32, (ch, ch), 1)
    lag = (ti - si).astype(F32)
    g_f = jnp.broadcast_to(gam_f[:, 0:1], (ch, ch))
    g_b = jnp.broadcast_to(gam_b[:, 0:1], (ch, ch))
    decay = jnp.where(ti > si, jnp.exp(g_f * jnp.maximum(lag, 0.0)),
                      jnp.where(ti < si, jnp.exp(g_b * jnp.maximum(-lag, 0.0)), 2.0))

    def out_body(c, carry):
        r = pl.multiple_of(c * ch, ch)
        q = _rope(q_ref[0, 0, pl.ds(r, ch), :], cos_ref[pl.ds(r, ch), :], sin_ref[pl.ds(r, ch), :])
        k = kr_s[pl.ds(r, ch), :].astype(BF16)
        v = v_ref[0, 0, pl.ds(r, ch), :].astype(BF16)
        a = _dot_nt(q.astype(BF16), k) * decay
        q_in = jnp.concatenate([q * dd["qf"], q * dd["qb"]], axis=1).astype(BF16)
        o = _dot(a.astype(BF16), v) + _dot_nt(q_in, st_s[c])
        oc = o - jnp.mean(o, axis=-1, keepdims=True)
        y = oc * lax.rsqrt(jnp.mean(oc * oc, axis=-1, keepdims=True) + EPS)
        y_ref[0, pl.ds(r, ch), :] = (y * _silu(g_ref[0, 0, pl.ds(r, ch), :])).astype(y_ref.dtype)
        return carry

    lax.fori_loop(0, n, out_body, 0, unroll=RET_UNROLL)


def _ret_call(p_x, p_c, decay_logit, cos, sin_signed, heads, cast_src1, cast_src2, c_rows, w_mod, b_mod, mod_col0):
    b, _, t, _ = p_x.shape
    tc = p_c.shape[2]
    grid = (b, heads)
    r, d = c_rows.shape
    mod_cols = w_mod.shape[1] - mod_col0
    mod_tn = mod_cols // (b * heads)
    assert mod_tn % LANES == 0 and mod_col0 % mod_tn == 0
    mod_blk0 = mod_col0 // mod_tn
    assert cast_src1.shape == cast_src2.shape
    n_cast, c_in, c_out, c_shape = _cast_specs(cast_src1, grid)
    swap = jnp.asarray(np.eye(LANES, dtype=np.float32)[:, np.arange(LANES) ^ 1], BF16)

    def col(off):
        return pl.BlockSpec((1, 1, t, LANES), lambda bi, h: (bi, off * heads + h, 0, 0))

    def ctx_col(off):
        return pl.BlockSpec((1, 1, tc, LANES), lambda bi, h: (bi, off * heads + h, 0, 0))

    return pl.pallas_call(
        functools.partial(_ret_kernel, n_cast),
        grid=grid,
        in_specs=[col(7), col(3), col(4), col(8), ctx_col(3), ctx_col(4),
                  pl.BlockSpec((1, 2, LANES), lambda bi, h: (h, 0, 0)),
                  pl.BlockSpec((t, LANES), lambda bi, h: (0, 0)),
                  pl.BlockSpec((t, LANES), lambda bi, h: (0, 0)),
                  pl.BlockSpec((LANES, LANES), lambda bi, h: (0, 0)),
                  c_in, c_in,
                  pl.BlockSpec((r, d), lambda bi, h: (0, 0)),
                  pl.BlockSpec((d, mod_tn), lambda bi, h: (0, mod_blk0 + bi * heads + h)),
                  pl.BlockSpec((1, mod_tn), lambda bi, h: (0, mod_blk0 + bi * heads + h))],
        out_specs=[pl.BlockSpec((1, t, LANES), lambda bi, h: (bi, 0, h)), c_out, c_out,
                   pl.BlockSpec((r, mod_tn), lambda bi, h: (0, bi * heads + h))],
        out_shape=[jax.ShapeDtypeStruct((b, t, heads * LANES), BF16), c_shape, c_shape,
                   jax.ShapeDtypeStruct((r, mod_cols), F32)],
        scratch_shapes=[pltpu.VMEM((t, LANES), F32),
                        pltpu.VMEM((t // RET_CHUNK, LANES, 2 * LANES), BF16),
                        pltpu.VMEM((LANES, LANES), F32), pltpu.VMEM((LANES, LANES), F32)],
        compiler_params=_params("arbitrary", "arbitrary"),
        name="ret",
    )(p_x, p_x, p_x, p_x, p_c, p_c, decay_logit, cos, sin_signed, swap, cast_src1, cast_src2,
      c_rows, w_mod, b_mod.reshape(1, -1))


def _merge_kernel(yh_ref, yr_ref, wh_ref, wr_ref, gh_ref, gr_ref, o_ref):
    acc_h = _dot(yh_ref[0], wh_ref[...])
    acc_r = _dot(yr_ref[0], wr_ref[...])
    for j in range(gh_ref.shape[1]):
        sl = slice(j * LANES, (j + 1) * LANES)
        o_ref[0, :, sl] = (jax.nn.sigmoid(gh_ref[0, j]) * acc_h[:, sl]
                           + jax.nn.sigmoid(gr_ref[0, j]) * acc_r[:, sl]).astype(o_ref.dtype)


def _merge_call(y_hg, y_ret, w_bh, w_br, p_x, gate_h_blk, gate_r_blk):
    b, t, kw = y_hg.shape
    d = w_bh.shape[1]
    tm = _tile(t, 1024, SUBLANES)
    tn = _tile(d, 512)
    nb = tn // LANES
    return pl.pallas_call(
        _merge_kernel,
        grid=(b, t // tm, d // tn),
        in_specs=[pl.BlockSpec((1, tm, kw), lambda bi, i, j: (bi, i, 0)),
                  pl.BlockSpec((1, tm, kw), lambda bi, i, j: (bi, i, 0)),
                  pl.BlockSpec((kw, tn), lambda bi, i, j: (0, j)),
                  pl.BlockSpec((kw, tn), lambda bi, i, j: (0, j)),
                  pl.BlockSpec((1, nb, tm, LANES), lambda bi, i, j: (bi, gate_h_blk // nb + j, i, 0)),
                  pl.BlockSpec((1, nb, tm, LANES), lambda bi, i, j: (bi, gate_r_blk // nb + j, i, 0))],
        out_specs=pl.BlockSpec((1, tm, tn), lambda bi, i, j: (bi, i, j)),
        out_shape=jax.ShapeDtypeStruct((b, t, d), BF16),
        compiler_params=_params("parallel", "parallel", "arbitrary"),
        name="merge",
    )(y_hg, y_ret, w_bh, w_br, p_x, p_x)


def _gated_residual_kernel(a_ref, w_ref, x_ref, gate_ref, o_ref):
    o_ref[0] = x_ref[0] + gate_ref[0] * _dot(a_ref[0], w_ref[...])


def _out_proj_call(a, w, x, mod3, gate_blk_of):
    b, t, kw = a.shape
    d = w.shape[1]
    tm = _tile(t, 1024, SUBLANES)
    tn = _tile(d, 1024)
    return pl.pallas_call(
        _gated_residual_kernel,
        grid=(b, t // tm, d // tn),
        in_specs=[pl.BlockSpec((1, tm, kw), lambda bi, i, j: (bi, i, 0)),
                  pl.BlockSpec((kw, tn), lambda bi, i, j: (0, j)),
                  pl.BlockSpec((1, tm, tn), lambda bi, i, j: (bi, i, j)),
                  pl.BlockSpec((1, 1, tn), lambda bi, i, j: (bi, 0, gate_blk_of(tn) + j))],
        out_specs=pl.BlockSpec((1, tm, tn), lambda bi, i, j: (bi, i, j)),
        out_shape=jax.ShapeDtypeStruct((b, t, d), F32),
        compiler_params=_params("parallel", "parallel", "arbitrary"),
        name="out_proj",
    )(a, w, x, mod3)


def _ff1_kernel(n_cast, h_ref, w_ref, c_ref, o_ref, co_ref):
    z = jnp.maximum(_dot(h_ref[0], w_ref[...]), 0.0)
    o_ref[0] = (z * z).astype(o_ref.dtype)
    _cast_block(c_ref, co_ref, _grid_step(3), n_cast)


def _ff1_call(h, w, cast_src):
    b, t, d = h.shape
    n = w.shape[1]
    tm = _tile(t, 1024, SUBLANES)
    tn = _tile(n, 1024)
    grid = (b, t // tm, n // tn)
    n_cast, c_in, c_out, c_shape = _cast_specs(cast_src, grid)
    return pl.pallas_call(
        functools.partial(_ff1_kernel, n_cast),
        grid=grid,
        in_specs=[pl.BlockSpec((1, tm, d), lambda bi, i, j: (bi, i, 0)),
                  pl.BlockSpec((d, tn), lambda bi, i, j: (0, j)),
                  c_in],
        out_specs=[pl.BlockSpec((1, tm, tn), lambda bi, i, j: (bi, i, j)), c_out],
        out_shape=[jax.ShapeDtypeStruct((b, t, n), BF16), c_shape],
        compiler_params=_params("arbitrary", "arbitrary", "arbitrary"),
        name="ff1",
    )(h, w, cast_src)


def _ff2_kernel(a_ref, w_ref, x_ref, gate_ref, o_ref, acc_ref):
    k = pl.program_id(3)

    @pl.when(k == 0)
    def _():
        acc_ref[...] = jnp.zeros_like(acc_ref)

    acc_ref[...] += _dot(a_ref[0], w_ref[...])

    @pl.when(k == pl.num_programs(3) - 1)
    def _():
        o_ref[0] = x_ref[0] + gate_ref[0] * acc_ref[...]


def _ff2_call(a, w, x, mod3, gate_blk_of):
    b, t, kw = a.shape
    d = w.shape[1]
    tm = _tile(t, 1024, SUBLANES)
    tn = _tile(d, 1024)
    tk = _tile(kw, 4096)
    return pl.pallas_call(
        _ff2_kernel,
        grid=(b, t // tm, d // tn, kw // tk),
        in_specs=[pl.BlockSpec((1, tm, tk), lambda bi, i, j, k: (bi, i, k)),
                  pl.BlockSpec((tk, tn), lambda bi, i, j, k: (k, j)),
                  pl.BlockSpec((1, tm, tn), lambda bi, i, j, k: (bi, i, j)),
                  pl.BlockSpec((1, 1, tn), lambda bi, i, j, k: (bi, 0, gate_blk_of(tn) + j))],
        out_specs=pl.BlockSpec((1, tm, tn), lambda bi, i, j, k: (bi, i, j)),
        out_shape=jax.ShapeDtypeStruct((b, t, d), F32),
        scratch_shapes=[pltpu.VMEM((tm, tn), F32)],
        compiler_params=_params("parallel", "parallel", "parallel", "arbitrary"),
        name="ff2",
    )(a, w, x, mod3)


def _rope_tables(t_len):
    pos = jnp.arange(t_len)
    row = (pos // GRID_W).astype(F32)
    col = (pos % GRID_W).astype(F32)
    n_freq = LANES // 4
    inv_freq = ROPE_BASE ** (-jnp.arange(n_freq, dtype=F32) / n_freq)
    ang = jnp.concatenate([row[:, None] * inv_freq, col[:, None] * inv_freq], axis=-1)
    cos, sin = jnp.cos(ang), jnp.sin(ang)
    cos_rep = jnp.repeat(cos, 2, axis=-1)
    sin_signed = jnp.stack([-sin, sin], axis=-1).reshape(t_len, LANES)
    return cos_rep, sin_signed


def kernel(x, c, ctx, c_ctx, w_mod, b_mod, norm1_g, norm2_g, w_in, hg_lb_logits, hg_norm_g, ret_decay_logit,
           w_branch_hgrn, w_branch_ret, w_out, w_ff1, w_ff2, final_norm_g):
    b, t, d = x.shape
    assert w_mod.shape[0] == 1 and hg_lb_logits.shape[0] == 2, "one layer"
    assert b + 1 <= SUBLANES
    hw = w_branch_hgrn.shape[1]
    heads = hw // LANES
    assert w_branch_ret.shape[1] == hw and w_in.shape[2] == 9 * hw + 2 * d
    n_state = 5 * hw

    c_rows = jnp.zeros((SUBLANES, d), F32).at[:b].set(c).at[b].set(c_ctx)
    mod_a = _mod_call(c_rows, w_mod[0], b_mod[0], 2 * d).reshape(SUBLANES, 1, 2 * d)

    h_x = _norm_mod_call(x, norm1_g[0], mod_a, lambda bi: bi, 0, 1)
    h_c = _norm_mod_call(ctx, norm1_g[0], mod_a, lambda bi: b, 0, 1)
    p_x, w_ff1_b = _in_proj_call(h_x, w_in[0], w_in.shape[2], 512, cast_src=w_ff1[0])
    p_c = _in_proj_call(h_c, w_in[0], n_state, 256)

    y_hg, w_out_b = _hgrn_call(p_x, p_c, hg_lb_logits, hg_norm_g[0], heads, w_out[0])
    cos_rep, sin_signed = _rope_tables(t)
    decay_logit = jnp.broadcast_to(ret_decay_logit[0].T[:, :, None], (heads, 2, LANES))
    y_ret, w_bh_b, w_br_b, mod_b = _ret_call(p_x, p_c, decay_logit, cos_rep, sin_signed, heads,
                                             w_branch_hgrn[0], w_branch_ret[0], c_rows, w_mod[0], b_mod[0], 2 * d)
    mod_b = mod_b.reshape(SUBLANES, 1, 4 * d)

    merged = _merge_call(y_hg, y_ret, w_bh_b, w_br_b, p_x, 9 * heads, 9 * heads + d // LANES)
    x1 = _out_proj_call(merged, w_out_b, x, mod_b, lambda tn: 0)
    h2 = _norm_mod_call(x1, norm2_g[0], mod_b, lambda bi: bi, 1, 2)
    act, w_ff2_b = _ff1_call(h2, w_ff1_b, w_ff2[0])
    x2 = _ff2_call(act, w_ff2_b, x1, mod_b, lambda tn: 3 * d // tn)
    return _final_norm_call(x2, final_norm_g)
```

```python
import functools

import jax
import jax.numpy as jnp
import numpy as np
from jax import lax
from jax.experimental import pallas as pl
from jax.experimental.pallas import tpu as pltpu

F32 = jnp.float32
BF16 = jnp.bfloat16

LANES = 128
SUBLANES = 8
BF16_ROWS = 16
VMEM_LIMIT = 60 * 2 ** 20
EPS = 1e-6
ROPE_BASE = 10000.0
GRID_W = 64
HG_CHUNK = 128
RET_CHUNK = 256
HG_STATE_UNROLL = 4
HG_OUT_UNROLL = 4
RET_UNROLL = 4
W_IN_SPLIT = 4


def _params(*sem):
    return pltpu.CompilerParams(dimension_semantics=sem, vmem_limit_bytes=VMEM_LIMIT)


def _tile(n, target, unit=LANES):
    t = min(n, target) // unit * unit
    while n % t:
        t -= unit
    return t


def _silu(z):
    return z * jax.nn.sigmoid(z)


def _dot(a, b):
    return jnp.dot(a, b, preferred_element_type=F32)


def _dot_nt(a, b):
    return lax.dot_general(a, b, (((1,), (1,)), ((), ())), preferred_element_type=F32)


def _dot_tn(a, b):
    return lax.dot_general(a, b, (((0,), (0,)), ((), ())), preferred_element_type=F32)


def _grid_step(rank):
    step = pl.program_id(0)
    for axis in range(1, rank):
        step = step * pl.num_programs(axis) + pl.program_id(axis)
    return step


def _linear_step(ids, sizes):
    step = ids[0]
    for i, n in zip(ids[1:], sizes[1:]):
        step = step * n + i
    return step


def _cast_specs(src, grid):
    rows, cols = src.shape
    steps = int(np.prod(grid))
    n_blocks = 1
    while n_blocks * 2 <= steps and rows % (n_blocks * 2) == 0 and rows // (n_blocks * 2) >= BF16_ROWS:
        n_blocks *= 2

    def idx(*ids):
        return (jnp.minimum(_linear_step(ids, grid), n_blocks - 1), 0)

    spec = pl.BlockSpec((rows // n_blocks, cols), idx)
    return n_blocks, spec, spec, jax.ShapeDtypeStruct(src.shape, BF16)


def _cast_block(src_ref, dst_ref, step, n_blocks):
    @pl.when(step < n_blocks)
    def _():
        dst_ref[...] = src_ref[...].astype(dst_ref.dtype)


def _mod_kernel(c_ref, w_ref, b_ref, o_ref):
    s = _silu(c_ref[...]).astype(BF16)
    o_ref[...] = _dot(s, w_ref[...].astype(BF16)) + b_ref[...]


def _mod_call(c_rows, w_mod, b_mod, n):
    r, d = c_rows.shape
    tn = _tile(n, 512)
    return pl.pallas_call(
        _mod_kernel,
        grid=(n // tn,),
        in_specs=[pl.BlockSpec((r, d), lambda j: (0, 0)),
                  pl.BlockSpec((d, tn), lambda j: (0, j)),
                  pl.BlockSpec((1, tn), lambda j: (0, j))],
        out_specs=pl.BlockSpec((r, tn), lambda j: (0, j)),
        out_shape=jax.ShapeDtypeStruct((r, n), F32),
        compiler_params=_params("parallel"),
        name="mod",
    )(c_rows, w_mod, b_mod.reshape(1, -1))


def _norm_mod_kernel(x_ref, g_ref, scale_ref, shift_ref, o_ref):
    x = x_ref[0]
    y = x * lax.rsqrt(jnp.mean(x * x, axis=-1, keepdims=True) + EPS) * g_ref[...]
    o_ref[0] = (y * (1.0 + scale_ref[0]) + shift_ref[0]).astype(o_ref.dtype)


def _norm_mod_call(x, gain, mod3, row_of_batch, shift_blk, scale_blk):
    b, t, d = x.shape
    tm = _tile(t, 512, SUBLANES)
    return pl.pallas_call(
        _norm_mod_kernel,
        grid=(b, t // tm),
        in_specs=[pl.BlockSpec((1, tm, d), lambda bi, i: (bi, i, 0)),
                  pl.BlockSpec((1, d), lambda bi, i: (0, 0)),
                  pl.BlockSpec((1, 1, d), lambda bi, i: (row_of_batch(bi), 0, scale_blk)),
                  pl.BlockSpec((1, 1, d), lambda bi, i: (row_of_batch(bi), 0, shift_blk))],
        out_specs=pl.BlockSpec((1, tm, d), lambda bi, i: (bi, i, 0)),
        out_shape=jax.ShapeDtypeStruct((b, t, d), BF16),
        compiler_params=_params("parallel", "parallel"),
        name="norm_mod",
    )(x, gain.reshape(1, d), mod3, mod3)


def _final_norm_kernel(x_ref, g_ref, o_ref):
    x = x_ref[0]
    o_ref[0] = x * lax.rsqrt(jnp.mean(x * x, axis=-1, keepdims=True) + EPS) * g_ref[...]


def _final_norm_call(x, gain):
    b, t, d = x.shape
    tm = _tile(t, 512, SUBLANES)
    return pl.pallas_call(
        _final_norm_kernel,
        grid=(b, t // tm),
        in_specs=[pl.BlockSpec((1, tm, d), lambda bi, i: (bi, i, 0)),
                  pl.BlockSpec((1, d), lambda bi, i: (0, 0))],
        out_specs=pl.BlockSpec((1, tm, d), lambda bi, i: (bi, i, 0)),
        out_shape=jax.ShapeDtypeStruct((b, t, d), F32),
        compiler_params=_params("parallel", "parallel"),
        name="final_norm",
    )(x, gain.reshape(1, d))


def _in_proj_tile(h_ref, w_refs, o_ref, wb_s):
    @pl.when((pl.program_id(1) == 0) & (pl.program_id(2) == 0))
    def _():
        rows = w_refs[0].shape[0]
        for q, w_ref in enumerate(w_refs):
            wb_s[q * rows:(q + 1) * rows, :] = w_ref[...].astype(BF16)

    acc = _dot(h_ref[0], wb_s[...])
    for j in range(o_ref.shape[1]):
        o_ref[0, j] = acc[:, j * LANES:(j + 1) * LANES]


def _in_proj_kernel(h_ref, *refs):
    w_refs, (o_ref, wb_s) = refs[:W_IN_SPLIT], refs[W_IN_SPLIT:]
    _in_proj_tile(h_ref, w_refs, o_ref, wb_s)


def _in_proj_cast_kernel(n_cast, h_ref, *refs):
    w_refs, (c_ref, o_ref, co_ref, wb_s) = refs[:W_IN_SPLIT], refs[W_IN_SPLIT:]
    _in_proj_tile(h_ref, w_refs, o_ref, wb_s)
    _cast_block(c_ref, co_ref, _grid_step(3), n_cast)


def _in_proj_call(h, w, n, tm_target, cast_src=None):
    b, t, d = h.shape
    tm = _tile(t, tm_target, SUBLANES)
    tn = _tile(n, 1024)
    nj, ni = n // tn, t // tm
    grid = (nj, b, ni)
    dk = d // W_IN_SPLIT
    period = b * ni

    def w_spec(q):
        switch = 1 + q * max(1, (period - 2) // W_IN_SPLIT)
        return pl.BlockSpec((dk, tn), lambda j, bi, i: (q, jnp.minimum(j + (bi * ni + i >= switch), nj - 1)))

    in_specs = [pl.BlockSpec((1, tm, d), lambda j, bi, i: (bi, i, 0))] + [w_spec(q) for q in range(W_IN_SPLIT)]
    out_specs = pl.BlockSpec((1, tn // LANES, tm, LANES), lambda j, bi, i: (bi, j, i, 0))
    out_shape = jax.ShapeDtypeStruct((b, n // LANES, t, LANES), F32)
    common = dict(grid=grid, scratch_shapes=[pltpu.VMEM((d, tn), BF16)],
                  compiler_params=_params("arbitrary", "arbitrary", "arbitrary"),
                  name="in_proj")
    ws = (w,) * W_IN_SPLIT
    if cast_src is None:
        return pl.pallas_call(_in_proj_kernel, in_specs=in_specs, out_specs=out_specs, out_shape=out_shape,
                              **common)(h, *ws)
    n_cast, c_in, c_out, c_shape = _cast_specs(cast_src, grid)
    return pl.pallas_call(functools.partial(_in_proj_cast_kernel, n_cast), in_specs=in_specs + [c_in],
                          out_specs=[out_specs, c_out], out_shape=[out_shape, c_shape],
                          **common)(h, *ws, cast_src)


def _roll_in_groups(x, shift):
    n, w = x.shape
    x3 = x.reshape(n // SUBLANES, SUBLANES, w)
    return pltpu.roll(x3, shift % SUBLANES, axis=1).reshape(n, w)


def _cumsum_rows(x, reverse):
    n = x.shape[0]
    row = lax.broadcasted_iota(jnp.int32, x.shape, 0)
    d = 1
    while d < n:
        if reverse:
            x = x + jnp.where(row < n - d, pltpu.roll(x, n - d, axis=0), 0.0)
        else:
            x = x + jnp.where(row >= d, pltpu.roll(x, d, axis=0), 0.0)
        d *= 2
    return x


def _block_row_bcast(x, block, offset):
    n, w = x.shape
    if block >= SUBLANES:
        xb = x.reshape(n // block, block, w)
        return jnp.broadcast_to(xb[:, offset:offset + 1, :], xb.shape).reshape(n, w)
    m = lax.broadcasted_iota(jnp.int32, x.shape, 0) & (block - 1)
    out = x
    for mm in range(block):
        if mm != offset:
            out = jnp.where(m == mm, _roll_in_groups(x, mm - offset), out)
    return out


def _hgrn_gates(z, lb):
    f = lb + (1.0 - lb) * jax.nn.sigmoid(z)
    return 1.0 - f, jnp.log2(f)


def _hgrn_level_halves(q, kf, kb, cf, cb, h):
    n = q.shape[0]
    qm, km = [], []
    for lo in range(0, n, 2 * h):
        mid, hi = lo + h, lo + 2 * h
        bf = jnp.broadcast_to(cf[mid - 1:mid, :], (h, LANES))
        bb = jnp.broadcast_to(cb[mid:mid + 1, :], (h, LANES))
        qm += [q[lo:mid] * jnp.exp2(cb[lo:mid] - bb), q[mid:hi] * jnp.exp2(cf[mid:hi] - bf)]
        km += [kf[lo:mid] * jnp.exp2(bf - cf[lo:mid]), kb[mid:hi] * jnp.exp2(bb - cb[mid:hi])]
    return jnp.concatenate(qm, axis=0), jnp.concatenate(km, axis=0)


def _state_update(s_ref, k, cum, v, reverse):
    n = cum.shape[0]
    edge = cum[0:1, :] if reverse else cum[n - 1:n, :]
    k_dec = (k * jnp.exp2(edge - cum)).astype(BF16)
    s_ref[...] = s_ref[...] * jnp.exp2(edge) + _dot_tn(v.astype(BF16), k_dec)


def _hgrn_kernel(n_cast, i_ref, ff_ref, fb_ref, q_ref, g_ref, ci_ref, cff_ref, cfb_ref, lbl_ref, ng_ref, c_ref,
                 y_ref, co_ref, kf_s, kb_s, cf_s, cb_s, st_s, sf_s, sb_s):
    _cast_block(c_ref, co_ref, _grid_step(2), n_cast)
    t_len, tc_len = i_ref.shape[2], ci_ref.shape[2]
    ch = HG_CHUNK
    cc = min(ch, tc_len)
    n, nc = t_len // ch, tc_len // cc

    l0, l1 = lbl_ref[0], lbl_ref[1]
    mx = jnp.maximum(l0, l1)
    e0, e1 = jnp.exp(l0 - mx), jnp.exp(l1 - mx)
    lb = e0 / (e0 + e1)
    lb_f, lb_b = lb[0:1], lb[1:2]

    sf_s[...] = jnp.zeros_like(sf_s)
    sb_s[...] = jnp.zeros_like(sb_s)

    def ctx_body(c, carry):
        rf = pl.multiple_of(c * cc, cc)
        rb = pl.multiple_of((nc - 1 - c) * cc, cc)
        k, lf = _hgrn_gates(cff_ref[0, 0, pl.ds(rf, cc), :], lb_f)
        _state_update(sf_s, k, _cumsum_rows(lf, False), ci_ref[0, 0, pl.ds(rf, cc), :], False)
        k, lf = _hgrn_gates(cfb_ref[0, 0, pl.ds(rb, cc), :], lb_b)
        _state_update(sb_s, k, _cumsum_rows(lf, True), ci_ref[0, 0, pl.ds(rb, cc), :], True)
        return carry

    lax.fori_loop(0, nc, ctx_body, 0)

    def state_body(c, carry):
        cb = n - 1 - c
        rf = pl.multiple_of(c * ch, ch)
        rb = pl.multiple_of(cb * ch, ch)
        k, lf = _hgrn_gates(ff_ref[0, 0, pl.ds(rf, ch), :], lb_f)
        cum = _cumsum_rows(lf, False)
        kf_s[pl.ds(rf, ch), :] = k
        cf_s[pl.ds(rf, ch), :] = cum
        st_s[c, :, 0:LANES] = sf_s[...].astype(BF16)
        _state_update(sf_s, k, cum, i_ref[0, 0, pl.ds(rf, ch), :], False)
        k, lf = _hgrn_gates(fb_ref[0, 0, pl.ds(rb, ch), :], lb_b)
        cum = _cumsum_rows(lf, True)
        kb_s[pl.ds(rb, ch), :] = k
        cb_s[pl.ds(rb, ch), :] = cum
        st_s[cb, :, LANES:2 * LANES] = sb_s[...].astype(BF16)
        _state_update(sb_s, k, cum, i_ref[0, 0, pl.ds(rb, ch), :], True)
        return carry

    lax.fori_loop(0, n, state_body, 0, unroll=HG_STATE_UNROLL)

    pair_xor = (lax.broadcasted_iota(jnp.int32, (ch, ch), 0) ^ lax.broadcasted_iota(jnp.int32, (ch, ch), 1))
    row = lax.broadcasted_iota(jnp.int32, (ch, LANES), 0)
    q_scale = LANES ** -0.5

    def out_body(c, carry):
        r = pl.multiple_of(c * ch, ch)
        q = _silu(q_ref[0, 0, pl.ds(r, ch), :]) * q_scale
        kf, kb = kf_s[pl.ds(r, ch), :], kb_s[pl.ds(r, ch), :]
        cf, cb = cf_s[pl.ds(r, ch), :], cb_s[pl.ds(r, ch), :]
        v = i_ref[0, 0, pl.ds(r, ch), :].astype(BF16)
        a = jnp.where(pair_xor == 0, _dot_nt(q.astype(BF16), (kf + kb).astype(BF16)), 0.0)
        odd = (row & 1) != 0
        qm = q * jnp.where(odd, 1.0 - kf, 1.0 - kb)
        a = jnp.where(pair_xor == 1, _dot_nt(qm.astype(BF16), jnp.where(odd, kb, kf).astype(BF16)), a)
        h, log_h = 2, 1
        while h < ch:
            if h >= SUBLANES:
                qm, km = _hgrn_level_halves(q, kf, kb, cf, cb, h)
            else:
                second = (row & h) != 0
                df = cf - _block_row_bcast(cf, 2 * h, h - 1)
                db = cb - _block_row_bcast(cb, 2 * h, h)
                qm = q * jnp.exp2(jnp.where(second, df, db))
                km = jnp.where(second, kb, kf) * jnp.exp2(-jnp.where(second, db, df))
            a = jnp.where((pair_xor >> log_h) == 1, _dot_nt(qm.astype(BF16), km.astype(BF16)), a)
            h, log_h = 2 * h, log_h + 1
        q_in = jnp.concatenate([q * jnp.exp2(cf), q * jnp.exp2(cb)], axis=1).astype(BF16)
        o = _dot(a.astype(BF16), v) + _dot_nt(q_in, st_s[c])
        y = o * lax.rsqrt(jnp.mean(o * o, axis=-1, keepdims=True) + EPS) * ng_ref[...]
        y_ref[0, pl.ds(r, ch), :] = (y * _silu(g_ref[0, 0, pl.ds(r, ch), :])).astype(y_ref.dtype)
        return carry

    lax.fori_loop(0, n, out_body, 0, unroll=HG_OUT_UNROLL)


def _hgrn_call(p_x, p_c, lb_logits, norm_g, heads, cast_src):
    b, _, t, _ = p_x.shape
    tc = p_c.shape[2]
    grid = (b, heads)
    n_cast, c_in, c_out, c_shape = _cast_specs(cast_src, grid)

    def col(off):
        return pl.BlockSpec((1, 1, t, LANES), lambda bi, h: (bi, off * heads + h, 0, 0))

    def ctx_col(off):
        return pl.BlockSpec((1, 1, tc, LANES), lambda bi, h: (bi, off * heads + h, 0, 0))

    return pl.pallas_call(
        functools.partial(_hgrn_kernel, n_cast),
        grid=grid,
        in_specs=[col(0), col(1), col(2), col(5), col(6), ctx_col(0), ctx_col(1), ctx_col(2),
                  pl.BlockSpec((2, 2, LANES), lambda bi, h: (0, 0, h)),
                  pl.BlockSpec((1, LANES), lambda bi, h: (0, 0)),
                  c_in],
        out_specs=[pl.BlockSpec((1, t, LANES), lambda bi, h: (bi, 0, h)), c_out],
        out_shape=[jax.ShapeDtypeStruct((b, t, heads * LANES), BF16), c_shape],
        scratch_shapes=[pltpu.VMEM((t, LANES), F32), pltpu.VMEM((t, LANES), F32),
                        pltpu.VMEM((t, LANES), F32), pltpu.VMEM((t, LANES), F32),
                        pltpu.VMEM((t // HG_CHUNK, LANES, 2 * LANES), BF16),
                        pltpu.VMEM((LANES, LANES), F32), pltpu.VMEM((LANES, LANES), F32)],
        compiler_params=_params("arbitrary", "arbitrary"),
        name="hgrn",
    )(p_x, p_x, p_x, p_x, p_x, p_c, p_c, p_c, lb_logits, norm_g.reshape(1, LANES), cast_src)


def _rope(a, cos, sin_signed, swap=None):
    if swap is None:
        lane = lax.broadcasted_iota(jnp.int32, a.shape, 1)
        partner = jnp.where((lane & 1) == 0, pltpu.roll(a, LANES - 1, axis=1), pltpu.roll(a, 1, axis=1))
    else:
        partner = _dot(a.astype(BF16), swap)
    return a * cos + partner * sin_signed


def _ret_state_update(s_ref, k, v, k_decay, s_decay):
    s_ref[...] = s_ref[...] * s_decay + _dot_tn(v.astype(BF16), (k * k_decay).astype(BF16))


def _ret_kernel(n_cast, q_ref, k_ref, v_ref, g_ref, ck_ref, cv_ref, dl_ref, cos_ref, sin_ref, sw_ref, c1_ref, c2_ref,
                mc_ref, mw_ref, mb_ref, y_ref, co1_ref, co2_ref, mo_ref, kr_s, st_s, sf_s, sb_s):
    _cast_block(c1_ref, co1_ref, _grid_step(2), n_cast)
    _cast_block(c2_ref, co2_ref, _grid_step(2), n_cast)
    _mod_kernel(mc_ref, mw_ref, mb_ref, mo_ref)
    t_len, tc_len = q_ref.shape[2], ck_ref.shape[2]
    ch = RET_CHUNK
    cc = min(ch, tc_len)
    n, nc = t_len // ch, tc_len // cc
    k_scale = LANES ** -0.5

    dl = dl_ref[0]
    gam = jnp.minimum(dl, 0.0) - jnp.log1p(jnp.exp(-jnp.abs(dl)))
    gam_f, gam_b = gam[0:1], gam[1:2]

    def decays(m):
        pos = lax.broadcasted_iota(jnp.int32, (m, LANES), 0).astype(F32)
        return dict(kf=jnp.exp(gam_f * (m - 1.0 - pos)), kb=jnp.exp(gam_b * pos),
                    qf=jnp.exp(gam_f * (pos + 1.0)), qb=jnp.exp(gam_b * (m - pos)),
                    sf=jnp.exp(gam_f * m), sb=jnp.exp(gam_b * m))

    sf_s[...] = jnp.zeros_like(sf_s)
    sb_s[...] = jnp.zeros_like(sb_s)

    dc = decays(cc)

    def ctx_body(c, carry):
        rf = pl.multiple_of(c * cc, cc)
        rb = pl.multiple_of((nc - 1 - c) * cc, cc)
        _ret_state_update(sf_s, ck_ref[0, 0, pl.ds(rf, cc), :] * k_scale, cv_ref[0, 0, pl.ds(rf, cc), :],
                          dc["kf"], dc["sf"])
        _ret_state_update(sb_s, ck_ref[0, 0, pl.ds(rb, cc), :] * k_scale, cv_ref[0, 0, pl.ds(rb, cc), :],
                          dc["kb"], dc["sb"])
        return carry

    lax.fori_loop(0, nc, ctx_body, 0)

    def rope_body(c, carry):
        r = pl.multiple_of(c * ch, ch)
        kr_s[pl.ds(r, ch), :] = _rope(k_ref[0, 0, pl.ds(r, ch), :] * k_scale,
                                      cos_ref[pl.ds(r, ch), :], sin_ref[pl.ds(r, ch), :], sw_ref[...])
        return carry

    lax.fori_loop(0, n, rope_body, 0, unroll=RET_UNROLL)

    dd = decays(ch)

    def state_body(c, carry):
        cb = n - 1 - c
        rf = pl.multiple_of(c * ch, ch)
        rb = pl.multiple_of(cb * ch, ch)
        st_s[c, :, 0:LANES] = sf_s[...].astype(BF16)
        _ret_state_update(sf_s, kr_s[pl.ds(rf, ch), :], v_ref[0, 0, pl.ds(rf, ch), :], dd["kf"], dd["sf"])
        st_s[cb, :, LANES:2 * LANES] = sb_s[...].astype(BF16)
        _ret_state_update(sb_s, kr_s[pl.ds(rb, ch), :], v_ref[0, 0, pl.ds(rb, ch), :], dd["kb"], dd["sb"])
        return carry

    lax.fori_loop(0, n, state_body, 0, unroll=RET_UNROLL)

    ti = lax.broadcasted_iota(jnp.int32, (ch, ch), 0)
    si = lax.broadcasted_iota(jnp.int32, (ch, ch), 1)
    lag = (ti - si).astype(F32)
    g_f = jnp.broadcast_to(gam_f[:, 0:1], (ch, ch))
    g_b = jnp.broadcast_to(gam_b[:, 0:1], (ch, ch))
    decay = jnp.where(ti > si, jnp.exp(g_f * jnp.maximum(lag, 0.0)),
                      jnp.where(ti < si, jnp.exp(g_b * jnp.maximum(-lag, 0.0)), 2.0))

    def out_body(c, carry):
        r = pl.multiple_of(c * ch, ch)
        q = _rope(q_ref[0, 0, pl.ds(r, ch), :], cos_ref[pl.ds(r, ch), :], sin_ref[pl.ds(r, ch), :])
        k = kr_s[pl.ds(r, ch), :].astype(BF16)
        v = v_ref[0, 0, pl.ds(r, ch), :].astype(BF16)
        a = _dot_nt(q.astype(BF16), k) * decay
        q_in = jnp.concatenate([q * dd["qf"], q * dd["qb"]], axis=1).astype(BF16)
        o = _dot(a.astype(BF16), v) + _dot_nt(q_in, st_s[c])
        oc = o - jnp.mean(o, axis=-1, keepdims=True)
        y = oc * lax.rsqrt(jnp.mean(oc * oc, axis=-1, keepdims=True) + EPS)
        y_ref[0, pl.ds(r, ch), :] = (y * _silu(g_ref[0, 0, pl.ds(r, ch), :])).astype(y_ref.dtype)
        return carry

    lax.fori_loop(0, n, out_body, 0, unroll=RET_UNROLL)


def _ret_call(p_x, p_c, decay_logit, cos, sin_signed, heads, cast_src1, cast_src2, c_rows, w_mod, b_mod, mod_col0):
    b, _, t, _ = p_x.shape
    tc = p_c.shape[2]
    grid = (b, heads)
    r, d = c_rows.shape
    mod_cols = w_mod.shape[1] - mod_col0
    mod_tn = mod_cols // (b * heads)
    assert mod_tn % LANES == 0 and mod_col0 % mod_tn == 0
    mod_blk0 = mod_col0 // mod_tn
    assert cast_src1.shape == cast_src2.shape
    n_cast, c_in, c_out, c_shape = _cast_specs(cast_src1, grid)
    swap = jnp.asarray(np.eye(LANES, dtype=np.float32)[:, np.arange(LANES) ^ 1], BF16)

    def col(off):
        return pl.BlockSpec((1, 1, t, LANES), lambda bi, h: (bi, off * heads + h, 0, 0))

    def ctx_col(off):
        return pl.BlockSpec((1, 1, tc, LANES), lambda bi, h: (bi, off * heads + h, 0, 0))

    return pl.pallas_call(
        functools.partial(_ret_kernel, n_cast),
        grid=grid,
        in_specs=[col(7), col(3), col(4), col(8), ctx_col(3), ctx_col(4),
                  pl.BlockSpec((1, 2, LANES), lambda bi, h: (h, 0, 0)),
                  pl.BlockSpec((t, LANES), lambda bi, h: (0, 0)),
                  pl.BlockSpec((t, LANES), lambda bi, h: (0, 0)),
                  pl.BlockSpec((LANES, LANES), lambda bi, h: (0, 0)),
                  c_in, c_in,
                  pl.BlockSpec((r, d), lambda bi, h: (0, 0)),
                  pl.BlockSpec((d, mod_tn), lambda bi, h: (0, mod_blk0 + bi * heads + h)),
                  pl.BlockSpec((1, mod_tn), lambda bi, h: (0, mod_blk0 + bi * heads + h))],
        out_specs=[pl.BlockSpec((1, t, LANES), lambda bi, h: (bi, 0, h)), c_out, c_out,
                   pl.BlockSpec((r, mod_tn), lambda bi, h: (0, bi * heads + h))],
        out_shape=[jax.ShapeDtypeStruct((b, t, heads * LANES), BF16), c_shape, c_shape,
                   jax.ShapeDtypeStruct((r, mod_cols), F32)],
        scratch_shapes=[pltpu.VMEM((t, LANES), F32),
                        pltpu.VMEM((t // RET_CHUNK, LANES, 2 * LANES), BF16),
                        pltpu.VMEM((LANES, LANES), F32), pltpu.VMEM((LANES, LANES), F32)],
        compiler_params=_params("arbitrary", "arbitrary"),
        name="ret",
    )(p_x, p_x, p_x, p_x, p_c, p_c, decay_logit, cos, sin_signed, swap, cast_src1, cast_src2,
      c_rows, w_mod, b_mod.reshape(1, -1))


def _merge_kernel(yh_ref, yr_ref, wh_ref, wr_ref, gh_ref, gr_ref, o_ref):
    acc_h = _dot(yh_ref[0], wh_ref[...])
    acc_r = _dot(yr_ref[0], wr_ref[...])
    for j in range(gh_ref.shape[1]):
        sl = slice(j * LANES, (j + 1) * LANES)
        o_ref[0, :, sl] = (jax.nn.sigmoid(gh_ref[0, j]) * acc_h[:, sl]
                           + jax.nn.sigmoid(gr_ref[0, j]) * acc_r[:, sl]).astype(o_ref.dtype)


def _merge_call(y_hg, y_ret, w_bh, w_br, p_x, gate_h_blk, gate_r_blk):
    b, t, kw = y_hg.shape
    d = w_bh.shape[1]
    tm = _tile(t, 1024, SUBLANES)
    tn = _tile(d, 512)
    nb = tn // LANES
    return pl.pallas_call(
        _merge_kernel,
        grid=(b, t // tm, d // tn),
        in_specs=[pl.BlockSpec((1, tm, kw), lambda bi, i, j: (bi, i, 0)),
                  pl.BlockSpec((1, tm, kw), lambda bi, i, j: (bi, i, 0)),
                  pl.BlockSpec((kw, tn), lambda bi, i, j: (0, j)),
                  pl.BlockSpec((kw, tn), lambda bi, i, j: (0, j)),
                  pl.BlockSpec((1, nb, tm, LANES), lambda bi, i, j: (bi, gate_h_blk // nb + j, i, 0)),
                  pl.BlockSpec((1, nb, tm, LANES), lambda bi, i, j: (bi, gate_r_blk // nb + j, i, 0))],
        out_specs=pl.BlockSpec((1, tm, tn), lambda bi, i, j: (bi, i, j)),
        out_shape=jax.ShapeDtypeStruct((b, t, d), BF16),
        compiler_params=_params("parallel", "parallel", "arbitrary"),
        name="merge",
    )(y_hg, y_ret, w_bh, w_br, p_x, p_x)


def _gated_residual_kernel(a_ref, w_ref, x_ref, gate_ref, o_ref):
    o_ref[0] = x_ref[0] + gate_ref[0] * _dot(a_ref[0], w_ref[...])


def _out_proj_call(a, w, x, mod3, gate_blk_of):
    b, t, kw = a.shape
    d = w.shape[1]
    tm = _tile(t, 1024, SUBLANES)
    tn = _tile(d, 1024)
    return pl.pallas_call(
        _gated_residual_kernel,
        grid=(b, t // tm, d // tn),
        in_specs=[pl.BlockSpec((1, tm, kw), lambda bi, i, j: (bi, i, 0)),
                  pl.BlockSpec((kw, tn), lambda bi, i, j: (0, j)),
                  pl.BlockSpec((1, tm, tn), lambda bi, i, j: (bi, i, j)),
                  pl.BlockSpec((1, 1, tn), lambda bi, i, j: (bi, 0, gate_blk_of(tn) + j))],
        out_specs=pl.BlockSpec((1, tm, tn), lambda bi, i, j: (bi, i, j)),
        out_shape=jax.ShapeDtypeStruct((b, t, d), F32),
        compiler_params=_params("parallel", "parallel", "arbitrary"),
        name="out_proj",
    )(a, w, x, mod3)


def _ff1_kernel(n_cast, h_ref, w_ref, c_ref, o_ref, co_ref):
    z = jnp.maximum(_dot(h_ref[0], w_ref[...]), 0.0)
    o_ref[0] = (z * z).astype(o_ref.dtype)
    _cast_block(c_ref, co_ref, _grid_step(3), n_cast)


def _ff1_call(h, w, cast_src):
    b, t, d = h.shape
    n = w.shape[1]
    tm = _tile(t, 1024, SUBLANES)
    tn = _tile(n, 1024)
    grid = (b, t // tm, n // tn)
    n_cast, c_in, c_out, c_shape = _cast_specs(cast_src, grid)
    return pl.pallas_call(
        functools.partial(_ff1_kernel, n_cast),
        grid=grid,
        in_specs=[pl.BlockSpec((1, tm, d), lambda bi, i, j: (bi, i, 0)),
                  pl.BlockSpec((d, tn), lambda bi, i, j: (0, j)),
                  c_in],
        out_specs=[pl.BlockSpec((1, tm, tn), lambda bi, i, j: (bi, i, j)), c_out],
        out_shape=[jax.ShapeDtypeStruct((b, t, n), BF16), c_shape],
        compiler_params=_params("arbitrary", "arbitrary", "arbitrary"),
        name="ff1",
    )(h, w, cast_src)


def _ff2_kernel(a_ref, w_ref, x_ref, gate_ref, o_ref, acc_ref):
    k = pl.program_id(3)

    @pl.when(k == 0)
    def _():
        acc_ref[...] = jnp.zeros_like(acc_ref)

    acc_ref[...] += _dot(a_ref[0], w_ref[...])

    @pl.when(k == pl.num_programs(3) - 1)
    def _():
        o_ref[0] = x_ref[0] + gate_ref[0] * acc_ref[...]


def _ff2_call(a, w, x, mod3, gate_blk_of):
    b, t, kw = a.shape
    d = w.shape[1]
    tm = _tile(t, 1024, SUBLANES)
    tn = _tile(d, 1024)
    tk = _tile(kw, 4096)
    return pl.pallas_call(
        _ff2_kernel,
        grid=(b, t // tm, d // tn, kw // tk),
        in_specs=[pl.BlockSpec((1, tm, tk), lambda bi, i, j, k: (bi, i, k)),
                  pl.BlockSpec((tk, tn), lambda bi, i, j, k: (k, j)),
                  pl.BlockSpec((1, tm, tn), lambda bi, i, j, k: (bi, i, j)),
                  pl.BlockSpec((1, 1, tn), lambda bi, i, j, k: (bi, 0, gate_blk_of(tn) + j))],
        out_specs=pl.BlockSpec((1, tm, tn), lambda bi, i, j, k: (bi, i, j)),
        out_shape=jax.ShapeDtypeStruct((b, t, d), F32),
        scratch_shapes=[pltpu.VMEM((tm, tn), F32)],
        compiler_params=_params("parallel", "parallel", "parallel", "arbitrary"),
        name="ff2",
    )(a, w, x, mod3)


def _rope_tables(t_len):
    pos = jnp.arange(t_len)
    row = (pos // GRID_W).astype(F32)
    col = (pos % GRID_W).astype(F32)
    n_freq = LANES // 4
    inv_freq = ROPE_BASE ** (-jnp.arange(n_freq, dtype=F32) / n_freq)
    ang = jnp.concatenate([row[:, None] * inv_freq, col[:, None] * inv_freq], axis=-1)
    cos, sin = jnp.cos(ang), jnp.sin(ang)
    cos_rep = jnp.repeat(cos, 2, axis=-1)
    sin_signed = jnp.stack([-sin, sin], axis=-1).reshape(t_len, LANES)
    return cos_rep, sin_signed


def kernel(x, c, ctx, c_ctx, w_mod, b_mod, norm1_g, norm2_g, w_in, hg_lb_logits, hg_norm_g, ret_decay_logit,
           w_branch_hgrn, w_branch_ret, w_out, w_ff1, w_ff2, final_norm_g):
    b, t, d = x.shape
    assert w_mod.shape[0] == 1 and hg_lb_logits.shape[0] == 2, "one layer"
    assert b + 1 <= SUBLANES
    hw = w_branch_hgrn.shape[1]
    heads = hw // LANES
    assert w_branch_ret.shape[1] == hw and w_in.shape[2] == 9 * hw + 2 * d
    n_state = 5 * hw

    c_rows = jnp.zeros((SUBLANES, d), F32).at[:b].set(c).at[b].set(c_ctx)
    mod_a = _mod_call(c_rows, w_mod[0], b_mod[0], 2 * d).reshape(SUBLANES, 1, 2 * d)

    h_x = _norm_mod_call(x, norm1_g[0], mod_a, lambda bi: bi, 0, 1)
    h_c = _norm_mod_call(ctx, norm1_g[0], mod_a, lambda bi: b, 0, 1)
    p_x, w_ff1_b = _in_proj_call(h_x, w_in[0], w_in.shape[2], 512, cast_src=w_ff1[0])
    p_c = _in_proj_call(h_c, w_in[0], n_state, 256)

    y_hg, w_out_b = _hgrn_call(p_x, p_c, hg_lb_logits, hg_norm_g[0], heads, w_out[0])
    cos_rep, sin_signed = _rope_tables(t)
    decay_logit = jnp.broadcast_to(ret_decay_logit[0].T[:, :, None], (heads, 2, LANES))
    y_ret, w_bh_b, w_br_b, mod_b = _ret_call(p_x, p_c, decay_logit, cos_rep, sin_signed, heads,
                                             w_branch_hgrn[0], w_branch_ret[0], c_rows, w_mod[0], b_mod[0], 2 * d)
    mod_b = mod_b.reshape(SUBLANES, 1, 4 * d)

    merged = _merge_call(y_hg, y_ret, w_bh_b, w_br_b, p_x, 9 * heads, 9 * heads + d // LANES)
    x1 = _out_proj_call(merged, w_out_b, x, mod_b, lambda tn: 0)
    h2 = _norm_mod_call(x1, norm2_g[0], mod_b, lambda bi: bi, 1, 2)
    act, w_ff2_b = _ff1_call(h2, w_ff1_b, w_ff2[0])
    x2 = _ff2_call(act, w_ff2_b, x1, mod_b, lambda tn: 3 * d // tn)
    return _final_norm_call(x2, final_norm_g)
```

```python
import functools

import jax
import jax.numpy as jnp
import numpy as np
from jax import lax
from jax.experimental import pallas as pl
from jax.experimental.pallas import tpu as pltpu

F32 = jnp.float32
BF16 = jnp.bfloat16

LANES = 128
SUBLANES = 8
BF16_ROWS = 16
VMEM_LIMIT = 60 * 2 ** 20
EPS = 1e-6
ROPE_BASE = 10000.0
GRID_W = 64
HG_CHUNK = 128
RET_CHUNK = 256
HG_STATE_UNROLL = 8
HG_OUT_UNROLL = 8
RET_UNROLL = 4
W_IN_SPLIT = 4


def _params(*sem):
    return pltpu.CompilerParams(dimension_semantics=sem, vmem_limit_bytes=VMEM_LIMIT)


def _tile(n, target, unit=LANES):
    t = min(n, target) // unit * unit
    while n % t:
        t -= unit
    return t


def _silu(z):
    return z * jax.nn.sigmoid(z)


def _dot(a, b):
    return jnp.dot(a, b, preferred_element_type=F32)


def _dot_nt(a, b):
    return lax.dot_general(a, b, (((1,), (1,)), ((), ())), preferred_element_type=F32)


def _dot_tn(a, b):
    return lax.dot_general(a, b, (((0,), (0,)), ((), ())), preferred_element_type=F32)


def _grid_step(rank):
    step = pl.program_id(0)
    for axis in range(1, rank):
        step = step * pl.num_programs(axis) + pl.program_id(axis)
    return step


def _linear_step(ids, sizes):
    step = ids[0]
    for i, n in zip(ids[1:], sizes[1:]):
        step = step * n + i
    return step


def _cast_specs(src, grid):
    rows, cols = src.shape
    steps = int(np.prod(grid))
    n_blocks = 1
    while n_blocks * 2 <= steps and rows % (n_blocks * 2) == 0 and rows // (n_blocks * 2) >= BF16_ROWS:
        n_blocks *= 2

    def idx(*ids):
        return (jnp.minimum(_linear_step(ids, grid), n_blocks - 1), 0)

    spec = pl.BlockSpec((rows // n_blocks, cols), idx)
    return n_blocks, spec, spec, jax.ShapeDtypeStruct(src.shape, BF16)


def _cast_block(src_ref, dst_ref, step, n_blocks):
    @pl.when(step < n_blocks)
    def _():
        dst_ref[...] = src_ref[...].astype(dst_ref.dtype)


def _mod_kernel(c_ref, w_ref, b_ref, o_ref):
    s = _silu(c_ref[...]).astype(BF16)
    o_ref[...] = _dot(s, w_ref[...].astype(BF16)) + b_ref[...]


def _mod_call(c_rows, w_mod, b_mod, n):
    r, d = c_rows.shape
    tn = _tile(n, 512)
    return pl.pallas_call(
        _mod_kernel,
        grid=(n // tn,),
        in_specs=[pl.BlockSpec((r, d), lambda j: (0, 0)),
                  pl.BlockSpec((d, tn), lambda j: (0, j)),
                  pl.BlockSpec((1, tn), lambda j: (0, j))],
        out_specs=pl.BlockSpec((r, tn), lambda j: (0, j)),
        out_shape=jax.ShapeDtypeStruct((r, n), F32),
        compiler_params=_params("parallel"),
        name="mod",
    )(c_rows, w_mod, b_mod.reshape(1, -1))


def _norm_mod_kernel(x_ref, g_ref, scale_ref, shift_ref, o_ref):
    x = x_ref[0]
    y = x * lax.rsqrt(jnp.mean(x * x, axis=-1, keepdims=True) + EPS) * g_ref[...]
    o_ref[0] = (y * (1.0 + scale_ref[0]) + shift_ref[0]).astype(o_ref.dtype)


def _norm_mod_call(x, gain, mod3, row_of_batch, shift_blk, scale_blk):
    b, t, d = x.shape
    tm = _tile(t, 512, SUBLANES)
    return pl.pallas_call(
        _norm_mod_kernel,
        grid=(b, t // tm),
        in_specs=[pl.BlockSpec((1, tm, d), lambda bi, i: (bi, i, 0)),
                  pl.BlockSpec((1, d), lambda bi, i: (0, 0)),
                  pl.BlockSpec((1, 1, d), lambda bi, i: (row_of_batch(bi), 0, scale_blk)),
                  pl.BlockSpec((1, 1, d), lambda bi, i: (row_of_batch(bi), 0, shift_blk))],
        out_specs=pl.BlockSpec((1, tm, d), lambda bi, i: (bi, i, 0)),
        out_shape=jax.ShapeDtypeStruct((b, t, d), BF16),
        compiler_params=_params("parallel", "parallel"),
        name="norm_mod",
    )(x, gain.reshape(1, d), mod3, mod3)


def _final_norm_kernel(x_ref, g_ref, o_ref):
    x = x_ref[0]
    o_ref[0] = x * lax.rsqrt(jnp.mean(x * x, axis=-1, keepdims=True) + EPS) * g_ref[...]


def _final_norm_call(x, gain):
    b, t, d = x.shape
    tm = _tile(t, 512, SUBLANES)
    return pl.pallas_call(
        _final_norm_kernel,
        grid=(b, t // tm),
        in_specs=[pl.BlockSpec((1, tm, d), lambda bi, i: (bi, i, 0)),
                  pl.BlockSpec((1, d), lambda bi, i: (0, 0))],
        out_specs=pl.BlockSpec((1, tm, d), lambda bi, i: (bi, i, 0)),
        out_shape=jax.ShapeDtypeStruct((b, t, d), F32),
        compiler_params=_params("parallel", "parallel"),
        name="final_norm",
    )(x, gain.reshape(1, d))


def _in_proj_tile(h_ref, w_refs, o_ref, wb_s):
    @pl.when((pl.program_id(1) == 0) & (pl.program_id(2) == 0))
    def _():
        rows = w_refs[0].shape[0]
        for q, w_ref in enumerate(w_refs):
            wb_s[q * rows:(q + 1) * rows, :] = w_ref[...].astype(BF16)

    acc = _dot(h_ref[0], wb_s[...])
    for j in range(o_ref.shape[1]):
        o_ref[0, j] = acc[:, j * LANES:(j + 1) * LANES]


def _in_proj_kernel(h_ref, *refs):
    w_refs, (o_ref, wb_s) = refs[:W_IN_SPLIT], refs[W_IN_SPLIT:]
    _in_proj_tile(h_ref, w_refs, o_ref, wb_s)


def _in_proj_cast_kernel(n_cast, h_ref, *refs):
    w_refs, (c_ref, o_ref, co_ref, wb_s) = refs[:W_IN_SPLIT], refs[W_IN_SPLIT:]
    _in_proj_tile(h_ref, w_refs, o_ref, wb_s)
    _cast_block(c_ref, co_ref, _grid_step(3), n_cast)


def _in_proj_call(h, w, n, tm_target, cast_src=None):
    b, t, d = h.shape
    tm = _tile(t, tm_target, SUBLANES)
    tn = _tile(n, 1024)
    nj, ni = n // tn, t // tm
    grid = (nj, b, ni)
    dk = d // W_IN_SPLIT
    period = b * ni

    def w_spec(q):
        switch = 1 + q * max(1, (period - 2) // W_IN_SPLIT)
        return pl.BlockSpec((dk, tn), lambda j, bi, i: (q, jnp.minimum(j + (bi * ni + i >= switch), nj - 1)))

    in_specs = [pl.BlockSpec((1, tm, d), lambda j, bi, i: (bi, i, 0))] + [w_spec(q) for q in range(W_IN_SPLIT)]
    out_specs = pl.BlockSpec((1, tn // LANES, tm, LANES), lambda j, bi, i: (bi, j, i, 0))
    out_shape = jax.ShapeDtypeStruct((b, n // LANES, t, LANES), F32)
    common = dict(grid=grid, scratch_shapes=[pltpu.VMEM((d, tn), BF16)],
                  compiler_params=_params("arbitrary", "arbitrary", "arbitrary"),
                  name="in_proj")
    ws = (w,) * W_IN_SPLIT
    if cast_src is None:
        return pl.pallas_call(_in_proj_kernel, in_specs=in_specs, out_specs=out_specs, out_shape=out_shape,
                              **common)(h, *ws)
    n_cast, c_in, c_out, c_shape = _cast_specs(cast_src, grid)
    return pl.pallas_call(functools.partial(_in_proj_cast_kernel, n_cast), in_specs=in_specs + [c_in],
                          out_specs=[out_specs, c_out], out_shape=[out_shape, c_shape],
                          **common)(h, *ws, cast_src)


def _roll_in_groups(x, shift):
    n, w = x.shape
    x3 = x.reshape(n // SUBLANES, SUBLANES, w)
    return pltpu.roll(x3, shift % SUBLANES, axis=1).reshape(n, w)


def _cumsum_rows(x, reverse):
    n = x.shape[0]
    row = lax.broadcasted_iota(jnp.int32, x.shape, 0)
    d = 1
    while d < n:
        if reverse:
            x = x + jnp.where(row < n - d, pltpu.roll(x, n - d, axis=0), 0.0)
        else:
            x = x + jnp.where(row >= d, pltpu.roll(x, d, axis=0), 0.0)
        d *= 2
    return x


def _block_row_bcast(x, block, offset):
    n, w = x.shape
    if block >= SUBLANES:
        xb = x.reshape(n // block, block, w)
        return jnp.broadcast_to(xb[:, offset:offset + 1, :], xb.shape).reshape(n, w)
    m = lax.broadcasted_iota(jnp.int32, x.shape, 0) & (block - 1)
    out = x
    for mm in range(block):
        if mm != offset:
            out = jnp.where(m == mm, _roll_in_groups(x, mm - offset), out)
    return out


def _hgrn_gates(z, lb):
    f = lb + (1.0 - lb) * jax.nn.sigmoid(z)
    return 1.0 - f, jnp.log2(f)


def _hgrn_level_halves(q, kf, kb, cf, cb, h):
    n = q.shape[0]
    qm, km = [], []
    for lo in range(0, n, 2 * h):
        mid, hi = lo + h, lo + 2 * h
        bf = jnp.broadcast_to(cf[mid - 1:mid, :], (h, LANES))
        bb = jnp.broadcast_to(cb[mid:mid + 1, :], (h, LANES))
        qm += [q[lo:mid] * jnp.exp2(cb[lo:mid] - bb), q[mid:hi] * jnp.exp2(cf[mid:hi] - bf)]
        km += [kf[lo:mid] * jnp.exp2(bf - cf[lo:mid]), kb[mid:hi] * jnp.exp2(bb - cb[mid:hi])]
    return jnp.concatenate(qm, axis=0), jnp.concatenate(km, axis=0)


def _state_update(s_ref, k, cum, v, reverse):
    n = cum.shape[0]
    edge = cum[0:1, :] if reverse else cum[n - 1:n, :]
    k_dec = (k * jnp.exp2(edge - cum)).astype(BF16)
    s_ref[...] = s_ref[...] * jnp.exp2(edge) + _dot_tn(v.astype(BF16), k_dec)


def _hgrn_kernel(n_cast, i_ref, ff_ref, fb_ref, q_ref, g_ref, ci_ref, cff_ref, cfb_ref, lbl_ref, ng_ref, c_ref,
                 y_ref, co_ref, kf_s, kb_s, cf_s, cb_s, st_s, sf_s, sb_s):
    _cast_block(c_ref, co_ref, _grid_step(2), n_cast)
    t_len, tc_len = i_ref.shape[2], ci_ref.shape[2]
    ch = HG_CHUNK
    cc = min(ch, tc_len)
    n, nc = t_len // ch, tc_len // cc

    l0, l1 = lbl_ref[0], lbl_ref[1]
    mx = jnp.maximum(l0, l1)
    e0, e1 = jnp.exp(l0 - mx), jnp.exp(l1 - mx)
    lb = e0 / (e0 + e1)
    lb_f, lb_b = lb[0:1], lb[1:2]

    sf_s[...] = jnp.zeros_like(sf_s)
    sb_s[...] = jnp.zeros_like(sb_s)

    def ctx_body(c, carry):
        rf = pl.multiple_of(c * cc, cc)
        rb = pl.multiple_of((nc - 1 - c) * cc, cc)
        k, lf = _hgrn_gates(cff_ref[0, 0, pl.ds(rf, cc), :], lb_f)
        _state_update(sf_s, k, _cumsum_rows(lf, False), ci_ref[0, 0, pl.ds(rf, cc), :], False)
        k, lf = _hgrn_gates(cfb_ref[0, 0, pl.ds(rb, cc), :], lb_b)
        _state_update(sb_s, k, _cumsum_rows(lf, True), ci_ref[0, 0, pl.ds(rb, cc), :], True)
        return carry

    lax.fori_loop(0, nc, ctx_body, 0)

    def state_body(c, carry):
        cb = n - 1 - c
        rf = pl.multiple_of(c * ch, ch)
        rb = pl.multiple_of(cb * ch, ch)
        k, lf = _hgrn_gates(ff_ref[0, 0, pl.ds(rf, ch), :], lb_f)
        cum = _cumsum_rows(lf, False)
        kf_s[pl.ds(rf, ch), :] = k
        cf_s[pl.ds(rf, ch), :] = cum
        st_s[c, :, 0:LANES] = sf_s[...].astype(BF16)
        _state_update(sf_s, k, cum, i_ref[0, 0, pl.ds(rf, ch), :], False)
        k, lf = _hgrn_gates(fb_ref[0, 0, pl.ds(rb, ch), :], lb_b)
        cum = _cumsum_rows(lf, True)
        kb_s[pl.ds(rb, ch), :] = k
        cb_s[pl.ds(rb, ch), :] = cum
        st_s[cb, :, LANES:2 * LANES] = sb_s[...].astype(BF16)
        _state_update(sb_s, k, cum, i_ref[0, 0, pl.ds(rb, ch), :], True)
        return carry

    lax.fori_loop(0, n, state_body, 0, unroll=HG_STATE_UNROLL)

    pair_xor = (lax.broadcasted_iota(jnp.int32, (ch, ch), 0) ^ lax.broadcasted_iota(jnp.int32, (ch, ch), 1))
    row = lax.broadcasted_iota(jnp.int32, (ch, LANES), 0)
    q_scale = LANES ** -0.5

    def out_body(c, carry):
        r = pl.multiple_of(c * ch, ch)
        q = _silu(q_ref[0, 0, pl.ds(r, ch), :]) * q_scale
        kf, kb = kf_s[pl.ds(r, ch), :], kb_s[pl.ds(r, ch), :]
        cf, cb = cf_s[pl.ds(r, ch), :], cb_s[pl.ds(r, ch), :]
        v = i_ref[0, 0, pl.ds(r, ch), :].astype(BF16)
        a = jnp.where(pair_xor == 0, _dot_nt(q.astype(BF16), (kf + kb).astype(BF16)), 0.0)
        odd = (row & 1) != 0
        qm = q * jnp.where(odd, 1.0 - kf, 1.0 - kb)
        a = jnp.where(pair_xor == 1, _dot_nt(qm.astype(BF16), jnp.where(odd, kb, kf).astype(BF16)), a)
        h, log_h = 2, 1
        while h < ch:
            if h >= SUBLANES:
                qm, km = _hgrn_level_halves(q, kf, kb, cf, cb, h)
            else:
                second = (row & h) != 0
                df = cf - _block_row_bcast(cf, 2 * h, h - 1)
                db = cb - _block_row_bcast(cb, 2 * h, h)
                qm = q * jnp.exp2(jnp.where(second, df, db))
                km = jnp.where(second, kb, kf) * jnp.exp2(-jnp.where(second, db, df))
            a = jnp.where((pair_xor >> log_h) == 1, _dot_nt(qm.astype(BF16), km.astype(BF16)), a)
            h, log_h = 2 * h, log_h + 1
        q_in = jnp.concatenate([q * jnp.exp2(cf), q * jnp.exp2(cb)], axis=1).astype(BF16)
        o = _dot(a.astype(BF16), v) + _dot_nt(q_in, st_s[c])
        y = o * lax.rsqrt(jnp.mean(o * o, axis=-1, keepdims=True) + EPS) * ng_ref[...]
        y_ref[0, pl.ds(r, ch), :] = (y * _silu(g_ref[0, 0, pl.ds(r, ch), :])).astype(y_ref.dtype)
        return carry

    lax.fori_loop(0, n, out_body, 0, unroll=HG_OUT_UNROLL)


def _hgrn_call(p_x, p_c, lb_logits, norm_g, heads, cast_src):
    b, _, t, _ = p_x.shape
    tc = p_c.shape[2]
    grid = (b, heads)
    n_cast, c_in, c_out, c_shape = _cast_specs(cast_src, grid)

    def col(off):
        return pl.BlockSpec((1, 1, t, LANES), lambda bi, h: (bi, off * heads + h, 0, 0))

    def ctx_col(off):
        return pl.BlockSpec((1, 1, tc, LANES), lambda bi, h: (bi, off * heads + h, 0, 0))

    return pl.pallas_call(
        functools.partial(_hgrn_kernel, n_cast),
        grid=grid,
        in_specs=[col(0), col(1), col(2), col(5), col(6), ctx_col(0), ctx_col(1), ctx_col(2),
                  pl.BlockSpec((2, 2, LANES), lambda bi, h: (0, 0, h)),
                  pl.BlockSpec((1, LANES), lambda bi, h: (0, 0)),
                  c_in],
        out_specs=[pl.BlockSpec((1, t, LANES), lambda bi, h: (bi, 0, h)), c_out],
        out_shape=[jax.ShapeDtypeStruct((b, t, heads * LANES), BF16), c_shape],
        scratch_shapes=[pltpu.VMEM((t, LANES), F32), pltpu.VMEM((t, LANES), F32),
                        pltpu.VMEM((t, LANES), F32), pltpu.VMEM((t, LANES), F32),
                        pltpu.VMEM((t // HG_CHUNK, LANES, 2 * LANES), BF16),
                        pltpu.VMEM((LANES, LANES), F32), pltpu.VMEM((LANES, LANES), F32)],
        compiler_params=_params("arbitrary", "arbitrary"),
        name="hgrn",
    )(p_x, p_x, p_x, p_x, p_x, p_c, p_c, p_c, lb_logits, norm_g.reshape(1, LANES), cast_src)


def _rope(a, cos, sin_signed, swap=None):
    if swap is None:
        lane = lax.broadcasted_iota(jnp.int32, a.shape, 1)
        partner = jnp.where((lane & 1) == 0, pltpu.roll(a, LANES - 1, axis=1), pltpu.roll(a, 1, axis=1))
    else:
        partner = _dot(a.astype(BF16), swap)
    return a * cos + partner * sin_signed


def _ret_state_update(s_ref, k, v, k_decay, s_decay):
    s_ref[...] = s_ref[...] * s_decay + _dot_tn(v.astype(BF16), (k * k_decay).astype(BF16))


def _ret_kernel(n_cast, q_ref, k_ref, v_ref, g_ref, ck_ref, cv_ref, dl_ref, cos_ref, sin_ref, sw_ref, c1_ref, c2_ref,
                mc_ref, mw_ref, mb_ref, y_ref, co1_ref, co2_ref, mo_ref, kr_s, st_s, sf_s, sb_s):
    _cast_block(c1_ref, co1_ref, _grid_step(2), n_cast)
    _cast_block(c2_ref, co2_ref, _grid_step(2), n_cast)
    _mod_kernel(mc_ref, mw_ref, mb_ref, mo_ref)
    t_len, tc_len = q_ref.shape[2], ck_ref.shape[2]
    ch = RET_CHUNK
    cc = min(ch, tc_len)
    n, nc = t_len // ch, tc_len // cc
    k_scale = LANES ** -0.5

    dl = dl_ref[0]
    gam = jnp.minimum(dl, 0.0) - jnp.log1p(jnp.exp(-jnp.abs(dl)))
    gam_f, gam_b = gam[0:1], gam[1:2]

    def decays(m):
        pos = lax.broadcasted_iota(jnp.int32, (m, LANES), 0).astype(F32)
        return dict(kf=jnp.exp(gam_f * (m - 1.0 - pos)), kb=jnp.exp(gam_b * pos),
                    qf=jnp.exp(gam_f * (pos + 1.0)), qb=jnp.exp(gam_b * (m - pos)),
                    sf=jnp.exp(gam_f * m), sb=jnp.exp(gam_b * m))

    sf_s[...] = jnp.zeros_like(sf_s)
    sb_s[...] = jnp.zeros_like(sb_s)

    dc = decays(cc)

    def ctx_body(c, carry):
        rf = pl.multiple_of(c * cc, cc)
        rb = pl.multiple_of((nc - 1 - c) * cc, cc)
        _ret_state_update(sf_s, ck_ref[0, 0, pl.ds(rf, cc), :] * k_scale, cv_ref[0, 0, pl.ds(rf, cc), :],
                          dc["kf"], dc["sf"])
        _ret_state_update(sb_s, ck_ref[0, 0, pl.ds(rb, cc), :] * k_scale, cv_ref[0, 0, pl.ds(rb, cc), :],
                          dc["kb"], dc["sb"])
        return carry

    lax.fori_loop(0, nc, ctx_body, 0)

    def rope_body(c, carry):
        r = pl.multiple_of(c * ch, ch)
        kr_s[pl.ds(r, ch), :] = _rope(k_ref[0, 0, pl.ds(r, ch), :] * k_scale,
                                      cos_ref[pl.ds(r, ch), :], sin_ref[pl.ds(r, ch), :], sw_ref[...])
        return carry

    lax.fori_loop(0, n, rope_body, 0, unroll=RET_UNROLL)

    dd = decays(ch)

    def state_body(c, carry):
        cb = n - 1 - c
        rf = pl.multiple_of(c * ch, ch)
        rb = pl.multiple_of(cb * ch, ch)
        st_s[c, :, 0:LANES] = sf_s[...].astype(BF16)
        _ret_state_update(sf_s, kr_s[pl.ds(rf, ch), :], v_ref[0, 0, pl.ds(rf, ch), :], dd["kf"], dd["sf"])
        st_s[cb, :, LANES:2 * LANES] = sb_s[...].astype(BF16)
        _ret_state_update(sb_s, kr_s[pl.ds(rb, ch), :], v_ref[0, 0, pl.ds(rb, ch), :], dd["kb"], dd["sb"])
        return carry

    lax.fori_loop(0, n, state_body, 0, unroll=RET_UNROLL)

    ti = lax.broadcasted_iota(jnp.int32, (ch, ch), 0)
    si = lax.broadcasted_iota(jnp.int32, (ch, ch), 1)
    lag = (ti - si).astype(F32)
    g_f = jnp.broadcast_to(gam_f[:, 0:1], (ch, ch))
    g_b = jnp.broadcast_to(gam_b[:, 0:1], (ch, ch))
    decay = jnp.where(ti > si, jnp.exp(g_f * jnp.maximum(lag, 0.0)),
                      jnp.where(ti < si, jnp.exp(g_b * jnp.maximum(-lag, 0.0)), 2.0))

    def out_body(c, carry):
        r = pl.multiple_of(c * ch, ch)
        q = _rope(q_ref[0, 0, pl.ds(r, ch), :], cos_ref[pl.ds(r, ch), :], sin_ref[pl.ds(r, ch), :])
        k = kr_s[pl.ds(r, ch), :].astype(BF16)
        v = v_ref[0, 0, pl.ds(r, ch), :].astype(BF16)
        a = _dot_nt(q.astype(BF16), k) * decay
        q_in = jnp.concatenate([q * dd["qf"], q * dd["qb"]], axis=1).astype(BF16)
        o = _dot(a.astype(BF16), v) + _dot_nt(q_in, st_s[c])
        oc = o - jnp.mean(o, axis=-1, keepdims=True)
        y = oc * lax.rsqrt(jnp.mean(oc * oc, axis=-1, keepdims=True) + EPS)
        y_ref[0, pl.ds(r, ch), :] = (y * _silu(g_ref[0, 0, pl.ds(r, ch), :])).astype(y_ref.dtype)
        return carry

    lax.fori_loop(0, n, out_body, 0, unroll=RET_UNROLL)


def _ret_call(p_x, p_c, decay_logit, cos, sin_signed, heads, cast_src1, cast_src2, c_rows, w_mod, b_mod, mod_col0):
    b, _, t, _ = p_x.shape
    tc = p_c.shape[2]
    grid = (b, heads)
    r, d = c_rows.shape
    mod_cols = w_mod.shape[1] - mod_col0
    mod_tn = mod_cols // (b * heads)
    assert mod_tn % LANES == 0 and mod_col0 % mod_tn == 0
    mod_blk0 = mod_col0 // mod_tn
    assert cast_src1.shape == cast_src2.shape
    n_cast, c_in, c_out, c_shape = _cast_specs(cast_src1, grid)
    swap = jnp.asarray(np.eye(LANES, dtype=np.float32)[:, np.arange(LANES) ^ 1], BF16)

    def col(off):
        return pl.BlockSpec((1, 1, t, LANES), lambda bi, h: (bi, off * heads + h, 0, 0))

    def ctx_col(off):
        return pl.BlockSpec((1, 1, tc, LANES), lambda bi, h: (bi, off * heads + h, 0, 0))

    return pl.pallas_call(
        functools.partial(_ret_kernel, n_cast),
        grid=grid,
        in_specs=[col(7), col(3), col(4), col(8), ctx_col(3), ctx_col(4),
                  pl.BlockSpec((1, 2, LANES), lambda bi, h: (h, 0, 0)),
                  pl.BlockSpec((t, LANES), lambda bi, h: (0, 0)),
                  pl.BlockSpec((t, LANES), lambda bi, h: (0, 0)),
                  pl.BlockSpec((LANES, LANES), lambda bi, h: (0, 0)),
                  c_in, c_in,
                  pl.BlockSpec((r, d), lambda bi, h: (0, 0)),
                  pl.BlockSpec((d, mod_tn), lambda bi, h: (0, mod_blk0 + bi * heads + h)),
                  pl.BlockSpec((1, mod_tn), lambda bi, h: (0, mod_blk0 + bi * heads + h))],
        out_specs=[pl.BlockSpec((1, t, LANES), lambda bi, h: (bi, 0, h)), c_out, c_out,
                   pl.BlockSpec((r, mod_tn), lambda bi, h: (0, bi * heads + h))],
        out_shape=[jax.ShapeDtypeStruct((b, t, heads * LANES), BF16), c_shape, c_shape,
                   jax.ShapeDtypeStruct((r, mod_cols), F32)],
        scratch_shapes=[pltpu.VMEM((t, LANES), F32),
                        pltpu.VMEM((t // RET_CHUNK, LANES, 2 * LANES), BF16),
                        pltpu.VMEM((LANES, LANES), F32), pltpu.VMEM((LANES, LANES), F32)],
        compiler_params=_params("arbitrary", "arbitrary"),
        name="ret",
    )(p_x, p_x, p_x, p_x, p_c, p_c, decay_logit, cos, sin_signed, swap, cast_src1, cast_src2,
      c_rows, w_mod, b_mod.reshape(1, -1))


def _merge_kernel(yh_ref, yr_ref, wh_ref, wr_ref, gh_ref, gr_ref, o_ref):
    acc_h = _dot(yh_ref[0], wh_ref[...])
    acc_r = _dot(yr_ref[0], wr_ref[...])
    for j in range(gh_ref.shape[1]):
        sl = slice(j * LANES, (j + 1) * LANES)
        o_ref[0, :, sl] = (jax.nn.sigmoid(gh_ref[0, j]) * acc_h[:, sl]
                           + jax.nn.sigmoid(gr_ref[0, j]) * acc_r[:, sl]).astype(o_ref.dtype)


def _merge_call(y_hg, y_ret, w_bh, w_br, p_x, gate_h_blk, gate_r_blk):
    b, t, kw = y_hg.shape
    d = w_bh.shape[1]
    tm = _tile(t, 1024, SUBLANES)
    tn = _tile(d, 512)
    nb = tn // LANES
    return pl.pallas_call(
        _merge_kernel,
        grid=(b, t // tm, d // tn),
        in_specs=[pl.BlockSpec((1, tm, kw), lambda bi, i, j: (bi, i, 0)),
                  pl.BlockSpec((1, tm, kw), lambda bi, i, j: (bi, i, 0)),
                  pl.BlockSpec((kw, tn), lambda bi, i, j: (0, j)),
                  pl.BlockSpec((kw, tn), lambda bi, i, j: (0, j)),
                  pl.BlockSpec((1, nb, tm, LANES), lambda bi, i, j: (bi, gate_h_blk // nb + j, i, 0)),
                  pl.BlockSpec((1, nb, tm, LANES), lambda bi, i, j: (bi, gate_r_blk // nb + j, i, 0))],
        out_specs=pl.BlockSpec((1, tm, tn), lambda bi, i, j: (bi, i, j)),
        out_shape=jax.ShapeDtypeStruct((b, t, d), BF16),
        compiler_params=_params("parallel", "parallel", "arbitrary"),
        name="merge",
    )(y_hg, y_ret, w_bh, w_br, p_x, p_x)


def _gated_residual_kernel(a_ref, w_ref, x_ref, gate_ref, o_ref):
    o_ref[0] = x_ref[0] + gate_ref[0] * _dot(a_ref[0], w_ref[...])


def _out_proj_call(a, w, x, mod3, gate_blk_of):
    b, t, kw = a.shape
    d = w.shape[1]
    tm = _tile(t, 1024, SUBLANES)
    tn = _tile(d, 1024)
    return pl.pallas_call(
        _gated_residual_kernel,
        grid=(b, t // tm, d // tn),
        in_specs=[pl.BlockSpec((1, tm, kw), lambda bi, i, j: (bi, i, 0)),
                  pl.BlockSpec((kw, tn), lambda bi, i, j: (0, j)),
                  pl.BlockSpec((1, tm, tn), lambda bi, i, j: (bi, i, j)),
                  pl.BlockSpec((1, 1, tn), lambda bi, i, j: (bi, 0, gate_blk_of(tn) + j))],
        out_specs=pl.BlockSpec((1, tm, tn), lambda bi, i, j: (bi, i, j)),
        out_shape=jax.ShapeDtypeStruct((b, t, d), F32),
        compiler_params=_params("parallel", "parallel", "arbitrary"),
        name="out_proj",
    )(a, w, x, mod3)


def _ff1_kernel(n_cast, h_ref, w_ref, c_ref, o_ref, co_ref):
    z = jnp.maximum(_dot(h_ref[0], w_ref[...]), 0.0)
    o_ref[0] = (z * z).astype(o_ref.dtype)
    _cast_block(c_ref, co_ref, _grid_step(3), n_cast)


def _ff1_call(h, w, cast_src):
    b, t, d = h.shape
    n = w.shape[1]
    tm = _tile(t, 1024, SUBLANES)
    tn = _tile(n, 1024)
    grid = (b, t // tm, n // tn)
    n_cast, c_in, c_out, c_shape = _cast_specs(cast_src, grid)
    return pl.pallas_call(
        functools.partial(_ff1_kernel, n_cast),
        grid=grid,
        in_specs=[pl.BlockSpec((1, tm, d), lambda bi, i, j: (bi, i, 0)),
                  pl.BlockSpec((d, tn), lambda bi, i, j: (0, j)),
                  c_in],
        out_specs=[pl.BlockSpec((1, tm, tn), lambda bi, i, j: (bi, i, j)), c_out],
        out_shape=[jax.ShapeDtypeStruct((b, t, n), BF16), c_shape],
        compiler_params=_params("arbitrary", "arbitrary", "arbitrary"),
        name="ff1",
    )(h, w, cast_src)


def _ff2_kernel(a_ref, w_ref, x_ref, gate_ref, o_ref, acc_ref):
    k = pl.program_id(3)

    @pl.when(k == 0)
    def _():
        acc_ref[...] = jnp.zeros_like(acc_ref)

    acc_ref[...] += _dot(a_ref[0], w_ref[...])

    @pl.when(k == pl.num_programs(3) - 1)
    def _():
        o_ref[0] = x_ref[0] + gate_ref[0] * acc_ref[...]


def _ff2_call(a, w, x, mod3, gate_blk_of):
    b, t, kw = a.shape
    d = w.shape[1]
    tm = _tile(t, 1024, SUBLANES)
    tn = _tile(d, 1024)
    tk = _tile(kw, 4096)
    return pl.pallas_call(
        _ff2_kernel,
        grid=(b, t // tm, d // tn, kw // tk),
        in_specs=[pl.BlockSpec((1, tm, tk), lambda bi, i, j, k: (bi, i, k)),
                  pl.BlockSpec((tk, tn), lambda bi, i, j, k: (k, j)),
                  pl.BlockSpec((1, tm, tn), lambda bi, i, j, k: (bi, i, j)),
                  pl.BlockSpec((1, 1, tn), lambda bi, i, j, k: (bi, 0, gate_blk_of(tn) + j))],
        out_specs=pl.BlockSpec((1, tm, tn), lambda bi, i, j, k: (bi, i, j)),
        out_shape=jax.ShapeDtypeStruct((b, t, d), F32),
        scratch_shapes=[pltpu.VMEM((tm, tn), F32)],
        compiler_params=_params("parallel", "parallel", "parallel", "arbitrary"),
        name="ff2",
    )(a, w, x, mod3)


def _rope_tables(t_len):
    pos = jnp.arange(t_len)
    row = (pos // GRID_W).astype(F32)
    col = (pos % GRID_W).astype(F32)
    n_freq = LANES // 4
    inv_freq = ROPE_BASE ** (-jnp.arange(n_freq, dtype=F32) / n_freq)
    ang = jnp.concatenate([row[:, None] * inv_freq, col[:, None] * inv_freq], axis=-1)
    cos, sin = jnp.cos(ang), jnp.sin(ang)
    cos_rep = jnp.repeat(cos, 2, axis=-1)
    sin_signed = jnp.stack([-sin, sin], axis=-1).reshape(t_len, LANES)
    return cos_rep, sin_signed


def kernel(x, c, ctx, c_ctx, w_mod, b_mod, norm1_g, norm2_g, w_in, hg_lb_logits, hg_norm_g, ret_decay_logit,
           w_branch_hgrn, w_branch_ret, w_out, w_ff1, w_ff2, final_norm_g):
    b, t, d = x.shape
    assert w_mod.shape[0] == 1 and hg_lb_logits.shape[0] == 2, "one layer"
    assert b + 1 <= SUBLANES
    hw = w_branch_hgrn.shape[1]
    heads = hw // LANES
    assert w_branch_ret.shape[1] == hw and w_in.shape[2] == 9 * hw + 2 * d
    n_state = 5 * hw

    c_rows = jnp.zeros((SUBLANES, d), F32).at[:b].set(c).at[b].set(c_ctx)
    mod_a = _mod_call(c_rows, w_mod[0], b_mod[0], 2 * d).reshape(SUBLANES, 1, 2 * d)

    h_x = _norm_mod_call(x, norm1_g[0], mod_a, lambda bi: bi, 0, 1)
    h_c = _norm_mod_call(ctx, norm1_g[0], mod_a, lambda bi: b, 0, 1)
    p_x, w_ff1_b = _in_proj_call(h_x, w_in[0], w_in.shape[2], 512, cast_src=w_ff1[0])
    p_c = _in_proj_call(h_c, w_in[0], n_state, 256)

    y_hg, w_out_b = _hgrn_call(p_x, p_c, hg_lb_logits, hg_norm_g[0], heads, w_out[0])
    cos_rep, sin_signed = _rope_tables(t)
    decay_logit = jnp.broadcast_to(ret_decay_logit[0].T[:, :, None], (heads, 2, LANES))
    y_ret, w_bh_b, w_br_b, mod_b = _ret_call(p_x, p_c, decay_logit, cos_rep, sin_signed, heads,
                                             w_branch_hgrn[0], w_branch_ret[0], c_rows, w_mod[0], b_mod[0], 2 * d)
    mod_b = mod_b.reshape(SUBLANES, 1, 4 * d)

    merged = _merge_call(y_hg, y_ret, w_bh_b, w_br_b, p_x, 9 * heads, 9 * heads + d // LANES)
    x1 = _out_proj_call(merged, w_out_b, x, mod_b, lambda tn: 0)
    h2 = _norm_mod_call(x1, norm2_g[0], mod_b, lambda bi: bi, 1, 2)
    act, w_ff2_b = _ff1_call(h2, w_ff1_b, w_ff2[0])
    x2 = _ff2_call(act, w_ff2_b, x1, mod_b, lambda tn: 3 * d // tn)
    return _final_norm_call(x2, final_norm_g)
```

```python
import functools

import jax
import jax.numpy as jnp
import numpy as np
from jax import lax
from jax.experimental import pallas as pl
from jax.experimental.pallas import tpu as pltpu

F32 = jnp.float32
BF16 = jnp.bfloat16

LANES = 128
SUBLANES = 8
BF16_ROWS = 16
VMEM_LIMIT = 63 * 2 ** 20
EPS = 1e-6
ROPE_BASE = 10000.0
GRID_W = 64
HG_CHUNK = 128
RET_CHUNK = 256
HG_STATE_UNROLL = 8
HG_OUT_UNROLL = 8
RET_UNROLL = 4
W_IN_SPLIT = 4


def _params(*sem):
    return pltpu.CompilerParams(dimension_semantics=sem, vmem_limit_bytes=VMEM_LIMIT)


def _tile(n, target, unit=LANES):
    t = min(n, target) // unit * unit
    while n % t:
        t -= unit
    return t


def _silu(z):
    return z * jax.nn.sigmoid(z)


def _dot(a, b):
    return jnp.dot(a, b, preferred_element_type=F32)


def _dot_nt(a, b):
    return lax.dot_general(a, b, (((1,), (1,)), ((), ())), preferred_element_type=F32)


def _dot_tn(a, b):
    return lax.dot_general(a, b, (((0,), (0,)), ((), ())), preferred_element_type=F32)


def _grid_step(rank):
    step = pl.program_id(0)
    for axis in range(1, rank):
        step = step * pl.num_programs(axis) + pl.program_id(axis)
    return step


def _linear_step(ids, sizes):
    step = ids[0]
    for i, n in zip(ids[1:], sizes[1:]):
        step = step * n + i
    return step


def _cast_specs(src, grid):
    rows, cols = src.shape
    steps = int(np.prod(grid))
    n_blocks = 1
    while n_blocks * 2 <= steps and rows % (n_blocks * 2) == 0 and rows // (n_blocks * 2) >= BF16_ROWS:
        n_blocks *= 2

    def idx(*ids):
        return (jnp.minimum(_linear_step(ids, grid), n_blocks - 1), 0)

    spec = pl.BlockSpec((rows // n_blocks, cols), idx)
    return n_blocks, spec, spec, jax.ShapeDtypeStruct(src.shape, BF16)


def _cast_block(src_ref, dst_ref, step, n_blocks):
    @pl.when(step < n_blocks)
    def _():
        dst_ref[...] = src_ref[...].astype(dst_ref.dtype)


def _mod_kernel(c_ref, w_ref, b_ref, o_ref):
    s = _silu(c_ref[...]).astype(BF16)
    o_ref[...] = _dot(s, w_ref[...].astype(BF16)) + b_ref[...]


def _mod_call(c_rows, w_mod, b_mod, n):
    r, d = c_rows.shape
    tn = _tile(n, 512)
    return pl.pallas_call(
        _mod_kernel,
        grid=(n // tn,),
        in_specs=[pl.BlockSpec((r, d), lambda j: (0, 0)),
                  pl.BlockSpec((d, tn), lambda j: (0, j)),
                  pl.BlockSpec((1, tn), lambda j: (0, j))],
        out_specs=pl.BlockSpec((r, tn), lambda j: (0, j)),
        out_shape=jax.ShapeDtypeStruct((r, n), F32),
        compiler_params=_params("parallel"),
        name="mod",
    )(c_rows, w_mod, b_mod.reshape(1, -1))


def _norm_mod_kernel(x_ref, g_ref, scale_ref, shift_ref, o_ref):
    x = x_ref[0]
    y = x * lax.rsqrt(jnp.mean(x * x, axis=-1, keepdims=True) + EPS) * g_ref[...]
    o_ref[0] = (y * (1.0 + scale_ref[0]) + shift_ref[0]).astype(o_ref.dtype)


def _norm_mod_call(x, gain, mod3, row_of_batch, shift_blk, scale_blk):
    b, t, d = x.shape
    tm = _tile(t, 512, SUBLANES)
    return pl.pallas_call(
        _norm_mod_kernel,
        grid=(b, t // tm),
        in_specs=[pl.BlockSpec((1, tm, d), lambda bi, i: (bi, i, 0)),
                  pl.BlockSpec((1, d), lambda bi, i: (0, 0)),
                  pl.BlockSpec((1, 1, d), lambda bi, i: (row_of_batch(bi), 0, scale_blk)),
                  pl.BlockSpec((1, 1, d), lambda bi, i: (row_of_batch(bi), 0, shift_blk))],
        out_specs=pl.BlockSpec((1, tm, d), lambda bi, i: (bi, i, 0)),
        out_shape=jax.ShapeDtypeStruct((b, t, d), BF16),
        compiler_params=_params("parallel", "parallel"),
        name="norm_mod",
    )(x, gain.reshape(1, d), mod3, mod3)


def _final_norm_kernel(x_ref, g_ref, o_ref):
    x = x_ref[0]
    o_ref[0] = x * lax.rsqrt(jnp.mean(x * x, axis=-1, keepdims=True) + EPS) * g_ref[...]


def _final_norm_call(x, gain):
    b, t, d = x.shape
    tm = _tile(t, 512, SUBLANES)
    return pl.pallas_call(
        _final_norm_kernel,
        grid=(b, t // tm),
        in_specs=[pl.BlockSpec((1, tm, d), lambda bi, i: (bi, i, 0)),
                  pl.BlockSpec((1, d), lambda bi, i: (0, 0))],
        out_specs=pl.BlockSpec((1, tm, d), lambda bi, i: (bi, i, 0)),
        out_shape=jax.ShapeDtypeStruct((b, t, d), F32),
        compiler_params=_params("parallel", "parallel"),
        name="final_norm",
    )(x, gain.reshape(1, d))


def _in_proj_tile(h_ref, w_hbm, o_ref, stage_s, wb_s, sem):
    j = pl.program_id(0)
    nj = pl.num_programs(0)
    period = pl.num_programs(1) * pl.num_programs(2)
    p = pl.program_id(1) * pl.num_programs(2) + pl.program_id(2)
    dk = stage_s.shape[0] // W_IN_SPLIT
    tn = stage_s.shape[1]

    def slice_copy(tile, q):
        col = pl.multiple_of(tile * tn, tn)
        return pltpu.make_async_copy(w_hbm.at[pl.ds(q * dk, dk), pl.ds(col, tn)],
                                     stage_s.at[pl.ds(q * dk, dk), :], sem.at[q])

    @pl.when(p == 0)
    def _():
        @pl.when(j == 0)
        def _():
            for q in range(W_IN_SPLIT):
                slice_copy(0, q).start()

        for q in range(W_IN_SPLIT):
            slice_copy(j, q).wait()
            wb_s[q * dk:(q + 1) * dk, :] = stage_s[q * dk:(q + 1) * dk, :].astype(BF16)

    for q in range(W_IN_SPLIT):
        @pl.when((p == 1 + (q * (period - 1)) // W_IN_SPLIT) & (j + 1 < nj))
        def _():
            slice_copy(j + 1, q).start()

    acc = _dot(h_ref[0], wb_s[...])
    for jj in range(o_ref.shape[1]):
        o_ref[0, jj] = acc[:, jj * LANES:(jj + 1) * LANES]


def _in_proj_kernel(h_ref, w_hbm, o_ref, stage_s, wb_s, sem):
    _in_proj_tile(h_ref, w_hbm, o_ref, stage_s, wb_s, sem)


def _in_proj_cast_kernel(n_cast, h_ref, w_hbm, c_ref, o_ref, co_ref, stage_s, wb_s, sem):
    _in_proj_tile(h_ref, w_hbm, o_ref, stage_s, wb_s, sem)
    _cast_block(c_ref, co_ref, _grid_step(3), n_cast)


def _in_proj_call(h, w, n, tm_target, cast_src=None):
    b, t, d = h.shape
    tm = _tile(t, tm_target, SUBLANES)
    tn = _tile(n, 1024)
    grid = (n // tn, b, t // tm)
    assert b * (t // tm) >= 2 and d % W_IN_SPLIT == 0
    in_specs = [pl.BlockSpec((1, tm, d), lambda j, bi, i: (bi, i, 0)),
                pl.BlockSpec(memory_space=pl.ANY)]
    out_specs = pl.BlockSpec((1, tn // LANES, tm, LANES), lambda j, bi, i: (bi, j, i, 0))
    out_shape = jax.ShapeDtypeStruct((b, n // LANES, t, LANES), F32)
    common = dict(grid=grid,
                  scratch_shapes=[pltpu.VMEM((d, tn), F32), pltpu.VMEM((d, tn), BF16),
                                  pltpu.SemaphoreType.DMA((W_IN_SPLIT,))],
                  compiler_params=_params("arbitrary", "arbitrary", "arbitrary"),
                  name="in_proj")
    if cast_src is None:
        return pl.pallas_call(_in_proj_kernel, in_specs=in_specs, out_specs=out_specs, out_shape=out_shape,
                              **common)(h, w)
    n_cast, c_in, c_out, c_shape = _cast_specs(cast_src, grid)
    return pl.pallas_call(functools.partial(_in_proj_cast_kernel, n_cast), in_specs=in_specs + [c_in],
                          out_specs=[out_specs, c_out], out_shape=[out_shape, c_shape],
                          **common)(h, w, cast_src)


def _roll_in_groups(x, shift):
    n, w = x.shape
    x3 = x.reshape(n // SUBLANES, SUBLANES, w)
    return pltpu.roll(x3, shift % SUBLANES, axis=1).reshape(n, w)


def _cumsum_rows(x, reverse):
    n = x.shape[0]
    row = lax.broadcasted_iota(jnp.int32, x.shape, 0)
    d = 1
    while d < n:
        if reverse:
            x = x + jnp.where(row < n - d, pltpu.roll(x, n - d, axis=0), 0.0)
        else:
            x = x + jnp.where(row >= d, pltpu.roll(x, d, axis=0), 0.0)
        d *= 2
    return x


def _block_row_bcast(x, block, offset):
    n, w = x.shape
    if block >= SUBLANES:
        xb = x.reshape(n // block, block, w)
        return jnp.broadcast_to(xb[:, offset:offset + 1, :], xb.shape).reshape(n, w)
    m = lax.broadcasted_iota(jnp.int32, x.shape, 0) & (block - 1)
    out = x
    for mm in range(block):
        if mm != offset:
            out = jnp.where(m == mm, _roll_in_groups(x, mm - offset), out)
    return out


def _hgrn_gates(z, lb):
    f = lb + (1.0 - lb) * jax.nn.sigmoid(z)
    return 1.0 - f, jnp.log2(f)


def _hgrn_level_halves(q, kf, kb, cf, cb, h):
    n = q.shape[0]
    qm, km = [], []
    for lo in range(0, n, 2 * h):
        mid, hi = lo + h, lo + 2 * h
        bf = jnp.broadcast_to(cf[mid - 1:mid, :], (h, LANES))
        bb = jnp.broadcast_to(cb[mid:mid + 1, :], (h, LANES))
        qm += [q[lo:mid] * jnp.exp2(cb[lo:mid] - bb), q[mid:hi] * jnp.exp2(cf[mid:hi] - bf)]
        km += [kf[lo:mid] * jnp.exp2(bf - cf[lo:mid]), kb[mid:hi] * jnp.exp2(bb - cb[mid:hi])]
    return jnp.concatenate(qm, axis=0), jnp.concatenate(km, axis=0)


def _state_update(s_ref, k, cum, v, reverse):
    n = cum.shape[0]
    edge = cum[0:1, :] if reverse else cum[n - 1:n, :]
    k_dec = (k * jnp.exp2(edge - cum)).astype(BF16)
    s_ref[...] = s_ref[...] * jnp.exp2(edge) + _dot_tn(v.astype(BF16), k_dec)


def _hgrn_kernel(n_cast, i_ref, ff_ref, fb_ref, q_ref, g_ref, ci_ref, cff_ref, cfb_ref, lbl_ref, ng_ref, c_ref,
                 y_ref, co_ref, kf_s, kb_s, cf_s, cb_s, st_s, sf_s, sb_s):
    _cast_block(c_ref, co_ref, _grid_step(2), n_cast)
    t_len, tc_len = i_ref.shape[2], ci_ref.shape[2]
    ch = HG_CHUNK
    cc = min(ch, tc_len)
    n, nc = t_len // ch, tc_len // cc

    l0, l1 = lbl_ref[0], lbl_ref[1]
    mx = jnp.maximum(l0, l1)
    e0, e1 = jnp.exp(l0 - mx), jnp.exp(l1 - mx)
    lb = e0 / (e0 + e1)
    lb_f, lb_b = lb[0:1], lb[1:2]

    sf_s[...] = jnp.zeros_like(sf_s)
    sb_s[...] = jnp.zeros_like(sb_s)

    def ctx_body(c, carry):
        rf = pl.multiple_of(c * cc, cc)
        rb = pl.multiple_of((nc - 1 - c) * cc, cc)
        k, lf = _hgrn_gates(cff_ref[0, 0, pl.ds(rf, cc), :], lb_f)
        _state_update(sf_s, k, _cumsum_rows(lf, False), ci_ref[0, 0, pl.ds(rf, cc), :], False)
        k, lf = _hgrn_gates(cfb_ref[0, 0, pl.ds(rb, cc), :], lb_b)
        _state_update(sb_s, k, _cumsum_rows(lf, True), ci_ref[0, 0, pl.ds(rb, cc), :], True)
        return carry

    lax.fori_loop(0, nc, ctx_body, 0)

    def state_body(c, carry):
        cb = n - 1 - c
        rf = pl.multiple_of(c * ch, ch)
        rb = pl.multiple_of(cb * ch, ch)
        k, lf = _hgrn_gates(ff_ref[0, 0, pl.ds(rf, ch), :], lb_f)
        cum = _cumsum_rows(lf, False)
        kf_s[pl.ds(rf, ch), :] = k
        cf_s[pl.ds(rf, ch), :] = cum
        st_s[c, :, 0:LANES] = sf_s[...].astype(BF16)
        _state_update(sf_s, k, cum, i_ref[0, 0, pl.ds(rf, ch), :], False)
        k, lf = _hgrn_gates(fb_ref[0, 0, pl.ds(rb, ch), :], lb_b)
        cum = _cumsum_rows(lf, True)
        kb_s[pl.ds(rb, ch), :] = k
        cb_s[pl.ds(rb, ch), :] = cum
        st_s[cb, :, LANES:2 * LANES] = sb_s[...].astype(BF16)
        _state_update(sb_s, k, cum, i_ref[0, 0, pl.ds(rb, ch), :], True)
        return carry

    lax.fori_loop(0, n, state_body, 0, unroll=HG_STATE_UNROLL)

    pair_xor = (lax.broadcasted_iota(jnp.int32, (ch, ch), 0) ^ lax.broadcasted_iota(jnp.int32, (ch, ch), 1))
    row = lax.broadcasted_iota(jnp.int32, (ch, LANES), 0)
    q_scale = LANES ** -0.5

    def out_body(c, carry):
        r = pl.multiple_of(c * ch, ch)
        q = _silu(q_ref[0, 0, pl.ds(r, ch), :]) * q_scale
        kf, kb = kf_s[pl.ds(r, ch), :], kb_s[pl.ds(r, ch), :]
        cf, cb = cf_s[pl.ds(r, ch), :], cb_s[pl.ds(r, ch), :]
        v = i_ref[0, 0, pl.ds(r, ch), :].astype(BF16)
        a = jnp.where(pair_xor == 0, _dot_nt(q.astype(BF16), (kf + kb).astype(BF16)), 0.0)
        odd = (row & 1) != 0
        qm = q * jnp.where(odd, 1.0 - kf, 1.0 - kb)
        a = jnp.where(pair_xor == 1, _dot_nt(qm.astype(BF16), jnp.where(odd, kb, kf).astype(BF16)), a)
        h, log_h = 2, 1
        while h < ch:
            if h >= SUBLANES:
                qm, km = _hgrn_level_halves(q, kf, kb, cf, cb, h)
            else:
                second = (row & h) != 0
                df = cf - _block_row_bcast(cf, 2 * h, h - 1)
                db = cb - _block_row_bcast(cb, 2 * h, h)
                qm = q * jnp.exp2(jnp.where(second, df, db))
                km = jnp.where(second, kb, kf) * jnp.exp2(-jnp.where(second, db, df))
            a = jnp.where((pair_xor >> log_h) == 1, _dot_nt(qm.astype(BF16), km.astype(BF16)), a)
            h, log_h = 2 * h, log_h + 1
        q_in = jnp.concatenate([q * jnp.exp2(cf), q * jnp.exp2(cb)], axis=1).astype(BF16)
        o = _dot(a.astype(BF16), v) + _dot_nt(q_in, st_s[c])
        y = o * lax.rsqrt(jnp.mean(o * o, axis=-1, keepdims=True) + EPS) * ng_ref[...]
        y_ref[0, pl.ds(r, ch), :] = (y * _silu(g_ref[0, 0, pl.ds(r, ch), :])).astype(y_ref.dtype)
        return carry

    lax.fori_loop(0, n, out_body, 0, unroll=HG_OUT_UNROLL)


def _hgrn_call(p_x, p_c, lb_logits, norm_g, heads, cast_src):
    b, _, t, _ = p_x.shape
    tc = p_c.shape[2]
    grid = (b, heads)
    n_cast, c_in, c_out, c_shape = _cast_specs(cast_src, grid)

    def col(off):
        return pl.BlockSpec((1, 1, t, LANES), lambda bi, h: (bi, off * heads + h, 0, 0))

    def ctx_col(off):
        return pl.BlockSpec((1, 1, tc, LANES), lambda bi, h: (bi, off * heads + h, 0, 0))

    return pl.pallas_call(
        functools.partial(_hgrn_kernel, n_cast),
        grid=grid,
        in_specs=[col(0), col(1), col(2), col(5), col(6), ctx_col(0), ctx_col(1), ctx_col(2),
                  pl.BlockSpec((2, 2, LANES), lambda bi, h: (0, 0, h)),
                  pl.BlockSpec((1, LANES), lambda bi, h: (0, 0)),
                  c_in],
        out_specs=[pl.BlockSpec((1, t, LANES), lambda bi, h: (bi, 0, h)), c_out],
        out_shape=[jax.ShapeDtypeStruct((b, t, heads * LANES), BF16), c_shape],
        scratch_shapes=[pltpu.VMEM((t, LANES), F32), pltpu.VMEM((t, LANES), F32),
                        pltpu.VMEM((t, LANES), F32), pltpu.VMEM((t, LANES), F32),
                        pltpu.VMEM((t // HG_CHUNK, LANES, 2 * LANES), BF16),
                        pltpu.VMEM((LANES, LANES), F32), pltpu.VMEM((LANES, LANES), F32)],
        compiler_params=_params("arbitrary", "arbitrary"),
        name="hgrn",
    )(p_x, p_x, p_x, p_x, p_x, p_c, p_c, p_c, lb_logits, norm_g.reshape(1, LANES), cast_src)


def _rope(a, cos, sin_signed, swap=None):
    if swap is None:
        lane = lax.broadcasted_iota(jnp.int32, a.shape, 1)
        partner = jnp.where((lane & 1) == 0, pltpu.roll(a, LANES - 1, axis=1), pltpu.roll(a, 1, axis=1))
    else:
        partner = _dot(a.astype(BF16), swap)
    return a * cos + partner * sin_signed


def _ret_state_update(s_ref, k, v, k_decay, s_decay):
    s_ref[...] = s_ref[...] * s_decay + _dot_tn(v.astype(BF16), (k * k_decay).astype(BF16))


def _ret_kernel(n_cast, q_ref, k_ref, v_ref, g_ref, ck_ref, cv_ref, dl_ref, cos_ref, sin_ref, sw_ref, c1_ref, c2_ref,
                mc_ref, mw_ref, mb_ref, y_ref, co1_ref, co2_ref, mo_ref, kr_s, st_s, sf_s, sb_s):
    _cast_block(c1_ref, co1_ref, _grid_step(2), n_cast)
    _cast_block(c2_ref, co2_ref, _grid_step(2), n_cast)
    _mod_kernel(mc_ref, mw_ref, mb_ref, mo_ref)
    t_len, tc_len = q_ref.shape[2], ck_ref.shape[2]
    ch = RET_CHUNK
    cc = min(ch, tc_len)
    n, nc = t_len // ch, tc_len // cc
    k_scale = LANES ** -0.5

    dl = dl_ref[0]
    gam = jnp.minimum(dl, 0.0) - jnp.log1p(jnp.exp(-jnp.abs(dl)))
    gam_f, gam_b = gam[0:1], gam[1:2]

    def decays(m):
        pos = lax.broadcasted_iota(jnp.int32, (m, LANES), 0).astype(F32)
        return dict(kf=jnp.exp(gam_f * (m - 1.0 - pos)), kb=jnp.exp(gam_b * pos),
                    qf=jnp.exp(gam_f * (pos + 1.0)), qb=jnp.exp(gam_b * (m - pos)),
                    sf=jnp.exp(gam_f * m), sb=jnp.exp(gam_b * m))

    sf_s[...] = jnp.zeros_like(sf_s)
    sb_s[...] = jnp.zeros_like(sb_s)

    dc = decays(cc)

    def ctx_body(c, carry):
        rf = pl.multiple_of(c * cc, cc)
        rb = pl.multiple_of((nc - 1 - c) * cc, cc)
        _ret_state_update(sf_s, ck_ref[0, 0, pl.ds(rf, cc), :] * k_scale, cv_ref[0, 0, pl.ds(rf, cc), :],
                          dc["kf"], dc["sf"])
        _ret_state_update(sb_s, ck_ref[0, 0, pl.ds(rb, cc), :] * k_scale, cv_ref[0, 0, pl.ds(rb, cc), :],
                          dc["kb"], dc["sb"])
        return carry

    lax.fori_loop(0, nc, ctx_body, 0)

    def rope_body(c, carry):
        r = pl.multiple_of(c * ch, ch)
        kr_s[pl.ds(r, ch), :] = _rope(k_ref[0, 0, pl.ds(r, ch), :] * k_scale,
                                      cos_ref[pl.ds(r, ch), :], sin_ref[pl.ds(r, ch), :], sw_ref[...])
        return carry

    lax.fori_loop(0, n, rope_body, 0, unroll=RET_UNROLL)

    dd = decays(ch)

    def state_body(c, carry):
        cb = n - 1 - c
        rf = pl.multiple_of(c * ch, ch)
        rb = pl.multiple_of(cb * ch, ch)
        st_s[c, :, 0:LANES] = sf_s[...].astype(BF16)
        _ret_state_update(sf_s, kr_s[pl.ds(rf, ch), :], v_ref[0, 0, pl.ds(rf, ch), :], dd["kf"], dd["sf"])
        st_s[cb, :, LANES:2 * LANES] = sb_s[...].astype(BF16)
        _ret_state_update(sb_s, kr_s[pl.ds(rb, ch), :], v_ref[0, 0, pl.ds(rb, ch), :], dd["kb"], dd["sb"])
        return carry

    lax.fori_loop(0, n, state_body, 0, unroll=RET_UNROLL)

    ti = lax.broadcasted_iota(jnp.int32, (ch, ch), 0)
    si = lax.broadcasted_iota(jnp.int32, (ch, ch), 1)
    lag = (ti - si).astype(F32)
    g_f = jnp.broadcast_to(gam_f[:, 0:1], (ch, ch))
    g_b = jnp.broadcast_to(gam_b[:, 0:1], (ch, ch))
    decay = jnp.where(ti > si, jnp.exp(g_f * jnp.maximum(lag, 0.0)),
                      jnp.where(ti < si, jnp.exp(g_b * jnp.maximum(-lag, 0.0)), 2.0))

    def out_body(c, carry):
        r = pl.multiple_of(c * ch, ch)
        q = _rope(q_ref[0, 0, pl.ds(r, ch), :], cos_ref[pl.ds(r, ch), :], sin_ref[pl.ds(r, ch), :])
        k = kr_s[pl.ds(r, ch), :].astype(BF16)
        v = v_ref[0, 0, pl.ds(r, ch), :].astype(BF16)
        a = _dot_nt(q.astype(BF16), k) * decay
        q_in = jnp.concatenate([q * dd["qf"], q * dd["qb"]], axis=1).astype(BF16)
        o = _dot(a.astype(BF16), v) + _dot_nt(q_in, st_s[c])
        oc = o - jnp.mean(o, axis=-1, keepdims=True)
        y = oc * lax.rsqrt(jnp.mean(oc * oc, axis=-1, keepdims=True) + EPS)
        y_ref[0, pl.ds(r, ch), :] = (y * _silu(g_ref[0, 0, pl.ds(r, ch), :])).astype(y_ref.dtype)
        return carry

    lax.fori_loop(0, n, out_body, 0, unroll=RET_UNROLL)


def _ret_call(p_x, p_c, decay_logit, cos, sin_signed, heads, cast_src1, cast_src2, c_rows, w_mod, b_mod, mod_col0):
    b, _, t, _ = p_x.shape
    tc = p_c.shape[2]
    grid = (b, heads)
    r, d = c_rows.shape
    mod_cols = w_mod.shape[1] - mod_col0
    mod_tn = mod_cols // (b * heads)
    assert mod_tn % LANES == 0 and mod_col0 % mod_tn == 0
    mod_blk0 = mod_col0 // mod_tn
    assert cast_src1.shape == cast_src2.shape
    n_cast, c_in, c_out, c_shape = _cast_specs(cast_src1, grid)
    swap = jnp.asarray(np.eye(LANES, dtype=np.float32)[:, np.arange(LANES) ^ 1], BF16)

    def col(off):
        return pl.BlockSpec((1, 1, t, LANES), lambda bi, h: (bi, off * heads + h, 0, 0))

    def ctx_col(off):
        return pl.BlockSpec((1, 1, tc, LANES), lambda bi, h: (bi, off * heads + h, 0, 0))

    return pl.pallas_call(
        functools.partial(_ret_kernel, n_cast),
        grid=grid,
        in_specs=[col(7), col(3), col(4), col(8), ctx_col(3), ctx_col(4),
                  pl.BlockSpec((1, 2, LANES), lambda bi, h: (h, 0, 0)),
                  pl.BlockSpec((t, LANES), lambda bi, h: (0, 0)),
                  pl.BlockSpec((t, LANES), lambda bi, h: (0, 0)),
                  pl.BlockSpec((LANES, LANES), lambda bi, h: (0, 0)),
                  c_in, c_in,
                  pl.BlockSpec((r, d), lambda bi, h: (0, 0)),
                  pl.BlockSpec((d, mod_tn), lambda bi, h: (0, mod_blk0 + bi * heads + h)),
                  pl.BlockSpec((1, mod_tn), lambda bi, h: (0, mod_blk0 + bi * heads + h))],
        out_specs=[pl.BlockSpec((1, t, LANES), lambda bi, h: (bi, 0, h)), c_out, c_out,
                   pl.BlockSpec((r, mod_tn), lambda bi, h: (0, bi * heads + h))],
        out_shape=[jax.ShapeDtypeStruct((b, t, heads * LANES), BF16), c_shape, c_shape,
                   jax.ShapeDtypeStruct((r, mod_cols), F32)],
        scratch_shapes=[pltpu.VMEM((t, LANES), F32),
                        pltpu.VMEM((t // RET_CHUNK, LANES, 2 * LANES), BF16),
                        pltpu.VMEM((LANES, LANES), F32), pltpu.VMEM((LANES, LANES), F32)],
        compiler_params=_params("arbitrary", "arbitrary"),
        name="ret",
    )(p_x, p_x, p_x, p_x, p_c, p_c, decay_logit, cos, sin_signed, swap, cast_src1, cast_src2,
      c_rows, w_mod, b_mod.reshape(1, -1))


def _merge_kernel(yh_ref, yr_ref, wh_ref, wr_ref, gh_ref, gr_ref, o_ref):
    acc_h = _dot(yh_ref[0], wh_ref[...])
    acc_r = _dot(yr_ref[0], wr_ref[...])
    for j in range(gh_ref.shape[1]):
        sl = slice(j * LANES, (j + 1) * LANES)
        o_ref[0, :, sl] = (jax.nn.sigmoid(gh_ref[0, j]) * acc_h[:, sl]
                           + jax.nn.sigmoid(gr_ref[0, j]) * acc_r[:, sl]).astype(o_ref.dtype)


def _merge_call(y_hg, y_ret, w_bh, w_br, p_x, gate_h_blk, gate_r_blk):
    b, t, kw = y_hg.shape
    d = w_bh.shape[1]
    tm = _tile(t, 1024, SUBLANES)
    tn = _tile(d, 512)
    nb = tn // LANES
    return pl.pallas_call(
        _merge_kernel,
        grid=(b, t // tm, d // tn),
        in_specs=[pl.BlockSpec((1, tm, kw), lambda bi, i, j: (bi, i, 0)),
                  pl.BlockSpec((1, tm, kw), lambda bi, i, j: (bi, i, 0)),
                  pl.BlockSpec((kw, tn), lambda bi, i, j: (0, j)),
                  pl.BlockSpec((kw, tn), lambda bi, i, j: (0, j)),
                  pl.BlockSpec((1, nb, tm, LANES), lambda bi, i, j: (bi, gate_h_blk // nb + j, i, 0)),
                  pl.BlockSpec((1, nb, tm, LANES), lambda bi, i, j: (bi, gate_r_blk // nb + j, i, 0))],
        out_specs=pl.BlockSpec((1, tm, tn), lambda bi, i, j: (bi, i, j)),
        out_shape=jax.ShapeDtypeStruct((b, t, d), BF16),
        compiler_params=_params("parallel", "parallel", "arbitrary"),
        name="merge",
    )(y_hg, y_ret, w_bh, w_br, p_x, p_x)


def _gated_residual_kernel(a_ref, w_ref, x_ref, gate_ref, o_ref):
    o_ref[0] = x_ref[0] + gate_ref[0] * _dot(a_ref[0], w_ref[...])


def _out_proj_call(a, w, x, mod3, gate_blk_of):
    b, t, kw = a.shape
    d = w.shape[1]
    tm = _tile(t, 1024, SUBLANES)
    tn = _tile(d, 1024)
    return pl.pallas_call(
        _gated_residual_kernel,
        grid=(b, t // tm, d // tn),
        in_specs=[pl.BlockSpec((1, tm, kw), lambda bi, i, j: (bi, i, 0)),
                  pl.BlockSpec((kw, tn), lambda bi, i, j: (0, j)),
                  pl.BlockSpec((1, tm, tn), lambda bi, i, j: (bi, i, j)),
                  pl.BlockSpec((1, 1, tn), lambda bi, i, j: (bi, 0, gate_blk_of(tn) + j))],
        out_specs=pl.BlockSpec((1, tm, tn), lambda bi, i, j: (bi, i, j)),
        out_shape=jax.ShapeDtypeStruct((b, t, d), F32),
        compiler_params=_params("parallel", "parallel", "arbitrary"),
        name="out_proj",
    )(a, w, x, mod3)


def _ff1_kernel(n_cast, h_ref, w_ref, c_ref, o_ref, co_ref):
    z = jnp.maximum(_dot(h_ref[0], w_ref[...]), 0.0)
    o_ref[0] = (z * z).astype(o_ref.dtype)
    _cast_block(c_ref, co_ref, _grid_step(3), n_cast)


def _ff1_call(h, w, cast_src):
    b, t, d = h.shape
    n = w.shape[1]
    tm = _tile(t, 1024, SUBLANES)
    tn = _tile(n, 1024)
    grid = (b, t // tm, n // tn)
    n_cast, c_in, c_out, c_shape = _cast_specs(cast_src, grid)
    return pl.pallas_call(
        functools.partial(_ff1_kernel, n_cast),
        grid=grid,
        in_specs=[pl.BlockSpec((1, tm, d), lambda bi, i, j: (bi, i, 0)),
                  pl.BlockSpec((d, tn), lambda bi, i, j: (0, j)),
                  c_in],
        out_specs=[pl.BlockSpec((1, tm, tn), lambda bi, i, j: (bi, i, j)), c_out],
        out_shape=[jax.ShapeDtypeStruct((b, t, n), BF16), c_shape],
        compiler_params=_params("arbitrary", "arbitrary", "arbitrary"),
        name="ff1",
    )(h, w, cast_src)


def _ff2_kernel(a_ref, w_ref, x_ref, gate_ref, o_ref, acc_ref):
    k = pl.program_id(3)

    @pl.when(k == 0)
    def _():
        acc_ref[...] = jnp.zeros_like(acc_ref)

    acc_ref[...] += _dot(a_ref[0], w_ref[...])

    @pl.when(k == pl.num_programs(3) - 1)
    def _():
        o_ref[0] = x_ref[0] + gate_ref[0] * acc_ref[...]


def _ff2_call(a, w, x, mod3, gate_blk_of):
    b, t, kw = a.shape
    d = w.shape[1]
    tm = _tile(t, 1024, SUBLANES)
    tn = _tile(d, 1024)
    tk = _tile(kw, 4096)
    return pl.pallas_call(
        _ff2_kernel,
        grid=(b, t // tm, d // tn, kw // tk),
        in_specs=[pl.BlockSpec((1, tm, tk), lambda bi, i, j, k: (bi, i, k)),
                  pl.BlockSpec((tk, tn), lambda bi, i, j, k: (k, j)),
                  pl.BlockSpec((1, tm, tn), lambda bi, i, j, k: (bi, i, j)),
                  pl.BlockSpec((1, 1, tn), lambda bi, i, j, k: (bi, 0, gate_blk_of(tn) + j))],
        out_specs=pl.BlockSpec((1, tm, tn), lambda bi, i, j, k: (bi, i, j)),
        out_shape=jax.ShapeDtypeStruct((b, t, d), F32),
        scratch_shapes=[pltpu.VMEM((tm, tn), F32)],
        compiler_params=_params("parallel", "parallel", "parallel", "arbitrary"),
        name="ff2",
    )(a, w, x, mod3)


def _rope_tables(t_len):
    pos = jnp.arange(t_len)
    row = (pos // GRID_W).astype(F32)
    col = (pos % GRID_W).astype(F32)
    n_freq = LANES // 4
    inv_freq = ROPE_BASE ** (-jnp.arange(n_freq, dtype=F32) / n_freq)
    ang = jnp.concatenate([row[:, None] * inv_freq, col[:, None] * inv_freq], axis=-1)
    cos, sin = jnp.cos(ang), jnp.sin(ang)
    cos_rep = jnp.repeat(cos, 2, axis=-1)
    sin_signed = jnp.stack([-sin, sin], axis=-1).reshape(t_len, LANES)
    return cos_rep, sin_signed


def kernel(x, c, ctx, c_ctx, w_mod, b_mod, norm1_g, norm2_g, w_in, hg_lb_logits, hg_norm_g, ret_decay_logit,
           w_branch_hgrn, w_branch_ret, w_out, w_ff1, w_ff2, final_norm_g):
    b, t, d = x.shape
    assert w_mod.shape[0] == 1 and hg_lb_logits.shape[0] == 2, "one layer"
    assert b + 1 <= SUBLANES
    hw = w_branch_hgrn.shape[1]
    heads = hw // LANES
    assert w_branch_ret.shape[1] == hw and w_in.shape[2] == 9 * hw + 2 * d
    n_state = 5 * hw

    c_rows = jnp.zeros((SUBLANES, d), F32).at[:b].set(c).at[b].set(c_ctx)
    mod_a = _mod_call(c_rows, w_mod[0], b_mod[0], 2 * d).reshape(SUBLANES, 1, 2 * d)

    h_x = _norm_mod_call(x, norm1_g[0], mod_a, lambda bi: bi, 0, 1)
    h_c = _norm_mod_call(ctx, norm1_g[0], mod_a, lambda bi: b, 0, 1)
    p_x, w_ff1_b = _in_proj_call(h_x, w_in[0], w_in.shape[2], 1024, cast_src=w_ff1[0])
    p_c = _in_proj_call(h_c, w_in[0], n_state, 256)

    y_hg, w_out_b = _hgrn_call(p_x, p_c, hg_lb_logits, hg_norm_g[0], heads, w_out[0])
    cos_rep, sin_signed = _rope_tables(t)
    decay_logit = jnp.broadcast_to(ret_decay_logit[0].T[:, :, None], (heads, 2, LANES))
    y_ret, w_bh_b, w_br_b, mod_b = _ret_call(p_x, p_c, decay_logit, cos_rep, sin_signed, heads,
                                             w_branch_hgrn[0], w_branch_ret[0], c_rows, w_mod[0], b_mod[0], 2 * d)
    mod_b = mod_b.reshape(SUBLANES, 1, 4 * d)

    merged = _merge_call(y_hg, y_ret, w_bh_b, w_br_b, p_x, 9 * heads, 9 * heads + d // LANES)
    x1 = _out_proj_call(merged, w_out_b, x, mod_b, lambda tn: 0)
    h2 = _norm_mod_call(x1, norm2_g[0], mod_b, lambda bi: bi, 1, 2)
    act, w_ff2_b = _ff1_call(h2, w_ff1_b, w_ff2[0])
    x2 = _ff2_call(act, w_ff2_b, x1, mod_b, lambda tn: 3 * d // tn)
    return _final_norm_call(x2, final_norm_g)
```

```python
import functools

import jax
import jax.numpy as jnp
import numpy as np
from jax import lax
from jax.experimental import pallas as pl
from jax.experimental.pallas import tpu as pltpu

F32 = jnp.float32
BF16 = jnp.bfloat16

LANES = 128
SUBLANES = 8
BF16_ROWS = 16
MXU_COLS = 256
VMEM_LIMIT = 63 * 2 ** 20
EPS = 1e-6
ROPE_BASE = 10000.0
GRID_W = 64
HG_CHUNK = 128
RET_CHUNK = 256
HG_STATE_UNROLL = 8
HG_OUT_UNROLL = 8
RET_UNROLL = 4
W_IN_SPLIT = 4


def _params(*sem):
    return pltpu.CompilerParams(dimension_semantics=sem, vmem_limit_bytes=VMEM_LIMIT)


def _tile(n, target, unit=LANES):
    t = min(n, target) // unit * unit
    while n % t:
        t -= unit
    return t


def _silu(z):
    return z * jax.nn.sigmoid(z)


def _sub_block(n):
    return MXU_COLS if n % MXU_COLS == 0 else LANES


def _dot(a, b):
    return jnp.dot(a, b, preferred_element_type=F32)


def _dot_nt(a, b):
    return lax.dot_general(a, b, (((1,), (1,)), ((), ())), preferred_element_type=F32)


def _dot_tn(a, b):
    return lax.dot_general(a, b, (((0,), (0,)), ((), ())), preferred_element_type=F32)


def _grid_step(rank):
    step = pl.program_id(0)
    for axis in range(1, rank):
        step = step * pl.num_programs(axis) + pl.program_id(axis)
    return step


def _linear_step(ids, sizes):
    step = ids[0]
    for i, n in zip(ids[1:], sizes[1:]):
        step = step * n + i
    return step


def _cast_specs(src, grid):
    rows, cols = src.shape
    steps = int(np.prod(grid))
    n_blocks = 1
    while n_blocks * 2 <= steps and rows % (n_blocks * 2) == 0 and rows // (n_blocks * 2) >= BF16_ROWS:
        n_blocks *= 2

    def idx(*ids):
        return (jnp.minimum(_linear_step(ids, grid), n_blocks - 1), 0)

    spec = pl.BlockSpec((rows // n_blocks, cols), idx)
    return n_blocks, spec, spec, jax.ShapeDtypeStruct(src.shape, BF16)


def _cast_block(src_ref, dst_ref, step, n_blocks):
    @pl.when(step < n_blocks)
    def _():
        dst_ref[...] = src_ref[...].astype(dst_ref.dtype)


def _mod_kernel(c_ref, w_ref, b_ref, o_ref):
    s = _silu(c_ref[...]).astype(BF16)
    o_ref[...] = _dot(s, w_ref[...].astype(BF16)) + b_ref[...]


def _mod_call(c_rows, w_mod, b_mod, n):
    r, d = c_rows.shape
    tn = _tile(n, 512)
    return pl.pallas_call(
        _mod_kernel,
        grid=(n // tn,),
        in_specs=[pl.BlockSpec((r, d), lambda j: (0, 0)),
                  pl.BlockSpec((d, tn), lambda j: (0, j)),
                  pl.BlockSpec((1, tn), lambda j: (0, j))],
        out_specs=pl.BlockSpec((r, tn), lambda j: (0, j)),
        out_shape=jax.ShapeDtypeStruct((r, n), F32),
        compiler_params=_params("parallel"),
        name="mod",
    )(c_rows, w_mod, b_mod.reshape(1, -1))


def _norm_mod_kernel(x_ref, g_ref, scale_ref, shift_ref, o_ref):
    x = x_ref[0]
    y = x * lax.rsqrt(jnp.mean(x * x, axis=-1, keepdims=True) + EPS) * g_ref[...]
    o_ref[0] = (y * (1.0 + scale_ref[0]) + shift_ref[0]).astype(o_ref.dtype)


def _norm_mod_call(x, gain, mod3, row_of_batch, shift_blk, scale_blk):
    b, t, d = x.shape
    tm = _tile(t, 512, SUBLANES)
    return pl.pallas_call(
        _norm_mod_kernel,
        grid=(b, t // tm),
        in_specs=[pl.BlockSpec((1, tm, d), lambda bi, i: (bi, i, 0)),
                  pl.BlockSpec((1, d), lambda bi, i: (0, 0)),
                  pl.BlockSpec((1, 1, d), lambda bi, i: (row_of_batch(bi), 0, scale_blk)),
                  pl.BlockSpec((1, 1, d), lambda bi, i: (row_of_batch(bi), 0, shift_blk))],
        out_specs=pl.BlockSpec((1, tm, d), lambda bi, i: (bi, i, 0)),
        out_shape=jax.ShapeDtypeStruct((b, t, d), BF16),
        compiler_params=_params("parallel", "parallel"),
        name="norm_mod",
    )(x, gain.reshape(1, d), mod3, mod3)


def _final_norm_kernel(x_ref, g_ref, o_ref):
    x = x_ref[0]
    o_ref[0] = x * lax.rsqrt(jnp.mean(x * x, axis=-1, keepdims=True) + EPS) * g_ref[...]


def _final_norm_call(x, gain):
    b, t, d = x.shape
    tm = _tile(t, 512, SUBLANES)
    return pl.pallas_call(
        _final_norm_kernel,
        grid=(b, t // tm),
        in_specs=[pl.BlockSpec((1, tm, d), lambda bi, i: (bi, i, 0)),
                  pl.BlockSpec((1, d), lambda bi, i: (0, 0))],
        out_specs=pl.BlockSpec((1, tm, d), lambda bi, i: (bi, i, 0)),
        out_shape=jax.ShapeDtypeStruct((b, t, d), F32),
        compiler_params=_params("parallel", "parallel"),
        name="final_norm",
    )(x, gain.reshape(1, d))


def _in_proj_tile(h_ref, w_hbm, o_ref, stage_s, wb_s, sem):
    j = pl.program_id(0)
    nj = pl.num_programs(0)
    period = pl.num_programs(1) * pl.num_programs(2)
    p = pl.program_id(1) * pl.num_programs(2) + pl.program_id(2)
    dk = stage_s.shape[0] // W_IN_SPLIT
    tn = stage_s.shape[1]

    def slice_copy(tile, q):
        col = pl.multiple_of(tile * tn, tn)
        return pltpu.make_async_copy(w_hbm.at[pl.ds(q * dk, dk), pl.ds(col, tn)],
                                     stage_s.at[pl.ds(q * dk, dk), :], sem.at[q])

    @pl.when(p == 0)
    def _():
        @pl.when(j == 0)
        def _():
            for q in range(W_IN_SPLIT):
                slice_copy(0, q).start()

        for q in range(W_IN_SPLIT):
            slice_copy(j, q).wait()
            wb_s[q * dk:(q + 1) * dk, :] = stage_s[q * dk:(q + 1) * dk, :].astype(BF16)

    for q in range(W_IN_SPLIT):
        @pl.when((p == 1 + (q * (period - 1)) // W_IN_SPLIT) & (j + 1 < nj))
        def _():
            slice_copy(j + 1, q).start()

    acc = _dot(h_ref[0], wb_s[...])
    for jj in range(o_ref.shape[1]):
        o_ref[0, jj] = acc[:, jj * LANES:(jj + 1) * LANES]


def _in_proj_kernel(h_ref, w_hbm, o_ref, stage_s, wb_s, sem):
    _in_proj_tile(h_ref, w_hbm, o_ref, stage_s, wb_s, sem)


def _in_proj_cast_kernel(n_cast, h_ref, w_hbm, c_ref, o_ref, co_ref, stage_s, wb_s, sem):
    _in_proj_tile(h_ref, w_hbm, o_ref, stage_s, wb_s, sem)
    _cast_block(c_ref, co_ref, _grid_step(3), n_cast)


def _in_proj_call(h, w, n, tm_target, cast_src=None):
    b, t, d = h.shape
    tm = _tile(t, tm_target, SUBLANES)
    tn = _tile(n, 1024)
    grid = (n // tn, b, t // tm)
    assert b * (t // tm) >= 2 and d % W_IN_SPLIT == 0
    in_specs = [pl.BlockSpec((1, tm, d), lambda j, bi, i: (bi, i, 0)),
                pl.BlockSpec(memory_space=pl.ANY)]
    out_specs = pl.BlockSpec((1, tn // LANES, tm, LANES), lambda j, bi, i: (bi, j, i, 0))
    out_shape = jax.ShapeDtypeStruct((b, n // LANES, t, LANES), F32)
    common = dict(grid=grid,
                  scratch_shapes=[pltpu.VMEM((d, tn), F32), pltpu.VMEM((d, tn), BF16),
                                  pltpu.SemaphoreType.DMA((W_IN_SPLIT,))],
                  compiler_params=_params("arbitrary", "arbitrary", "arbitrary"),
                  name="in_proj")
    if cast_src is None:
        return pl.pallas_call(_in_proj_kernel, in_specs=in_specs, out_specs=out_specs, out_shape=out_shape,
                              **common)(h, w)
    n_cast, c_in, c_out, c_shape = _cast_specs(cast_src, grid)
    return pl.pallas_call(functools.partial(_in_proj_cast_kernel, n_cast), in_specs=in_specs + [c_in],
                          out_specs=[out_specs, c_out], out_shape=[out_shape, c_shape],
                          **common)(h, w, cast_src)


def _roll_in_groups(x, shift):
    n, w = x.shape
    x3 = x.reshape(n // SUBLANES, SUBLANES, w)
    return pltpu.roll(x3, shift % SUBLANES, axis=1).reshape(n, w)


def _cumsum_rows(x, reverse):
    n = x.shape[0]
    row = lax.broadcasted_iota(jnp.int32, x.shape, 0)
    d = 1
    while d < n:
        if reverse:
            x = x + jnp.where(row < n - d, pltpu.roll(x, n - d, axis=0), 0.0)
        else:
            x = x + jnp.where(row >= d, pltpu.roll(x, d, axis=0), 0.0)
        d *= 2
    return x


def _block_row_bcast(x, block, offset):
    n, w = x.shape
    if block >= SUBLANES:
        xb = x.reshape(n // block, block, w)
        return jnp.broadcast_to(xb[:, offset:offset + 1, :], xb.shape).reshape(n, w)
    m = lax.broadcasted_iota(jnp.int32, x.shape, 0) & (block - 1)
    out = x
    for mm in range(block):
        if mm != offset:
            out = jnp.where(m == mm, _roll_in_groups(x, mm - offset), out)
    return out


def _hgrn_gates(z, lb):
    f = lb + (1.0 - lb) * jax.nn.sigmoid(z)
    return 1.0 - f, jnp.log2(f)


def _hgrn_level_halves(q, kf, kb, cf, cb, h):
    n = q.shape[0]
    qm, km = [], []
    for lo in range(0, n, 2 * h):
        mid, hi = lo + h, lo + 2 * h
        bf = jnp.broadcast_to(cf[mid - 1:mid, :], (h, LANES))
        bb = jnp.broadcast_to(cb[mid:mid + 1, :], (h, LANES))
        qm += [q[lo:mid] * jnp.exp2(cb[lo:mid] - bb), q[mid:hi] * jnp.exp2(cf[mid:hi] - bf)]
        km += [kf[lo:mid] * jnp.exp2(bf - cf[lo:mid]), kb[mid:hi] * jnp.exp2(bb - cb[mid:hi])]
    return jnp.concatenate(qm, axis=0), jnp.concatenate(km, axis=0)


def _state_update(s_ref, k, cum, v, reverse):
    n = cum.shape[0]
    edge = cum[0:1, :] if reverse else cum[n - 1:n, :]
    k_dec = (k * jnp.exp2(edge - cum)).astype(BF16)
    s_ref[...] = s_ref[...] * jnp.exp2(edge) + _dot_tn(v.astype(BF16), k_dec)


def _hgrn_kernel(n_cast, i_ref, ff_ref, fb_ref, q_ref, g_ref, ci_ref, cff_ref, cfb_ref, lbl_ref, ng_ref, c_ref,
                 y_ref, co_ref, kf_s, kb_s, cf_s, cb_s, st_s, sf_s, sb_s):
    _cast_block(c_ref, co_ref, _grid_step(2), n_cast)
    t_len, tc_len = i_ref.shape[2], ci_ref.shape[2]
    ch = HG_CHUNK
    cc = min(ch, tc_len)
    n, nc = t_len // ch, tc_len // cc

    l0, l1 = lbl_ref[0], lbl_ref[1]
    mx = jnp.maximum(l0, l1)
    e0, e1 = jnp.exp(l0 - mx), jnp.exp(l1 - mx)
    lb = e0 / (e0 + e1)
    lb_f, lb_b = lb[0:1], lb[1:2]

    sf_s[...] = jnp.zeros_like(sf_s)
    sb_s[...] = jnp.zeros_like(sb_s)

    def ctx_body(c, carry):
        rf = pl.multiple_of(c * cc, cc)
        rb = pl.multiple_of((nc - 1 - c) * cc, cc)
        k, lf = _hgrn_gates(cff_ref[0, 0, pl.ds(rf, cc), :], lb_f)
        _state_update(sf_s, k, _cumsum_rows(lf, False), ci_ref[0, 0, pl.ds(rf, cc), :], False)
        k, lf = _hgrn_gates(cfb_ref[0, 0, pl.ds(rb, cc), :], lb_b)
        _state_update(sb_s, k, _cumsum_rows(lf, True), ci_ref[0, 0, pl.ds(rb, cc), :], True)
        return carry

    lax.fori_loop(0, nc, ctx_body, 0)

    def state_body(c, carry):
        cb = n - 1 - c
        rf = pl.multiple_of(c * ch, ch)
        rb = pl.multiple_of(cb * ch, ch)
        k, lf = _hgrn_gates(ff_ref[0, 0, pl.ds(rf, ch), :], lb_f)
        cum = _cumsum_rows(lf, False)
        kf_s[pl.ds(rf, ch), :] = k
        cf_s[pl.ds(rf, ch), :] = cum
        st_s[c, :, 0:LANES] = sf_s[...].astype(BF16)
        _state_update(sf_s, k, cum, i_ref[0, 0, pl.ds(rf, ch), :], False)
        k, lf = _hgrn_gates(fb_ref[0, 0, pl.ds(rb, ch), :], lb_b)
        cum = _cumsum_rows(lf, True)
        kb_s[pl.ds(rb, ch), :] = k
        cb_s[pl.ds(rb, ch), :] = cum
        st_s[cb, :, LANES:2 * LANES] = sb_s[...].astype(BF16)
        _state_update(sb_s, k, cum, i_ref[0, 0, pl.ds(rb, ch), :], True)
        return carry

    lax.fori_loop(0, n, state_body, 0, unroll=HG_STATE_UNROLL)

    pair_xor = (lax.broadcasted_iota(jnp.int32, (ch, ch), 0) ^ lax.broadcasted_iota(jnp.int32, (ch, ch), 1))
    row = lax.broadcasted_iota(jnp.int32, (ch, LANES), 0)
    q_scale = LANES ** -0.5

    def out_body(c, carry):
        r = pl.multiple_of(c * ch, ch)
        q = _silu(q_ref[0, 0, pl.ds(r, ch), :]) * q_scale
        kf, kb = kf_s[pl.ds(r, ch), :], kb_s[pl.ds(r, ch), :]
        cf, cb = cf_s[pl.ds(r, ch), :], cb_s[pl.ds(r, ch), :]
        v = i_ref[0, 0, pl.ds(r, ch), :].astype(BF16)
        a = jnp.where(pair_xor == 0, _dot_nt(q.astype(BF16), (kf + kb).astype(BF16)), 0.0)
        odd = (row & 1) != 0
        qm = q * jnp.where(odd, 1.0 - kf, 1.0 - kb)
        a = jnp.where(pair_xor == 1, _dot_nt(qm.astype(BF16), jnp.where(odd, kb, kf).astype(BF16)), a)
        h, log_h = 2, 1
        while h < ch:
            if h >= SUBLANES:
                qm, km = _hgrn_level_halves(q, kf, kb, cf, cb, h)
            else:
                second = (row & h) != 0
                df = cf - _block_row_bcast(cf, 2 * h, h - 1)
                db = cb - _block_row_bcast(cb, 2 * h, h)
                qm = q * jnp.exp2(jnp.where(second, df, db))
                km = jnp.where(second, kb, kf) * jnp.exp2(-jnp.where(second, db, df))
            a = jnp.where((pair_xor >> log_h) == 1, _dot_nt(qm.astype(BF16), km.astype(BF16)), a)
            h, log_h = 2 * h, log_h + 1
        q_in = jnp.concatenate([q * jnp.exp2(cf), q * jnp.exp2(cb)], axis=1).astype(BF16)
        o = _dot(a.astype(BF16), v) + _dot_nt(q_in, st_s[c])
        y = o * lax.rsqrt(jnp.mean(o * o, axis=-1, keepdims=True) + EPS) * ng_ref[...]
        y_ref[0, pl.ds(r, ch), :] = (y * _silu(g_ref[0, 0, pl.ds(r, ch), :])).astype(y_ref.dtype)
        return carry

    lax.fori_loop(0, n, out_body, 0, unroll=HG_OUT_UNROLL)


def _hgrn_call(p_x, p_c, lb_logits, norm_g, heads, cast_src):
    b, _, t, _ = p_x.shape
    tc = p_c.shape[2]
    grid = (b, heads)
    n_cast, c_in, c_out, c_shape = _cast_specs(cast_src, grid)

    def col(off):
        return pl.BlockSpec((1, 1, t, LANES), lambda bi, h: (bi, off * heads + h, 0, 0))

    def ctx_col(off):
        return pl.BlockSpec((1, 1, tc, LANES), lambda bi, h: (bi, off * heads + h, 0, 0))

    return pl.pallas_call(
        functools.partial(_hgrn_kernel, n_cast),
        grid=grid,
        in_specs=[col(0), col(1), col(2), col(5), col(6), ctx_col(0), ctx_col(1), ctx_col(2),
                  pl.BlockSpec((2, 2, LANES), lambda bi, h: (0, 0, h)),
                  pl.BlockSpec((1, LANES), lambda bi, h: (0, 0)),
                  c_in],
        out_specs=[pl.BlockSpec((1, t, LANES), lambda bi, h: (bi, 0, h)), c_out],
        out_shape=[jax.ShapeDtypeStruct((b, t, heads * LANES), BF16), c_shape],
        scratch_shapes=[pltpu.VMEM((t, LANES), F32), pltpu.VMEM((t, LANES), F32),
                        pltpu.VMEM((t, LANES), F32), pltpu.VMEM((t, LANES), F32),
                        pltpu.VMEM((t // HG_CHUNK, LANES, 2 * LANES), BF16),
                        pltpu.VMEM((LANES, LANES), F32), pltpu.VMEM((LANES, LANES), F32)],
        compiler_params=_params("arbitrary", "arbitrary"),
        name="hgrn",
    )(p_x, p_x, p_x, p_x, p_x, p_c, p_c, p_c, lb_logits, norm_g.reshape(1, LANES), cast_src)


def _rope(a, cos, sin_signed, swap=None):
    if swap is None:
        lane = lax.broadcasted_iota(jnp.int32, a.shape, 1)
        partner = jnp.where((lane & 1) == 0, pltpu.roll(a, LANES - 1, axis=1), pltpu.roll(a, 1, axis=1))
    else:
        partner = _dot(a.astype(BF16), swap)
    return a * cos + partner * sin_signed


def _ret_state_update(s_ref, k, v, k_decay, s_decay):
    s_ref[...] = s_ref[...] * s_decay + _dot_tn(v.astype(BF16), (k * k_decay).astype(BF16))


def _ret_kernel(n_cast, q_ref, k_ref, v_ref, g_ref, ck_ref, cv_ref, dl_ref, cos_ref, sin_ref, sw_ref, c1_ref, c2_ref,
                mc_ref, mw_ref, mb_ref, y_ref, co1_ref, co2_ref, mo_ref, kr_s, st_s, sf_s, sb_s):
    _cast_block(c1_ref, co1_ref, _grid_step(2), n_cast)
    _cast_block(c2_ref, co2_ref, _grid_step(2), n_cast)
    _mod_kernel(mc_ref, mw_ref, mb_ref, mo_ref)
    t_len, tc_len = q_ref.shape[2], ck_ref.shape[2]
    ch = RET_CHUNK
    cc = min(ch, tc_len)
    n, nc = t_len // ch, tc_len // cc
    k_scale = LANES ** -0.5

    dl = dl_ref[0]
    gam = jnp.minimum(dl, 0.0) - jnp.log1p(jnp.exp(-jnp.abs(dl)))
    gam_f, gam_b = gam[0:1], gam[1:2]

    def decays(m):
        pos = lax.broadcasted_iota(jnp.int32, (m, LANES), 0).astype(F32)
        return dict(kf=jnp.exp(gam_f * (m - 1.0 - pos)), kb=jnp.exp(gam_b * pos),
                    qf=jnp.exp(gam_f * (pos + 1.0)), qb=jnp.exp(gam_b * (m - pos)),
                    sf=jnp.exp(gam_f * m), sb=jnp.exp(gam_b * m))

    sf_s[...] = jnp.zeros_like(sf_s)
    sb_s[...] = jnp.zeros_like(sb_s)

    dc = decays(cc)

    def ctx_body(c, carry):
        rf = pl.multiple_of(c * cc, cc)
        rb = pl.multiple_of((nc - 1 - c) * cc, cc)
        _ret_state_update(sf_s, ck_ref[0, 0, pl.ds(rf, cc), :] * k_scale, cv_ref[0, 0, pl.ds(rf, cc), :],
                          dc["kf"], dc["sf"])
        _ret_state_update(sb_s, ck_ref[0, 0, pl.ds(rb, cc), :] * k_scale, cv_ref[0, 0, pl.ds(rb, cc), :],
                          dc["kb"], dc["sb"])
        return carry

    lax.fori_loop(0, nc, ctx_body, 0)

    def rope_body(c, carry):
        r = pl.multiple_of(c * ch, ch)
        kr_s[pl.ds(r, ch), :] = _rope(k_ref[0, 0, pl.ds(r, ch), :] * k_scale,
                                      cos_ref[pl.ds(r, ch), :], sin_ref[pl.ds(r, ch), :], sw_ref[...])
        return carry

    lax.fori_loop(0, n, rope_body, 0, unroll=RET_UNROLL)

    dd = decays(ch)

    def state_body(c, carry):
        cb = n - 1 - c
        rf = pl.multiple_of(c * ch, ch)
        rb = pl.multiple_of(cb * ch, ch)
        st_s[c, :, 0:LANES] = sf_s[...].astype(BF16)
        _ret_state_update(sf_s, kr_s[pl.ds(rf, ch), :], v_ref[0, 0, pl.ds(rf, ch), :], dd["kf"], dd["sf"])
        st_s[cb, :, LANES:2 * LANES] = sb_s[...].astype(BF16)
        _ret_state_update(sb_s, kr_s[pl.ds(rb, ch), :], v_ref[0, 0, pl.ds(rb, ch), :], dd["kb"], dd["sb"])
        return carry

    lax.fori_loop(0, n, state_body, 0, unroll=RET_UNROLL)

    ti = lax.broadcasted_iota(jnp.int32, (ch, ch), 0)
    si = lax.broadcasted_iota(jnp.int32, (ch, ch), 1)
    lag = (ti - si).astype(F32)
    g_f = jnp.broadcast_to(gam_f[:, 0:1], (ch, ch))
    g_b = jnp.broadcast_to(gam_b[:, 0:1], (ch, ch))
    decay = jnp.where(ti > si, jnp.exp(g_f * jnp.maximum(lag, 0.0)),
                      jnp.where(ti < si, jnp.exp(g_b * jnp.maximum(-lag, 0.0)), 2.0))

    def out_body(c, carry):
        r = pl.multiple_of(c * ch, ch)
        q = _rope(q_ref[0, 0, pl.ds(r, ch), :], cos_ref[pl.ds(r, ch), :], sin_ref[pl.ds(r, ch), :])
        k = kr_s[pl.ds(r, ch), :].astype(BF16)
        v = v_ref[0, 0, pl.ds(r, ch), :].astype(BF16)
        a = _dot_nt(q.astype(BF16), k) * decay
        q_in = jnp.concatenate([q * dd["qf"], q * dd["qb"]], axis=1).astype(BF16)
        o = _dot(a.astype(BF16), v) + _dot_nt(q_in, st_s[c])
        oc = o - jnp.mean(o, axis=-1, keepdims=True)
        y = oc * lax.rsqrt(jnp.mean(oc * oc, axis=-1, keepdims=True) + EPS)
        y_ref[0, pl.ds(r, ch), :] = (y * _silu(g_ref[0, 0, pl.ds(r, ch), :])).astype(y_ref.dtype)
        return carry

    lax.fori_loop(0, n, out_body, 0, unroll=RET_UNROLL)


def _ret_call(p_x, p_c, decay_logit, cos, sin_signed, heads, cast_src1, cast_src2, c_rows, w_mod, b_mod, mod_col0):
    b, _, t, _ = p_x.shape
    tc = p_c.shape[2]
    grid = (b, heads)
    r, d = c_rows.shape
    mod_cols = w_mod.shape[1] - mod_col0
    mod_tn = mod_cols // (b * heads)
    assert mod_tn % LANES == 0 and mod_col0 % mod_tn == 0
    mod_blk0 = mod_col0 // mod_tn
    assert cast_src1.shape == cast_src2.shape
    n_cast, c_in, c_out, c_shape = _cast_specs(cast_src1, grid)
    swap = jnp.asarray(np.eye(LANES, dtype=np.float32)[:, np.arange(LANES) ^ 1], BF16)

    def col(off):
        return pl.BlockSpec((1, 1, t, LANES), lambda bi, h: (bi, off * heads + h, 0, 0))

    def ctx_col(off):
        return pl.BlockSpec((1, 1, tc, LANES), lambda bi, h: (bi, off * heads + h, 0, 0))

    return pl.pallas_call(
        functools.partial(_ret_kernel, n_cast),
        grid=grid,
        in_specs=[col(7), col(3), col(4), col(8), ctx_col(3), ctx_col(4),
                  pl.BlockSpec((1, 2, LANES), lambda bi, h: (h, 0, 0)),
                  pl.BlockSpec((t, LANES), lambda bi, h: (0, 0)),
                  pl.BlockSpec((t, LANES), lambda bi, h: (0, 0)),
                  pl.BlockSpec((LANES, LANES), lambda bi, h: (0, 0)),
                  c_in, c_in,
                  pl.BlockSpec((r, d), lambda bi, h: (0, 0)),
                  pl.BlockSpec((d, mod_tn), lambda bi, h: (0, mod_blk0 + bi * heads + h)),
                  pl.BlockSpec((1, mod_tn), lambda bi, h: (0, mod_blk0 + bi * heads + h))],
        out_specs=[pl.BlockSpec((1, t, LANES), lambda bi, h: (bi, 0, h)), c_out, c_out,
                   pl.BlockSpec((r, mod_tn), lambda bi, h: (0, bi * heads + h))],
        out_shape=[jax.ShapeDtypeStruct((b, t, heads * LANES), BF16), c_shape, c_shape,
                   jax.ShapeDtypeStruct((r, mod_cols), F32)],
        scratch_shapes=[pltpu.VMEM((t, LANES), F32),
                        pltpu.VMEM((t // RET_CHUNK, LANES, 2 * LANES), BF16),
                        pltpu.VMEM((LANES, LANES), F32), pltpu.VMEM((LANES, LANES), F32)],
        compiler_params=_params("arbitrary", "arbitrary"),
        name="ret",
    )(p_x, p_x, p_x, p_x, p_c, p_c, decay_logit, cos, sin_signed, swap, cast_src1, cast_src2,
      c_rows, w_mod, b_mod.reshape(1, -1))


def _merge_kernel(yh_ref, yr_ref, wh_ref, wr_ref, gh_ref, gr_ref, o_ref):
    acc_h = _dot(yh_ref[0], wh_ref[...])
    acc_r = _dot(yr_ref[0], wr_ref[...])
    for j in range(gh_ref.shape[1]):
        sl = slice(j * LANES, (j + 1) * LANES)
        o_ref[0, :, sl] = (jax.nn.sigmoid(gh_ref[0, j]) * acc_h[:, sl]
                           + jax.nn.sigmoid(gr_ref[0, j]) * acc_r[:, sl]).astype(o_ref.dtype)


def _merge_call(y_hg, y_ret, w_bh, w_br, p_x, gate_h_blk, gate_r_blk):
    b, t, kw = y_hg.shape
    d = w_bh.shape[1]
    tm = _tile(t, 1024, SUBLANES)
    tn = _tile(d, 512)
    nb = tn // LANES
    return pl.pallas_call(
        _merge_kernel,
        grid=(b, t // tm, d // tn),
        in_specs=[pl.BlockSpec((1, tm, kw), lambda bi, i, j: (bi, i, 0)),
                  pl.BlockSpec((1, tm, kw), lambda bi, i, j: (bi, i, 0)),
                  pl.BlockSpec((kw, tn), lambda bi, i, j: (0, j)),
                  pl.BlockSpec((kw, tn), lambda bi, i, j: (0, j)),
                  pl.BlockSpec((1, nb, tm, LANES), lambda bi, i, j: (bi, gate_h_blk // nb + j, i, 0)),
                  pl.BlockSpec((1, nb, tm, LANES), lambda bi, i, j: (bi, gate_r_blk // nb + j, i, 0))],
        out_specs=pl.BlockSpec((1, tm, tn), lambda bi, i, j: (bi, i, j)),
        out_shape=jax.ShapeDtypeStruct((b, t, d), BF16),
        compiler_params=_params("parallel", "parallel", "arbitrary"),
        name="merge",
    )(y_hg, y_ret, w_bh, w_br, p_x, p_x)


def _gated_residual_kernel(a_ref, w_ref, x_ref, gate_ref, o_ref):
    o_ref[0] = x_ref[0] + gate_ref[0] * _dot(a_ref[0], w_ref[...])


def _out_proj_call(a, w, x, mod3, gate_blk_of):
    b, t, kw = a.shape
    d = w.shape[1]
    tm = _tile(t, 1024, SUBLANES)
    tn = _tile(d, 1024)
    return pl.pallas_call(
        _gated_residual_kernel,
        grid=(b, t // tm, d // tn),
        in_specs=[pl.BlockSpec((1, tm, kw), lambda bi, i, j: (bi, i, 0)),
                  pl.BlockSpec((kw, tn), lambda bi, i, j: (0, j)),
                  pl.BlockSpec((1, tm, tn), lambda bi, i, j: (bi, i, j)),
                  pl.BlockSpec((1, 1, tn), lambda bi, i, j: (bi, 0, gate_blk_of(tn) + j))],
        out_specs=pl.BlockSpec((1, tm, tn), lambda bi, i, j: (bi, i, j)),
        out_shape=jax.ShapeDtypeStruct((b, t, d), F32),
        compiler_params=_params("parallel", "parallel", "arbitrary"),
        name="out_proj",
    )(a, w, x, mod3)


def _ff1_kernel(n_cast, h_ref, w_ref, c_ref, o_ref, co_ref):
    z = jnp.maximum(_dot(h_ref[0], w_ref[...]), 0.0)
    o_ref[0] = (z * z).astype(o_ref.dtype)
    _cast_block(c_ref, co_ref, _grid_step(3), n_cast)


def _ff1_call(h, w, cast_src):
    b, t, d = h.shape
    n = w.shape[1]
    tm = _tile(t, 1024, SUBLANES)
    tn = _tile(n, 1024)
    grid = (b, t // tm, n // tn)
    n_cast, c_in, c_out, c_shape = _cast_specs(cast_src, grid)
    return pl.pallas_call(
        functools.partial(_ff1_kernel, n_cast),
        grid=grid,
        in_specs=[pl.BlockSpec((1, tm, d), lambda bi, i, j: (bi, i, 0)),
                  pl.BlockSpec((d, tn), lambda bi, i, j: (0, j)),
                  c_in],
        out_specs=[pl.BlockSpec((1, tm, tn), lambda bi, i, j: (bi, i, j)), c_out],
        out_shape=[jax.ShapeDtypeStruct((b, t, n), BF16), c_shape],
        compiler_params=_params("arbitrary", "arbitrary", "arbitrary"),
        name="ff1",
    )(h, w, cast_src)


def _ff2_kernel(a_ref, w_ref, x_ref, gate_ref, o_ref):
    k = pl.program_id(3)
    sb = _sub_block(o_ref.shape[2])
    cols = [slice(c, c + sb) for c in range(0, o_ref.shape[2], sb)]

    @pl.when(k == 0)
    def _():
        for c in cols:
            o_ref[0, :, c] = x_ref[0, :, c] + gate_ref[0, :, c] * _dot(a_ref[0], w_ref[:, c])

    @pl.when(k > 0)
    def _():
        for c in cols:
            o_ref[0, :, c] += gate_ref[0, :, c] * _dot(a_ref[0], w_ref[:, c])


def _ff2_call(a, w, x, mod3, gate_blk_of):
    b, t, kw = a.shape
    d = w.shape[1]
    tm = _tile(t, 1024, SUBLANES)
    tn = _tile(d, 1024)
    tk = _tile(kw, 4096)
    return pl.pallas_call(
        _ff2_kernel,
        grid=(b, t // tm, d // tn, kw // tk),
        in_specs=[pl.BlockSpec((1, tm, tk), lambda bi, i, j, k: (bi, i, k)),
                  pl.BlockSpec((tk, tn), lambda bi, i, j, k: (k, j)),
                  pl.BlockSpec((1, tm, tn), lambda bi, i, j, k: (bi, i, j)),
                  pl.BlockSpec((1, 1, tn), lambda bi, i, j, k: (bi, 0, gate_blk_of(tn) + j))],
        out_specs=pl.BlockSpec((1, tm, tn), lambda bi, i, j, k: (bi, i, j)),
        out_shape=jax.ShapeDtypeStruct((b, t, d), F32),
        compiler_params=_params("parallel", "parallel", "parallel", "arbitrary"),
        name="ff2",
    )(a, w, x, mod3)


def _rope_tables(t_len):
    pos = jnp.arange(t_len)
    row = (pos // GRID_W).astype(F32)
    col = (pos % GRID_W).astype(F32)
    n_freq = LANES // 4
    inv_freq = ROPE_BASE ** (-jnp.arange(n_freq, dtype=F32) / n_freq)
    ang = jnp.concatenate([row[:, None] * inv_freq, col[:, None] * inv_freq], axis=-1)
    cos, sin = jnp.cos(ang), jnp.sin(ang)
    cos_rep = jnp.repeat(cos, 2, axis=-1)
    sin_signed = jnp.stack([-sin, sin], axis=-1).reshape(t_len, LANES)
    return cos_rep, sin_signed


def kernel(x, c, ctx, c_ctx, w_mod, b_mod, norm1_g, norm2_g, w_in, hg_lb_logits, hg_norm_g, ret_decay_logit,
           w_branch_hgrn, w_branch_ret, w_out, w_ff1, w_ff2, final_norm_g):
    b, t, d = x.shape
    assert w_mod.shape[0] == 1 and hg_lb_logits.shape[0] == 2, "one layer"
    assert b + 1 <= SUBLANES
    hw = w_branch_hgrn.shape[1]
    heads = hw // LANES
    assert w_branch_ret.shape[1] == hw and w_in.shape[2] == 9 * hw + 2 * d
    n_state = 5 * hw

    c_rows = jnp.zeros((SUBLANES, d), F32).at[:b].set(c).at[b].set(c_ctx)
    mod_a = _mod_call(c_rows, w_mod[0], b_mod[0], 2 * d).reshape(SUBLANES, 1, 2 * d)

    h_x = _norm_mod_call(x, norm1_g[0], mod_a, lambda bi: bi, 0, 1)
    h_c = _norm_mod_call(ctx, norm1_g[0], mod_a, lambda bi: b, 0, 1)
    p_x, w_ff1_b = _in_proj_call(h_x, w_in[0], w_in.shape[2], 1024, cast_src=w_ff1[0])
    p_c = _in_proj_call(h_c, w_in[0], n_state, 256)

    y_hg, w_out_b = _hgrn_call(p_x, p_c, hg_lb_logits, hg_norm_g[0], heads, w_out[0])
    cos_rep, sin_signed = _rope_tables(t)
    decay_logit = jnp.broadcast_to(ret_decay_logit[0].T[:, :, None], (heads, 2, LANES))
    y_ret, w_bh_b, w_br_b, mod_b = _ret_call(p_x, p_c, decay_logit, cos_rep, sin_signed, heads,
                                             w_branch_hgrn[0], w_branch_ret[0], c_rows, w_mod[0], b_mod[0], 2 * d)
    mod_b = mod_b.reshape(SUBLANES, 1, 4 * d)

    merged = _merge_call(y_hg, y_ret, w_bh_b, w_br_b, p_x, 9 * heads, 9 * heads + d // LANES)
    x1 = _out_proj_call(merged, w_out_b, x, mod_b, lambda tn: 0)
    h2 = _norm_mod_call(x1, norm2_g[0], mod_b, lambda bi: bi, 1, 2)
    act, w_ff2_b = _ff1_call(h2, w_ff1_b, w_ff2[0])
    x2 = _ff2_call(act, w_ff2_b, x1, mod_b, lambda tn: 3 * d // tn)
    return _final_norm_call(x2, final_norm_g)
```

```python
import functools

import jax
import jax.numpy as jnp
import numpy as np
from jax import lax
from jax.experimental import pallas as pl
from jax.experimental.pallas import tpu as pltpu

F32 = jnp.float32
BF16 = jnp.bfloat16

LANES = 128
SUBLANES = 8
BF16_ROWS = 16
MXU_COLS = 256
VMEM_LIMIT = 63 * 2 ** 20
EPS = 1e-6
ROPE_BASE = 10000.0
GRID_W = 64
HG_CHUNK = 128
RET_CHUNK = 256
HG_STATE_UNROLL = 8
HG_OUT_UNROLL = 8
RET_UNROLL = 4
W_IN_SPLIT = 4


def _params(*sem):
    return pltpu.CompilerParams(dimension_semantics=sem, vmem_limit_bytes=VMEM_LIMIT)


def _tile(n, target, unit=LANES):
    t = min(n, target) // unit * unit
    while n % t:
        t -= unit
    return t


def _silu(z):
    return z * jax.nn.sigmoid(z)


def _sub_block(n):
    return MXU_COLS if n % MXU_COLS == 0 else LANES


def _dot(a, b):
    return jnp.dot(a, b, preferred_element_type=F32)


def _dot_nt(a, b):
    return lax.dot_general(a, b, (((1,), (1,)), ((), ())), preferred_element_type=F32)


def _dot_tn(a, b):
    return lax.dot_general(a, b, (((0,), (0,)), ((), ())), preferred_element_type=F32)


def _grid_step(rank):
    step = pl.program_id(0)
    for axis in range(1, rank):
        step = step * pl.num_programs(axis) + pl.program_id(axis)
    return step


def _linear_step(ids, sizes):
    step = ids[0]
    for i, n in zip(ids[1:], sizes[1:]):
        step = step * n + i
    return step


def _cast_specs(src, grid):
    rows, cols = src.shape
    steps = int(np.prod(grid))
    n_blocks = 1
    while n_blocks * 2 <= steps and rows % (n_blocks * 2) == 0 and rows // (n_blocks * 2) >= BF16_ROWS:
        n_blocks *= 2

    def idx(*ids):
        return (jnp.minimum(_linear_step(ids, grid), n_blocks - 1), 0)

    spec = pl.BlockSpec((rows // n_blocks, cols), idx)
    return n_blocks, spec, spec, jax.ShapeDtypeStruct(src.shape, BF16)


def _cast_block(src_ref, dst_ref, step, n_blocks):
    @pl.when(step < n_blocks)
    def _():
        dst_ref[...] = src_ref[...].astype(dst_ref.dtype)


def _mod_kernel(c_ref, w_ref, b_ref, o_ref):
    s = _silu(c_ref[...]).astype(BF16)
    o_ref[...] = _dot(s, w_ref[...].astype(BF16)) + b_ref[...]


def _mod_call(c_rows, w_mod, b_mod, n):
    r, d = c_rows.shape
    tn = _tile(n, 512)
    return pl.pallas_call(
        _mod_kernel,
        grid=(n // tn,),
        in_specs=[pl.BlockSpec((r, d), lambda j: (0, 0)),
                  pl.BlockSpec((d, tn), lambda j: (0, j)),
                  pl.BlockSpec((1, tn), lambda j: (0, j))],
        out_specs=pl.BlockSpec((r, tn), lambda j: (0, j)),
        out_shape=jax.ShapeDtypeStruct((r, n), F32),
        compiler_params=_params("parallel"),
        name="mod",
    )(c_rows, w_mod, b_mod.reshape(1, -1))


def _norm_mod_kernel(x_ref, g_ref, scale_ref, shift_ref, o_ref):
    x = x_ref[0]
    y = x * lax.rsqrt(jnp.mean(x * x, axis=-1, keepdims=True) + EPS) * g_ref[...]
    o_ref[0] = (y * (1.0 + scale_ref[0]) + shift_ref[0]).astype(o_ref.dtype)


def _norm_mod_call(x, gain, mod3, row_of_batch, shift_blk, scale_blk):
    b, t, d = x.shape
    tm = _tile(t, 512, SUBLANES)
    return pl.pallas_call(
        _norm_mod_kernel,
        grid=(b, t // tm),
        in_specs=[pl.BlockSpec((1, tm, d), lambda bi, i: (bi, i, 0)),
                  pl.BlockSpec((1, d), lambda bi, i: (0, 0)),
                  pl.BlockSpec((1, 1, d), lambda bi, i: (row_of_batch(bi), 0, scale_blk)),
                  pl.BlockSpec((1, 1, d), lambda bi, i: (row_of_batch(bi), 0, shift_blk))],
        out_specs=pl.BlockSpec((1, tm, d), lambda bi, i: (bi, i, 0)),
        out_shape=jax.ShapeDtypeStruct((b, t, d), BF16),
        compiler_params=_params("parallel", "parallel"),
        name="norm_mod",
    )(x, gain.reshape(1, d), mod3, mod3)


def _final_norm_kernel(x_ref, g_ref, o_ref):
    x = x_ref[0]
    o_ref[0] = x * lax.rsqrt(jnp.mean(x * x, axis=-1, keepdims=True) + EPS) * g_ref[...]


def _final_norm_call(x, gain):
    b, t, d = x.shape
    tm = _tile(t, 512, SUBLANES)
    return pl.pallas_call(
        _final_norm_kernel,
        grid=(b, t // tm),
        in_specs=[pl.BlockSpec((1, tm, d), lambda bi, i: (bi, i, 0)),
                  pl.BlockSpec((1, d), lambda bi, i: (0, 0))],
        out_specs=pl.BlockSpec((1, tm, d), lambda bi, i: (bi, i, 0)),
        out_shape=jax.ShapeDtypeStruct((b, t, d), F32),
        compiler_params=_params("parallel", "parallel"),
        name="final_norm",
    )(x, gain.reshape(1, d))


def _in_proj_tile(h_ref, w_hbm, o_ref, stage_s, wb_s, sem):
    j = pl.program_id(0)
    nj = pl.num_programs(0)
    period = pl.num_programs(1) * pl.num_programs(2)
    p = pl.program_id(1) * pl.num_programs(2) + pl.program_id(2)
    dk = stage_s.shape[0] // W_IN_SPLIT
    tn = stage_s.shape[1]

    def slice_copy(tile, q):
        col = pl.multiple_of(tile * tn, tn)
        return pltpu.make_async_copy(w_hbm.at[pl.ds(q * dk, dk), pl.ds(col, tn)],
                                     stage_s.at[pl.ds(q * dk, dk), :], sem.at[q])

    @pl.when(p == 0)
    def _():
        @pl.when(j == 0)
        def _():
            for q in range(W_IN_SPLIT):
                slice_copy(0, q).start()

        for q in range(W_IN_SPLIT):
            slice_copy(j, q).wait()
            wb_s[q * dk:(q + 1) * dk, :] = stage_s[q * dk:(q + 1) * dk, :].astype(BF16)

    for q in range(W_IN_SPLIT):
        @pl.when((p == 1 + (q * (period - 1)) // W_IN_SPLIT) & (j + 1 < nj))
        def _():
            slice_copy(j + 1, q).start()

    acc = _dot(h_ref[0], wb_s[...])
    for jj in range(o_ref.shape[1]):
        o_ref[0, jj] = acc[:, jj * LANES:(jj + 1) * LANES]


def _in_proj_kernel(h_ref, w_hbm, o_ref, stage_s, wb_s, sem):
    _in_proj_tile(h_ref, w_hbm, o_ref, stage_s, wb_s, sem)


def _in_proj_cast_kernel(n_cast, h_ref, w_hbm, c_ref, o_ref, co_ref, stage_s, wb_s, sem):
    _in_proj_tile(h_ref, w_hbm, o_ref, stage_s, wb_s, sem)
    _cast_block(c_ref, co_ref, _grid_step(3), n_cast)


def _in_proj_call(h, w, n, tm_target, cast_src=None):
    b, t, d = h.shape
    tm = _tile(t, tm_target, SUBLANES)
    tn = _tile(n, 1024)
    grid = (n // tn, b, t // tm)
    assert b * (t // tm) >= 2 and d % W_IN_SPLIT == 0
    in_specs = [pl.BlockSpec((1, tm, d), lambda j, bi, i: (bi, i, 0)),
                pl.BlockSpec(memory_space=pl.ANY)]
    out_specs = pl.BlockSpec((1, tn // LANES, tm, LANES), lambda j, bi, i: (bi, j, i, 0))
    out_shape = jax.ShapeDtypeStruct((b, n // LANES, t, LANES), F32)
    common = dict(grid=grid,
                  scratch_shapes=[pltpu.VMEM((d, tn), F32), pltpu.VMEM((d, tn), BF16),
                                  pltpu.SemaphoreType.DMA((W_IN_SPLIT,))],
                  compiler_params=_params("arbitrary", "arbitrary", "arbitrary"),
                  name="in_proj")
    if cast_src is None:
        return pl.pallas_call(_in_proj_kernel, in_specs=in_specs, out_specs=out_specs, out_shape=out_shape,
                              **common)(h, w)
    n_cast, c_in, c_out, c_shape = _cast_specs(cast_src, grid)
    return pl.pallas_call(functools.partial(_in_proj_cast_kernel, n_cast), in_specs=in_specs + [c_in],
                          out_specs=[out_specs, c_out], out_shape=[out_shape, c_shape],
                          **common)(h, w, cast_src)


def _cumsum_rows(x, reverse):
    n = x.shape[0]
    row = lax.broadcasted_iota(jnp.int32, x.shape, 0)
    d = 1
    while d < n:
        if reverse:
            x = x + jnp.where(row < n - d, pltpu.roll(x, n - d, axis=0), 0.0)
        else:
            x = x + jnp.where(row >= d, pltpu.roll(x, d, axis=0), 0.0)
        d *= 2
    return x


def _block_row_bcast(x, block, offset):
    n, w = x.shape
    if block >= SUBLANES:
        xb = x.reshape(n // block, block, w)
        return jnp.broadcast_to(xb[:, offset:offset + 1, :], xb.shape).reshape(n, w)
    x3 = x.reshape(n // SUBLANES, SUBLANES, w)
    sub = lax.broadcasted_iota(jnp.int32, x3.shape, 1)
    out = None
    for first in range(0, SUBLANES, block):
        src = jnp.broadcast_to(x3[:, first + offset:first + offset + 1, :], x3.shape)
        out = src if out is None else jnp.where(sub >= first, src, out)
    return out.reshape(n, w)


def _hgrn_gates(z, lb):
    f = lb + (1.0 - lb) * jax.nn.sigmoid(z)
    return 1.0 - f, jnp.log2(f)


def _hgrn_level_halves(q, kf, kb, cf, cb, h):
    n = q.shape[0]
    qm, km = [], []
    for lo in range(0, n, 2 * h):
        mid, hi = lo + h, lo + 2 * h
        bf = jnp.broadcast_to(cf[mid - 1:mid, :], (h, LANES))
        bb = jnp.broadcast_to(cb[mid:mid + 1, :], (h, LANES))
        qm += [q[lo:mid] * jnp.exp2(cb[lo:mid] - bb), q[mid:hi] * jnp.exp2(cf[mid:hi] - bf)]
        km += [kf[lo:mid] * jnp.exp2(bf - cf[lo:mid]), kb[mid:hi] * jnp.exp2(bb - cb[mid:hi])]
    return jnp.concatenate(qm, axis=0), jnp.concatenate(km, axis=0)


def _state_update(s_ref, k, cum, v, reverse):
    n = cum.shape[0]
    edge = cum[0:1, :] if reverse else cum[n - 1:n, :]
    k_dec = (k * jnp.exp2(edge - cum)).astype(BF16)
    s_ref[...] = s_ref[...] * jnp.exp2(edge) + _dot_tn(v.astype(BF16), k_dec)


def _hgrn_kernel(n_cast, i_ref, ff_ref, fb_ref, q_ref, g_ref, ci_ref, cff_ref, cfb_ref, lbl_ref, ng_ref, c_ref,
                 y_ref, co_ref, kf_s, kb_s, cf_s, cb_s, st_s, sf_s, sb_s):
    _cast_block(c_ref, co_ref, _grid_step(2), n_cast)
    t_len, tc_len = i_ref.shape[2], ci_ref.shape[2]
    ch = HG_CHUNK
    cc = min(ch, tc_len)
    n, nc = t_len // ch, tc_len // cc

    l0, l1 = lbl_ref[0], lbl_ref[1]
    mx = jnp.maximum(l0, l1)
    e0, e1 = jnp.exp(l0 - mx), jnp.exp(l1 - mx)
    lb = e0 / (e0 + e1)
    lb_f, lb_b = lb[0:1], lb[1:2]

    sf_s[...] = jnp.zeros_like(sf_s)
    sb_s[...] = jnp.zeros_like(sb_s)

    def ctx_body(c, carry):
        rf = pl.multiple_of(c * cc, cc)
        rb = pl.multiple_of((nc - 1 - c) * cc, cc)
        k, lf = _hgrn_gates(cff_ref[0, 0, pl.ds(rf, cc), :], lb_f)
        _state_update(sf_s, k, _cumsum_rows(lf, False), ci_ref[0, 0, pl.ds(rf, cc), :], False)
        k, lf = _hgrn_gates(cfb_ref[0, 0, pl.ds(rb, cc), :], lb_b)
        _state_update(sb_s, k, _cumsum_rows(lf, True), ci_ref[0, 0, pl.ds(rb, cc), :], True)
        return carry

    lax.fori_loop(0, nc, ctx_body, 0)

    def state_body(c, carry):
        cb = n - 1 - c
        rf = pl.multiple_of(c * ch, ch)
        rb = pl.multiple_of(cb * ch, ch)
        k, lf = _hgrn_gates(ff_ref[0, 0, pl.ds(rf, ch), :], lb_f)
        cum = _cumsum_rows(lf, False)
        kf_s[pl.ds(rf, ch), :] = k
        cf_s[pl.ds(rf, ch), :] = cum
        st_s[c, :, 0:LANES] = sf_s[...].astype(BF16)
        _state_update(sf_s, k, cum, i_ref[0, 0, pl.ds(rf, ch), :], False)
        k, lf = _hgrn_gates(fb_ref[0, 0, pl.ds(rb, ch), :], lb_b)
        cum = _cumsum_rows(lf, True)
        kb_s[pl.ds(rb, ch), :] = k
        cb_s[pl.ds(rb, ch), :] = cum
        st_s[cb, :, LANES:2 * LANES] = sb_s[...].astype(BF16)
        _state_update(sb_s, k, cum, i_ref[0, 0, pl.ds(rb, ch), :], True)
        return carry

    lax.fori_loop(0, n, state_body, 0, unroll=HG_STATE_UNROLL)

    pair_xor = (lax.broadcasted_iota(jnp.int32, (ch, ch), 0) ^ lax.broadcasted_iota(jnp.int32, (ch, ch), 1))
    row = lax.broadcasted_iota(jnp.int32, (ch, LANES), 0)
    q_scale = LANES ** -0.5

    def out_body(c, carry):
        r = pl.multiple_of(c * ch, ch)
        q = _silu(q_ref[0, 0, pl.ds(r, ch), :]) * q_scale
        kf, kb = kf_s[pl.ds(r, ch), :], kb_s[pl.ds(r, ch), :]
        cf, cb = cf_s[pl.ds(r, ch), :], cb_s[pl.ds(r, ch), :]
        v = i_ref[0, 0, pl.ds(r, ch), :].astype(BF16)
        a = jnp.where(pair_xor == 0, _dot_nt(q.astype(BF16), (kf + kb).astype(BF16)), 0.0)
        odd = (row & 1) != 0
        qm = q * jnp.where(odd, 1.0 - kf, 1.0 - kb)
        a = jnp.where(pair_xor == 1, _dot_nt(qm.astype(BF16), jnp.where(odd, kb, kf).astype(BF16)), a)
        h, log_h = 2, 1
        while h < ch:
            if h >= SUBLANES:
                qm, km = _hgrn_level_halves(q, kf, kb, cf, cb, h)
            else:
                second = (row & h) != 0
                df = cf - _block_row_bcast(cf, 2 * h, h - 1)
                db = cb - _block_row_bcast(cb, 2 * h, h)
                qm = q * jnp.exp2(jnp.where(second, df, db))
                km = jnp.where(second, kb, kf) * jnp.exp2(-jnp.where(second, db, df))
            a = jnp.where((pair_xor >> log_h) == 1, _dot_nt(qm.astype(BF16), km.astype(BF16)), a)
            h, log_h = 2 * h, log_h + 1
        q_in = jnp.concatenate([q * jnp.exp2(cf), q * jnp.exp2(cb)], axis=1).astype(BF16)
        o = _dot(a.astype(BF16), v) + _dot_nt(q_in, st_s[c])
        y = o * lax.rsqrt(jnp.mean(o * o, axis=-1, keepdims=True) + EPS) * ng_ref[...]
        y_ref[0, pl.ds(r, ch), :] = (y * _silu(g_ref[0, 0, pl.ds(r, ch), :])).astype(y_ref.dtype)
        return carry

    lax.fori_loop(0, n, out_body, 0, unroll=HG_OUT_UNROLL)


def _hgrn_call(p_x, p_c, lb_logits, norm_g, heads, cast_src):
    b, _, t, _ = p_x.shape
    tc = p_c.shape[2]
    grid = (b, heads)
    n_cast, c_in, c_out, c_shape = _cast_specs(cast_src, grid)

    def col(off):
        return pl.BlockSpec((1, 1, t, LANES), lambda bi, h: (bi, off * heads + h, 0, 0))

    def ctx_col(off):
        return pl.BlockSpec((1, 1, tc, LANES), lambda bi, h: (bi, off * heads + h, 0, 0))

    return pl.pallas_call(
        functools.partial(_hgrn_kernel, n_cast),
        grid=grid,
        in_specs=[col(0), col(1), col(2), col(5), col(6), ctx_col(0), ctx_col(1), ctx_col(2),
                  pl.BlockSpec((2, 2, LANES), lambda bi, h: (0, 0, h)),
                  pl.BlockSpec((1, LANES), lambda bi, h: (0, 0)),
                  c_in],
        out_specs=[pl.BlockSpec((1, t, LANES), lambda bi, h: (bi, 0, h)), c_out],
        out_shape=[jax.ShapeDtypeStruct((b, t, heads * LANES), BF16), c_shape],
        scratch_shapes=[pltpu.VMEM((t, LANES), F32), pltpu.VMEM((t, LANES), F32),
                        pltpu.VMEM((t, LANES), F32), pltpu.VMEM((t, LANES), F32),
                        pltpu.VMEM((t // HG_CHUNK, LANES, 2 * LANES), BF16),
                        pltpu.VMEM((LANES, LANES), F32), pltpu.VMEM((LANES, LANES), F32)],
        compiler_params=_params("arbitrary", "arbitrary"),
        name="hgrn",
    )(p_x, p_x, p_x, p_x, p_x, p_c, p_c, p_c, lb_logits, norm_g.reshape(1, LANES), cast_src)


def _rope(a, cos, sin_signed, swap=None):
    if swap is None:
        lane = lax.broadcasted_iota(jnp.int32, a.shape, 1)
        partner = jnp.where((lane & 1) == 0, pltpu.roll(a, LANES - 1, axis=1), pltpu.roll(a, 1, axis=1))
    else:
        partner = _dot(a.astype(BF16), swap)
    return a * cos + partner * sin_signed


def _ret_state_update(s_ref, k, v, k_decay, s_decay):
    s_ref[...] = s_ref[...] * s_decay + _dot_tn(v.astype(BF16), (k * k_decay).astype(BF16))


def _ret_kernel(n_cast, q_ref, k_ref, v_ref, g_ref, ck_ref, cv_ref, dl_ref, cos_ref, sin_ref, sw_ref, c1_ref, c2_ref,
                mc_ref, mw_ref, mb_ref, y_ref, co1_ref, co2_ref, mo_ref, kr_s, st_s, sf_s, sb_s):
    _cast_block(c1_ref, co1_ref, _grid_step(2), n_cast)
    _cast_block(c2_ref, co2_ref, _grid_step(2), n_cast)
    _mod_kernel(mc_ref, mw_ref, mb_ref, mo_ref)
    t_len, tc_len = q_ref.shape[2], ck_ref.shape[2]
    ch = RET_CHUNK
    cc = min(ch, tc_len)
    n, nc = t_len // ch, tc_len // cc
    k_scale = LANES ** -0.5

    dl = dl_ref[0]
    gam = jnp.minimum(dl, 0.0) - jnp.log1p(jnp.exp(-jnp.abs(dl)))
    gam_f, gam_b = gam[0:1], gam[1:2]

    def decays(m):
        pos = lax.broadcasted_iota(jnp.int32, (m, LANES), 0).astype(F32)
        return dict(kf=jnp.exp(gam_f * (m - 1.0 - pos)), kb=jnp.exp(gam_b * pos),
                    qf=jnp.exp(gam_f * (pos + 1.0)), qb=jnp.exp(gam_b * (m - pos)),
                    sf=jnp.exp(gam_f * m), sb=jnp.exp(gam_b * m))

    sf_s[...] = jnp.zeros_like(sf_s)
    sb_s[...] = jnp.zeros_like(sb_s)

    dc = decays(cc)

    def ctx_body(c, carry):
        rf = pl.multiple_of(c * cc, cc)
        rb = pl.multiple_of((nc - 1 - c) * cc, cc)
        _ret_state_update(sf_s, ck_ref[0, 0, pl.ds(rf, cc), :] * k_scale, cv_ref[0, 0, pl.ds(rf, cc), :],
                          dc["kf"], dc["sf"])
        _ret_state_update(sb_s, ck_ref[0, 0, pl.ds(rb, cc), :] * k_scale, cv_ref[0, 0, pl.ds(rb, cc), :],
                          dc["kb"], dc["sb"])
        return carry

    lax.fori_loop(0, nc, ctx_body, 0)

    def rope_body(c, carry):
        r = pl.multiple_of(c * ch, ch)
        kr_s[pl.ds(r, ch), :] = _rope(k_ref[0, 0, pl.ds(r, ch), :] * k_scale,
                                      cos_ref[pl.ds(r, ch), :], sin_ref[pl.ds(r, ch), :], sw_ref[...])
        return carry

    lax.fori_loop(0, n, rope_body, 0, unroll=RET_UNROLL)

    dd = decays(ch)

    def state_body(c, carry):
        cb = n - 1 - c
        rf = pl.multiple_of(c * ch, ch)
        rb = pl.multiple_of(cb * ch, ch)
        st_s[c, :, 0:LANES] = sf_s[...].astype(BF16)
        _ret_state_update(sf_s, kr_s[pl.ds(rf, ch), :], v_ref[0, 0, pl.ds(rf, ch), :], dd["kf"], dd["sf"])
        st_s[cb, :, LANES:2 * LANES] = sb_s[...].astype(BF16)
        _ret_state_update(sb_s, kr_s[pl.ds(rb, ch), :], v_ref[0, 0, pl.ds(rb, ch), :], dd["kb"], dd["sb"])
        return carry

    lax.fori_loop(0, n, state_body, 0, unroll=RET_UNROLL)

    ti = lax.broadcasted_iota(jnp.int32, (ch, ch), 0)
    si = lax.broadcasted_iota(jnp.int32, (ch, ch), 1)
    lag = (ti - si).astype(F32)
    g_f = jnp.broadcast_to(gam_f[:, 0:1], (ch, ch))
    g_b = jnp.broadcast_to(gam_b[:, 0:1], (ch, ch))
    decay = jnp.where(ti > si, jnp.exp(g_f * jnp.maximum(lag, 0.0)),
                      jnp.where(ti < si, jnp.exp(g_b * jnp.maximum(-lag, 0.0)), 2.0))

    def out_body(c, carry):
        r = pl.multiple_of(c * ch, ch)
        q = _rope(q_ref[0, 0, pl.ds(r, ch), :], cos_ref[pl.ds(r, ch), :], sin_ref[pl.ds(r, ch), :])
        k = kr_s[pl.ds(r, ch), :].astype(BF16)
        v = v_ref[0, 0, pl.ds(r, ch), :].astype(BF16)
        a = _dot_nt(q.astype(BF16), k) * decay
        q_in = jnp.concatenate([q * dd["qf"], q * dd["qb"]], axis=1).astype(BF16)
        o = _dot(a.astype(BF16), v) + _dot_nt(q_in, st_s[c])
        oc = o - jnp.mean(o, axis=-1, keepdims=True)
        y = oc * lax.rsqrt(jnp.mean(oc * oc, axis=-1, keepdims=True) + EPS)
        y_ref[0, pl.ds(r, ch), :] = (y * _silu(g_ref[0, 0, pl.ds(r, ch), :])).astype(y_ref.dtype)
        return carry

    lax.fori_loop(0, n, out_body, 0, unroll=RET_UNROLL)


def _ret_call(p_x, p_c, decay_logit, cos, sin_signed, heads, cast_src1, cast_src2, c_rows, w_mod, b_mod, mod_col0):
    b, _, t, _ = p_x.shape
    tc = p_c.shape[2]
    grid = (b, heads)
    r, d = c_rows.shape
    mod_cols = w_mod.shape[1] - mod_col0
    mod_tn = mod_cols // (b * heads)
    assert mod_tn % LANES == 0 and mod_col0 % mod_tn == 0
    mod_blk0 = mod_col0 // mod_tn
    assert cast_src1.shape == cast_src2.shape
    n_cast, c_in, c_out, c_shape = _cast_specs(cast_src1, grid)
    swap = jnp.asarray(np.eye(LANES, dtype=np.float32)[:, np.arange(LANES) ^ 1], BF16)

    def col(off):
        return pl.BlockSpec((1, 1, t, LANES), lambda bi, h: (bi, off * heads + h, 0, 0))

    def ctx_col(off):
        return pl.BlockSpec((1, 1, tc, LANES), lambda bi, h: (bi, off * heads + h, 0, 0))

    return pl.pallas_call(
        functools.partial(_ret_kernel, n_cast),
        grid=grid,
        in_specs=[col(7), col(3), col(4), col(8), ctx_col(3), ctx_col(4),
                  pl.BlockSpec((1, 2, LANES), lambda bi, h: (h, 0, 0)),
                  pl.BlockSpec((t, LANES), lambda bi, h: (0, 0)),
                  pl.BlockSpec((t, LANES), lambda bi, h: (0, 0)),
                  pl.BlockSpec((LANES, LANES), lambda bi, h: (0, 0)),
                  c_in, c_in,
                  pl.BlockSpec((r, d), lambda bi, h: (0, 0)),
                  pl.BlockSpec((d, mod_tn), lambda bi, h: (0, mod_blk0 + bi * heads + h)),
                  pl.BlockSpec((1, mod_tn), lambda bi, h: (0, mod_blk0 + bi * heads + h))],
        out_specs=[pl.BlockSpec((1, t, LANES), lambda bi, h: (bi, 0, h)), c_out, c_out,
                   pl.BlockSpec((r, mod_tn), lambda bi, h: (0, bi * heads + h))],
        out_shape=[jax.ShapeDtypeStruct((b, t, heads * LANES), BF16), c_shape, c_shape,
                   jax.ShapeDtypeStruct((r, mod_cols), F32)],
        scratch_shapes=[pltpu.VMEM((t, LANES), F32),
                        pltpu.VMEM((t // RET_CHUNK, LANES, 2 * LANES), BF16),
                        pltpu.VMEM((LANES, LANES), F32), pltpu.VMEM((LANES, LANES), F32)],
        compiler_params=_params("arbitrary", "arbitrary"),
        name="ret",
    )(p_x, p_x, p_x, p_x, p_c, p_c, decay_logit, cos, sin_signed, swap, cast_src1, cast_src2,
      c_rows, w_mod, b_mod.reshape(1, -1))


def _merge_kernel(yh_ref, yr_ref, wh_ref, wr_ref, gh_ref, gr_ref, o_ref):
    acc_h = _dot(yh_ref[0], wh_ref[...])
    acc_r = _dot(yr_ref[0], wr_ref[...])
    for j in range(gh_ref.shape[1]):
        sl = slice(j * LANES, (j + 1) * LANES)
        o_ref[0, :, sl] = (jax.nn.sigmoid(gh_ref[0, j]) * acc_h[:, sl]
                           + jax.nn.sigmoid(gr_ref[0, j]) * acc_r[:, sl]).astype(o_ref.dtype)


def _merge_call(y_hg, y_ret, w_bh, w_br, p_x, gate_h_blk, gate_r_blk):
    b, t, kw = y_hg.shape
    d = w_bh.shape[1]
    tm = _tile(t, 1024, SUBLANES)
    tn = _tile(d, 512)
    nb = tn // LANES
    return pl.pallas_call(
        _merge_kernel,
        grid=(b, t // tm, d // tn),
        in_specs=[pl.BlockSpec((1, tm, kw), lambda bi, i, j: (bi, i, 0)),
                  pl.BlockSpec((1, tm, kw), lambda bi, i, j: (bi, i, 0)),
                  pl.BlockSpec((kw, tn), lambda bi, i, j: (0, j)),
                  pl.BlockSpec((kw, tn), lambda bi, i, j: (0, j)),
                  pl.BlockSpec((1, nb, tm, LANES), lambda bi, i, j: (bi, gate_h_blk // nb + j, i, 0)),
                  pl.BlockSpec((1, nb, tm, LANES), lambda bi, i, j: (bi, gate_r_blk // nb + j, i, 0))],
        out_specs=pl.BlockSpec((1, tm, tn), lambda bi, i, j: (bi, i, j)),
        out_shape=jax.ShapeDtypeStruct((b, t, d), BF16),
        compiler_params=_params("parallel", "parallel", "arbitrary"),
        name="merge",
    )(y_hg, y_ret, w_bh, w_br, p_x, p_x)


def _gated_residual_kernel(a_ref, w_ref, x_ref, gate_ref, o_ref):
    o_ref[0] = x_ref[0] + gate_ref[0] * _dot(a_ref[0], w_ref[...])


def _out_proj_call(a, w, x, mod3, gate_blk_of):
    b, t, kw = a.shape
    d = w.shape[1]
    tm = _tile(t, 1024, SUBLANES)
    tn = _tile(d, 1024)
    return pl.pallas_call(
        _gated_residual_kernel,
        grid=(b, t // tm, d // tn),
        in_specs=[pl.BlockSpec((1, tm, kw), lambda bi, i, j: (bi, i, 0)),
                  pl.BlockSpec((kw, tn), lambda bi, i, j: (0, j)),
                  pl.BlockSpec((1, tm, tn), lambda bi, i, j: (bi, i, j)),
                  pl.BlockSpec((1, 1, tn), lambda bi, i, j: (bi, 0, gate_blk_of(tn) + j))],
        out_specs=pl.BlockSpec((1, tm, tn), lambda bi, i, j: (bi, i, j)),
        out_shape=jax.ShapeDtypeStruct((b, t, d), F32),
        compiler_params=_params("parallel", "parallel", "arbitrary"),
        name="out_proj",
    )(a, w, x, mod3)


def _ff1_kernel(n_cast, h_ref, w_ref, c_ref, o_ref, co_ref):
    z = jnp.maximum(_dot(h_ref[0], w_ref[...]), 0.0)
    o_ref[0] = (z * z).astype(o_ref.dtype)
    _cast_block(c_ref, co_ref, _grid_step(3), n_cast)


def _ff1_call(h, w, cast_src):
    b, t, d = h.shape
    n = w.shape[1]
    tm = _tile(t, 1024, SUBLANES)
    tn = _tile(n, 1024)
    grid = (b, t // tm, n // tn)
    n_cast, c_in, c_out, c_shape = _cast_specs(cast_src, grid)
    return pl.pallas_call(
        functools.partial(_ff1_kernel, n_cast),
        grid=grid,
        in_specs=[pl.BlockSpec((1, tm, d), lambda bi, i, j: (bi, i, 0)),
                  pl.BlockSpec((d, tn), lambda bi, i, j: (0, j)),
                  c_in],
        out_specs=[pl.BlockSpec((1, tm, tn), lambda bi, i, j: (bi, i, j)), c_out],
        out_shape=[jax.ShapeDtypeStruct((b, t, n), BF16), c_shape],
        compiler_params=_params("arbitrary", "arbitrary", "arbitrary"),
        name="ff1",
    )(h, w, cast_src)


def _ff2_kernel(a_ref, w_ref, x_ref, gate_ref, o_ref):
    k = pl.program_id(3)
    sb = _sub_block(o_ref.shape[2])
    cols = [slice(c, c + sb) for c in range(0, o_ref.shape[2], sb)]

    @pl.when(k == 0)
    def _():
        for c in cols:
            o_ref[0, :, c] = x_ref[0, :, c] + gate_ref[0, :, c] * _dot(a_ref[0], w_ref[:, c])

    @pl.when(k > 0)
    def _():
        for c in cols:
            o_ref[0, :, c] += gate_ref[0, :, c] * _dot(a_ref[0], w_ref[:, c])


def _ff2_call(a, w, x, mod3, gate_blk_of):
    b, t, kw = a.shape
    d = w.shape[1]
    tm = _tile(t, 1024, SUBLANES)
    tn = _tile(d, 1024)
    tk = _tile(kw, 4096)
    return pl.pallas_call(
        _ff2_kernel,
        grid=(b, t // tm, d // tn, kw // tk),
        in_specs=[pl.BlockSpec((1, tm, tk), lambda bi, i, j, k: (bi, i, k)),
                  pl.BlockSpec((tk, tn), lambda bi, i, j, k: (k, j)),
                  pl.BlockSpec((1, tm, tn), lambda bi, i, j, k: (bi, i, j)),
                  pl.BlockSpec((1, 1, tn), lambda bi, i, j, k: (bi, 0, gate_blk_of(tn) + j))],
        out_specs=pl.BlockSpec((1, tm, tn), lambda bi, i, j, k: (bi, i, j)),
        out_shape=jax.ShapeDtypeStruct((b, t, d), F32),
        compiler_params=_params("parallel", "parallel", "parallel", "arbitrary"),
        name="ff2",
    )(a, w, x, mod3)


def _rope_tables(t_len):
    pos = jnp.arange(t_len)
    row = (pos // GRID_W).astype(F32)
    col = (pos % GRID_W).astype(F32)
    n_freq = LANES // 4
    inv_freq = ROPE_BASE ** (-jnp.arange(n_freq, dtype=F32) / n_freq)
    ang = jnp.concatenate([row[:, None] * inv_freq, col[:, None] * inv_freq], axis=-1)
    cos, sin = jnp.cos(ang), jnp.sin(ang)
    cos_rep = jnp.repeat(cos, 2, axis=-1)
    sin_signed = jnp.stack([-sin, sin], axis=-1).reshape(t_len, LANES)
    return cos_rep, sin_signed


def kernel(x, c, ctx, c_ctx, w_mod, b_mod, norm1_g, norm2_g, w_in, hg_lb_logits, hg_norm_g, ret_decay_logit,
           w_branch_hgrn, w_branch_ret, w_out, w_ff1, w_ff2, final_norm_g):
    b, t, d = x.shape
    assert w_mod.shape[0] == 1 and hg_lb_logits.shape[0] == 2, "one layer"
    assert b + 1 <= SUBLANES
    hw = w_branch_hgrn.shape[1]
    heads = hw // LANES
    assert w_branch_ret.shape[1] == hw and w_in.shape[2] == 9 * hw + 2 * d
    n_state = 5 * hw

    c_rows = jnp.zeros((SUBLANES, d), F32).at[:b].set(c).at[b].set(c_ctx)
    mod_a = _mod_call(c_rows, w_mod[0], b_mod[0], 2 * d).reshape(SUBLANES, 1, 2 * d)

    h_x = _norm_mod_call(x, norm1_g[0], mod_a, lambda bi: bi, 0, 1)
    h_c = _norm_mod_call(ctx, norm1_g[0], mod_a, lambda bi: b, 0, 1)
    p_x, w_ff1_b = _in_proj_call(h_x, w_in[0], w_in.shape[2], 1024, cast_src=w_ff1[0])
    p_c = _in_proj_call(h_c, w_in[0], n_state, 256)

    y_hg, w_out_b = _hgrn_call(p_x, p_c, hg_lb_logits, hg_norm_g[0], heads, w_out[0])
    cos_rep, sin_signed = _rope_tables(t)
    decay_logit = jnp.broadcast_to(ret_decay_logit[0].T[:, :, None], (heads, 2, LANES))
    y_ret, w_bh_b, w_br_b, mod_b = _ret_call(p_x, p_c, decay_logit, cos_rep, sin_signed, heads,
                                             w_branch_hgrn[0], w_branch_ret[0], c_rows, w_mod[0], b_mod[0], 2 * d)
    mod_b = mod_b.reshape(SUBLANES, 1, 4 * d)

    merged = _merge_call(y_hg, y_ret, w_bh_b, w_br_b, p_x, 9 * heads, 9 * heads + d // LANES)
    x1 = _out_proj_call(merged, w_out_b, x, mod_b, lambda tn: 0)
    h2 = _norm_mod_call(x1, norm2_g[0], mod_b, lambda bi: bi, 1, 2)
    act, w_ff2_b = _ff1_call(h2, w_ff1_b, w_ff2[0])
    x2 = _ff2_call(act, w_ff2_b, x1, mod_b, lambda tn: 3 * d // tn)
    return _final_norm_call(x2, final_norm_g)
```

```python
import functools

import jax
import jax.numpy as jnp
import numpy as np
from jax import lax
from jax.experimental import pallas as pl
from jax.experimental.pallas import tpu as pltpu

F32 = jnp.float32
BF16 = jnp.bfloat16

LANES = 128
SUBLANES = 8
BF16_ROWS = 16
MXU_COLS = 256
VMEM_LIMIT = 63 * 2 ** 20
EPS = 1e-6
ROPE_BASE = 10000.0
GRID_W = 64
HG_CHUNK = 128
RET_CHUNK = 256
HG_STATE_UNROLL = 8
HG_OUT_UNROLL = 8
RET_UNROLL = 8
RET_OUT_UNROLL = 4
W_IN_SPLIT = 4


def _params(*sem):
    return pltpu.CompilerParams(dimension_semantics=sem, vmem_limit_bytes=VMEM_LIMIT)


def _tile(n, target, unit=LANES):
    t = min(n, target) // unit * unit
    while n % t:
        t -= unit
    return t


def _silu(z):
    return z * jax.nn.sigmoid(z)


def _sub_block(n):
    return MXU_COLS if n % MXU_COLS == 0 else LANES


def _dot(a, b):
    return jnp.dot(a, b, preferred_element_type=F32)


def _dot_nt(a, b):
    return lax.dot_general(a, b, (((1,), (1,)), ((), ())), preferred_element_type=F32)


def _dot_tn(a, b):
    return lax.dot_general(a, b, (((0,), (0,)), ((), ())), preferred_element_type=F32)


def _grid_step(rank):
    step = pl.program_id(0)
    for axis in range(1, rank):
        step = step * pl.num_programs(axis) + pl.program_id(axis)
    return step


def _linear_step(ids, sizes):
    step = ids[0]
    for i, n in zip(ids[1:], sizes[1:]):
        step = step * n + i
    return step


def _cast_specs(src, grid):
    rows, cols = src.shape
    steps = int(np.prod(grid))
    n_blocks = 1
    while n_blocks * 2 <= steps and rows % (n_blocks * 2) == 0 and rows // (n_blocks * 2) >= BF16_ROWS:
        n_blocks *= 2

    def idx(*ids):
        return (jnp.minimum(_linear_step(ids, grid), n_blocks - 1), 0)

    spec = pl.BlockSpec((rows // n_blocks, cols), idx)
    return n_blocks, spec, spec, jax.ShapeDtypeStruct(src.shape, BF16)


def _cast_block(src_ref, dst_ref, step, n_blocks):
    @pl.when(step < n_blocks)
    def _():
        dst_ref[...] = src_ref[...].astype(dst_ref.dtype)


def _mod_kernel(c_ref, w_ref, b_ref, o_ref):
    s = _silu(c_ref[...]).astype(BF16)
    o_ref[...] = _dot(s, w_ref[...].astype(BF16)) + b_ref[...]


def _mod_call(c_rows, w_mod, b_mod, n):
    r, d = c_rows.shape
    tn = _tile(n, 512)
    return pl.pallas_call(
        _mod_kernel,
        grid=(n // tn,),
        in_specs=[pl.BlockSpec((r, d), lambda j: (0, 0)),
                  pl.BlockSpec((d, tn), lambda j: (0, j)),
                  pl.BlockSpec((1, tn), lambda j: (0, j))],
        out_specs=pl.BlockSpec((r, tn), lambda j: (0, j)),
        out_shape=jax.ShapeDtypeStruct((r, n), F32),
        compiler_params=_params("parallel"),
        name="mod",
    )(c_rows, w_mod, b_mod.reshape(1, -1))


def _norm_mod_kernel(x_ref, g_ref, scale_ref, shift_ref, o_ref):
    x = x_ref[0]
    y = x * lax.rsqrt(jnp.mean(x * x, axis=-1, keepdims=True) + EPS) * g_ref[...]
    o_ref[0] = (y * (1.0 + scale_ref[0]) + shift_ref[0]).astype(o_ref.dtype)


def _norm_mod_call(x, gain, mod3, row_of_batch, shift_blk, scale_blk):
    b, t, d = x.shape
    tm = _tile(t, 512, SUBLANES)
    return pl.pallas_call(
        _norm_mod_kernel,
        grid=(b, t // tm),
        in_specs=[pl.BlockSpec((1, tm, d), lambda bi, i: (bi, i, 0)),
                  pl.BlockSpec((1, d), lambda bi, i: (0, 0)),
                  pl.BlockSpec((1, 1, d), lambda bi, i: (row_of_batch(bi), 0, scale_blk)),
                  pl.BlockSpec((1, 1, d), lambda bi, i: (row_of_batch(bi), 0, shift_blk))],
        out_specs=pl.BlockSpec((1, tm, d), lambda bi, i: (bi, i, 0)),
        out_shape=jax.ShapeDtypeStruct((b, t, d), BF16),
        compiler_params=_params("parallel", "parallel"),
        name="norm_mod",
    )(x, gain.reshape(1, d), mod3, mod3)


def _final_norm_kernel(x_ref, g_ref, o_ref):
    x = x_ref[0]
    o_ref[0] = x * lax.rsqrt(jnp.mean(x * x, axis=-1, keepdims=True) + EPS) * g_ref[...]


def _final_norm_call(x, gain):
    b, t, d = x.shape
    tm = _tile(t, 512, SUBLANES)
    return pl.pallas_call(
        _final_norm_kernel,
        grid=(b, t // tm),
        in_specs=[pl.BlockSpec((1, tm, d), lambda bi, i: (bi, i, 0)),
                  pl.BlockSpec((1, d), lambda bi, i: (0, 0))],
        out_specs=pl.BlockSpec((1, tm, d), lambda bi, i: (bi, i, 0)),
        out_shape=jax.ShapeDtypeStruct((b, t, d), F32),
        compiler_params=_params("parallel", "parallel"),
        name="final_norm",
    )(x, gain.reshape(1, d))


def _in_proj_tile(period, h_ref, w_hbm, o_ref, stage_s, wb_s, sem):
    j = pl.program_id(0)
    nj = pl.num_programs(0)
    p = pl.program_id(1) * pl.num_programs(2) + pl.program_id(2)
    dk = stage_s.shape[0] // W_IN_SPLIT
    tn = stage_s.shape[1]
    start_step = [(q * period) // W_IN_SPLIT for q in range(W_IN_SPLIT)]

    def slice_copy(tile, q):
        col = pl.multiple_of(tile * tn, tn)
        return pltpu.make_async_copy(w_hbm.at[pl.ds(q * dk, dk), pl.ds(col, tn)],
                                     stage_s.at[pl.ds(q * dk, dk), :], sem.at[q])

    def start_next(q):
        @pl.when(j + 1 < nj)
        def _():
            slice_copy(j + 1, q).start()

    @pl.when(p == 0)
    def _():
        @pl.when(j == 0)
        def _():
            for q in range(W_IN_SPLIT):
                slice_copy(0, q).start()

        for q in range(W_IN_SPLIT):
            slice_copy(j, q).wait()
            wb_s[q * dk:(q + 1) * dk, :] = stage_s[q * dk:(q + 1) * dk, :].astype(BF16)
            if start_step[q] == 0:
                start_next(q)

    for q in range(W_IN_SPLIT):
        if start_step[q] > 0:
            @pl.when(p == start_step[q])
            def _():
                start_next(q)

    acc = _dot(h_ref[0], wb_s[...])
    for jj in range(o_ref.shape[1]):
        o_ref[0, jj] = acc[:, jj * LANES:(jj + 1) * LANES]


def _in_proj_kernel(period, h_ref, w_hbm, o_ref, stage_s, wb_s, sem):
    _in_proj_tile(period, h_ref, w_hbm, o_ref, stage_s, wb_s, sem)


def _in_proj_cast_kernel(period, n_cast, h_ref, w_hbm, c_ref, o_ref, co_ref, stage_s, wb_s, sem):
    _in_proj_tile(period, h_ref, w_hbm, o_ref, stage_s, wb_s, sem)
    _cast_block(c_ref, co_ref, _grid_step(3), n_cast)


def _in_proj_call(h, w, n, tm_target, cast_src=None):
    b, t, d = h.shape
    tm = _tile(t, tm_target, SUBLANES)
    tn = _tile(n, 1024)
    grid = (n // tn, b, t // tm)
    period = b * (t // tm)
    assert period >= 2 and d % W_IN_SPLIT == 0
    in_specs = [pl.BlockSpec((1, tm, d), lambda j, bi, i: (bi, i, 0)),
                pl.BlockSpec(memory_space=pl.ANY)]
    out_specs = pl.BlockSpec((1, tn // LANES, tm, LANES), lambda j, bi, i: (bi, j, i, 0))
    out_shape = jax.ShapeDtypeStruct((b, n // LANES, t, LANES), F32)
    common = dict(grid=grid,
                  scratch_shapes=[pltpu.VMEM((d, tn), F32), pltpu.VMEM((d, tn), BF16),
                                  pltpu.SemaphoreType.DMA((W_IN_SPLIT,))],
                  compiler_params=_params("arbitrary", "arbitrary", "arbitrary"),
                  name="in_proj")
    if cast_src is None:
        return pl.pallas_call(functools.partial(_in_proj_kernel, period), in_specs=in_specs, out_specs=out_specs,
                              out_shape=out_shape, **common)(h, w)
    n_cast, c_in, c_out, c_shape = _cast_specs(cast_src, grid)
    return pl.pallas_call(functools.partial(_in_proj_cast_kernel, period, n_cast), in_specs=in_specs + [c_in],
                          out_specs=[out_specs, c_out], out_shape=[out_shape, c_shape],
                          **common)(h, w, cast_src)


def _cumsum_rows(x, reverse):
    n = x.shape[0]
    row = lax.broadcasted_iota(jnp.int32, x.shape, 0)
    d = 1
    while d < n:
        if reverse:
            x = x + jnp.where(row < n - d, pltpu.roll(x, n - d, axis=0), 0.0)
        else:
            x = x + jnp.where(row >= d, pltpu.roll(x, d, axis=0), 0.0)
        d *= 2
    return x


def _block_row_bcast(x, block, offset):
    n, w = x.shape
    if block >= SUBLANES:
        xb = x.reshape(n // block, block, w)
        return jnp.broadcast_to(xb[:, offset:offset + 1, :], xb.shape).reshape(n, w)
    x3 = x.reshape(n // SUBLANES, SUBLANES, w)
    sub = lax.broadcasted_iota(jnp.int32, x3.shape, 1)
    out = None
    for first in range(0, SUBLANES, block):
        src = jnp.broadcast_to(x3[:, first + offset:first + offset + 1, :], x3.shape)
        out = src if out is None else jnp.where(sub >= first, src, out)
    return out.reshape(n, w)


def _hgrn_gates(z, lb):
    f = lb + (1.0 - lb) * jax.nn.sigmoid(z)
    return 1.0 - f, jnp.log2(f)


def _hgrn_level_halves(q, kf, kb, cf, cb, h):
    n = q.shape[0]
    qm, km = [], []
    for lo in range(0, n, 2 * h):
        mid, hi = lo + h, lo + 2 * h
        bf = jnp.broadcast_to(cf[mid - 1:mid, :], (h, LANES))
        bb = jnp.broadcast_to(cb[mid:mid + 1, :], (h, LANES))
        qm += [q[lo:mid] * jnp.exp2(cb[lo:mid] - bb), q[mid:hi] * jnp.exp2(cf[mid:hi] - bf)]
        km += [kf[lo:mid] * jnp.exp2(bf - cf[lo:mid]), kb[mid:hi] * jnp.exp2(bb - cb[mid:hi])]
    return jnp.concatenate(qm, axis=0), jnp.concatenate(km, axis=0)


def _state_update(s_ref, k, cum, v, reverse):
    n = cum.shape[0]
    edge = cum[0:1, :] if reverse else cum[n - 1:n, :]
    k_dec = (k * jnp.exp2(edge - cum)).astype(BF16)
    s_ref[...] = s_ref[...] * jnp.exp2(edge) + _dot_tn(v.astype(BF16), k_dec)


def _hgrn_kernel(n_cast, i_ref, ff_ref, fb_ref, q_ref, g_ref, ci_ref, cff_ref, cfb_ref, lbl_ref, ng_ref, c_ref,
                 y_ref, co_ref, kf_s, kb_s, cf_s, cb_s, st_s, sf_s, sb_s):
    _cast_block(c_ref, co_ref, _grid_step(2), n_cast)
    t_len, tc_len = i_ref.shape[2], ci_ref.shape[2]
    ch = HG_CHUNK
    cc = min(ch, tc_len)
    n, nc = t_len // ch, tc_len // cc

    l0, l1 = lbl_ref[0], lbl_ref[1]
    mx = jnp.maximum(l0, l1)
    e0, e1 = jnp.exp(l0 - mx), jnp.exp(l1 - mx)
    lb = e0 / (e0 + e1)
    lb_f, lb_b = lb[0:1], lb[1:2]

    sf_s[...] = jnp.zeros_like(sf_s)
    sb_s[...] = jnp.zeros_like(sb_s)

    def ctx_body(c, carry):
        rf = pl.multiple_of(c * cc, cc)
        rb = pl.multiple_of((nc - 1 - c) * cc, cc)
        k, lf = _hgrn_gates(cff_ref[0, 0, pl.ds(rf, cc), :], lb_f)
        _state_update(sf_s, k, _cumsum_rows(lf, False), ci_ref[0, 0, pl.ds(rf, cc), :], False)
        k, lf = _hgrn_gates(cfb_ref[0, 0, pl.ds(rb, cc), :], lb_b)
        _state_update(sb_s, k, _cumsum_rows(lf, True), ci_ref[0, 0, pl.ds(rb, cc), :], True)
        return carry

    lax.fori_loop(0, nc, ctx_body, 0)

    def state_body(c, carry):
        cb = n - 1 - c
        rf = pl.multiple_of(c * ch, ch)
        rb = pl.multiple_of(cb * ch, ch)
        k, lf = _hgrn_gates(ff_ref[0, 0, pl.ds(rf, ch), :], lb_f)
        cum = _cumsum_rows(lf, False)
        kf_s[pl.ds(rf, ch), :] = k
        cf_s[pl.ds(rf, ch), :] = cum
        st_s[c, :, 0:LANES] = sf_s[...].astype(BF16)
        _state_update(sf_s, k, cum, i_ref[0, 0, pl.ds(rf, ch), :], False)
        k, lf = _hgrn_gates(fb_ref[0, 0, pl.ds(rb, ch), :], lb_b)
        cum = _cumsum_rows(lf, True)
        kb_s[pl.ds(rb, ch), :] = k
        cb_s[pl.ds(rb, ch), :] = cum
        st_s[cb, :, LANES:2 * LANES] = sb_s[...].astype(BF16)
        _state_update(sb_s, k, cum, i_ref[0, 0, pl.ds(rb, ch), :], True)
        return carry

    lax.fori_loop(0, n, state_body, 0, unroll=HG_STATE_UNROLL)

    pair_xor = (lax.broadcasted_iota(jnp.int32, (ch, ch), 0) ^ lax.broadcasted_iota(jnp.int32, (ch, ch), 1))
    row = lax.broadcasted_iota(jnp.int32, (ch, LANES), 0)
    q_scale = LANES ** -0.5

    def out_body(c, carry):
        r = pl.multiple_of(c * ch, ch)
        q = _silu(q_ref[0, 0, pl.ds(r, ch), :]) * q_scale
        kf, kb = kf_s[pl.ds(r, ch), :], kb_s[pl.ds(r, ch), :]
        cf, cb = cf_s[pl.ds(r, ch), :], cb_s[pl.ds(r, ch), :]
        v = i_ref[0, 0, pl.ds(r, ch), :].astype(BF16)
        a = jnp.where(pair_xor == 0, _dot_nt(q.astype(BF16), (kf + kb).astype(BF16)), 0.0)
        odd = (row & 1) != 0
        qm = q * jnp.where(odd, 1.0 - kf, 1.0 - kb)
        a = jnp.where(pair_xor == 1, _dot_nt(qm.astype(BF16), jnp.where(odd, kb, kf).astype(BF16)), a)
        h, log_h = 2, 1
        while h < ch:
            if h >= SUBLANES:
                qm, km = _hgrn_level_halves(q, kf, kb, cf, cb, h)
            else:
                second = (row & h) != 0
                df = cf - _block_row_bcast(cf, 2 * h, h - 1)
                db = cb - _block_row_bcast(cb, 2 * h, h)
                qm = q * jnp.exp2(jnp.where(second, df, db))
                km = jnp.where(second, kb, kf) * jnp.exp2(-jnp.where(second, db, df))
            a = jnp.where((pair_xor >> log_h) == 1, _dot_nt(qm.astype(BF16), km.astype(BF16)), a)
            h, log_h = 2 * h, log_h + 1
        q_in = jnp.concatenate([q * jnp.exp2(cf), q * jnp.exp2(cb)], axis=1).astype(BF16)
        o = _dot(a.astype(BF16), v) + _dot_nt(q_in, st_s[c])
        y = o * lax.rsqrt(jnp.mean(o * o, axis=-1, keepdims=True) + EPS) * ng_ref[...]
        y_ref[0, pl.ds(r, ch), :] = (y * _silu(g_ref[0, 0, pl.ds(r, ch), :])).astype(y_ref.dtype)
        return carry

    lax.fori_loop(0, n, out_body, 0, unroll=HG_OUT_UNROLL)


def _hgrn_call(p_x, p_c, lb_logits, norm_g, heads, cast_src):
    b, _, t, _ = p_x.shape
    tc = p_c.shape[2]
    grid = (b, heads)
    n_cast, c_in, c_out, c_shape = _cast_specs(cast_src, grid)

    def col(off):
        return pl.BlockSpec((1, 1, t, LANES), lambda bi, h: (bi, off * heads + h, 0, 0))

    def ctx_col(off):
        return pl.BlockSpec((1, 1, tc, LANES), lambda bi, h: (bi, off * heads + h, 0, 0))

    return pl.pallas_call(
        functools.partial(_hgrn_kernel, n_cast),
        grid=grid,
        in_specs=[col(0), col(1), col(2), col(5), col(6), ctx_col(0), ctx_col(1), ctx_col(2),
                  pl.BlockSpec((2, 2, LANES), lambda bi, h: (0, 0, h)),
                  pl.BlockSpec((1, LANES), lambda bi, h: (0, 0)),
                  c_in],
        out_specs=[pl.BlockSpec((1, t, LANES), lambda bi, h: (bi, 0, h)), c_out],
        out_shape=[jax.ShapeDtypeStruct((b, t, heads * LANES), BF16), c_shape],
        scratch_shapes=[pltpu.VMEM((t, LANES), F32), pltpu.VMEM((t, LANES), F32),
                        pltpu.VMEM((t, LANES), F32), pltpu.VMEM((t, LANES), F32),
                        pltpu.VMEM((t // HG_CHUNK, LANES, 2 * LANES), BF16),
                        pltpu.VMEM((LANES, LANES), F32), pltpu.VMEM((LANES, LANES), F32)],
        compiler_params=_params("arbitrary", "arbitrary"),
        name="hgrn",
    )(p_x, p_x, p_x, p_x, p_x, p_c, p_c, p_c, lb_logits, norm_g.reshape(1, LANES), cast_src)


def _rope(a, cos, sin_signed, swap=None):
    if swap is None:
        lane = lax.broadcasted_iota(jnp.int32, a.shape, 1)
        partner = jnp.where((lane & 1) == 0, pltpu.roll(a, LANES - 1, axis=1), pltpu.roll(a, 1, axis=1))
    else:
        partner = _dot(a.astype(BF16), swap)
    return a * cos + partner * sin_signed


def _ret_state_update(s_ref, k, v, k_decay, s_decay):
    s_ref[...] = s_ref[...] * s_decay + _dot_tn(v.astype(BF16), (k * k_decay).astype(BF16))


def _ret_kernel(n_cast, q_ref, k_ref, v_ref, g_ref, ck_ref, cv_ref, dl_ref, cos_ref, sin_ref, sw_ref, c1_ref, c2_ref,
                mc_ref, mw_ref, mb_ref, y_ref, co1_ref, co2_ref, mo_ref, kr_s, st_s, sf_s, sb_s):
    _cast_block(c1_ref, co1_ref, _grid_step(2), n_cast)
    _cast_block(c2_ref, co2_ref, _grid_step(2), n_cast)
    _mod_kernel(mc_ref, mw_ref, mb_ref, mo_ref)
    t_len, tc_len = q_ref.shape[2], ck_ref.shape[2]
    ch = RET_CHUNK
    cc = min(ch, tc_len)
    n, nc = t_len // ch, tc_len // cc
    k_scale = LANES ** -0.5

    dl = dl_ref[0]
    gam = jnp.minimum(dl, 0.0) - jnp.log1p(jnp.exp(-jnp.abs(dl)))
    gam_f, gam_b = gam[0:1], gam[1:2]

    def decays(m):
        pos = lax.broadcasted_iota(jnp.int32, (m, LANES), 0).astype(F32)
        return dict(kf=jnp.exp(gam_f * (m - 1.0 - pos)), kb=jnp.exp(gam_b * pos),
                    qf=jnp.exp(gam_f * (pos + 1.0)), qb=jnp.exp(gam_b * (m - pos)),
                    sf=jnp.exp(gam_f * m), sb=jnp.exp(gam_b * m))

    sf_s[...] = jnp.zeros_like(sf_s)
    sb_s[...] = jnp.zeros_like(sb_s)

    dc = decays(cc)

    def ctx_body(c, carry):
        rf = pl.multiple_of(c * cc, cc)
        rb = pl.multiple_of((nc - 1 - c) * cc, cc)
        _ret_state_update(sf_s, ck_ref[0, 0, pl.ds(rf, cc), :] * k_scale, cv_ref[0, 0, pl.ds(rf, cc), :],
                          dc["kf"], dc["sf"])
        _ret_state_update(sb_s, ck_ref[0, 0, pl.ds(rb, cc), :] * k_scale, cv_ref[0, 0, pl.ds(rb, cc), :],
                          dc["kb"], dc["sb"])
        return carry

    lax.fori_loop(0, nc, ctx_body, 0)

    def rope_body(c, carry):
        r = pl.multiple_of(c * ch, ch)
        kr_s[pl.ds(r, ch), :] = _rope(k_ref[0, 0, pl.ds(r, ch), :] * k_scale,
                                      cos_ref[pl.ds(r, ch), :], sin_ref[pl.ds(r, ch), :], sw_ref[...])
        return carry

    lax.fori_loop(0, n, rope_body, 0, unroll=RET_UNROLL)

    dd = decays(ch)

    def state_body(c, carry):
        cb = n - 1 - c
        rf = pl.multiple_of(c * ch, ch)
        rb = pl.multiple_of(cb * ch, ch)
        st_s[c, :, 0:LANES] = sf_s[...].astype(BF16)
        _ret_state_update(sf_s, kr_s[pl.ds(rf, ch), :], v_ref[0, 0, pl.ds(rf, ch), :], dd["kf"], dd["sf"])
        st_s[cb, :, LANES:2 * LANES] = sb_s[...].astype(BF16)
        _ret_state_update(sb_s, kr_s[pl.ds(rb, ch), :], v_ref[0, 0, pl.ds(rb, ch), :], dd["kb"], dd["sb"])
        return carry

    lax.fori_loop(0, n, state_body, 0, unroll=RET_UNROLL)

    ti = lax.broadcasted_iota(jnp.int32, (ch, ch), 0)
    si = lax.broadcasted_iota(jnp.int32, (ch, ch), 1)
    lag = (ti - si).astype(F32)
    g_f = jnp.broadcast_to(gam_f[:, 0:1], (ch, ch))
    g_b = jnp.broadcast_to(gam_b[:, 0:1], (ch, ch))
    decay = jnp.where(ti > si, jnp.exp(g_f * jnp.maximum(lag, 0.0)),
                      jnp.where(ti < si, jnp.exp(g_b * jnp.maximum(-lag, 0.0)), 2.0))

    def out_body(c, carry):
        r = pl.multiple_of(c * ch, ch)
        q = _rope(q_ref[0, 0, pl.ds(r, ch), :], cos_ref[pl.ds(r, ch), :], sin_ref[pl.ds(r, ch), :])
        k = kr_s[pl.ds(r, ch), :].astype(BF16)
        v = v_ref[0, 0, pl.ds(r, ch), :].astype(BF16)
        a = _dot_nt(q.astype(BF16), k) * decay
        q_in = jnp.concatenate([q * dd["qf"], q * dd["qb"]], axis=1).astype(BF16)
        o = _dot(a.astype(BF16), v) + _dot_nt(q_in, st_s[c])
        oc = o - jnp.mean(o, axis=-1, keepdims=True)
        y = oc * lax.rsqrt(jnp.mean(oc * oc, axis=-1, keepdims=True) + EPS)
        y_ref[0, pl.ds(r, ch), :] = (y * _silu(g_ref[0, 0, pl.ds(r, ch), :])).astype(y_ref.dtype)
        return carry

    lax.fori_loop(0, n, out_body, 0, unroll=RET_OUT_UNROLL)


def _ret_call(p_x, p_c, decay_logit, cos, sin_signed, heads, cast_src1, cast_src2, c_rows, w_mod, b_mod, mod_col0):
    b, _, t, _ = p_x.shape
    tc = p_c.shape[2]
    grid = (b, heads)
    r, d = c_rows.shape
    mod_cols = w_mod.shape[1] - mod_col0
    mod_tn = mod_cols // (b * heads)
    assert mod_tn % LANES == 0 and mod_col0 % mod_tn == 0
    mod_blk0 = mod_col0 // mod_tn
    assert cast_src1.shape == cast_src2.shape
    n_cast, c_in, c_out, c_shape = _cast_specs(cast_src1, grid)
    swap = jnp.asarray(np.eye(LANES, dtype=np.float32)[:, np.arange(LANES) ^ 1], BF16)

    def col(off):
        return pl.BlockSpec((1, 1, t, LANES), lambda bi, h: (bi, off * heads + h, 0, 0))

    def ctx_col(off):
        return pl.BlockSpec((1, 1, tc, LANES), lambda bi, h: (bi, off * heads + h, 0, 0))

    return pl.pallas_call(
        functools.partial(_ret_kernel, n_cast),
        grid=grid,
        in_specs=[col(7), col(3), col(4), col(8), ctx_col(3), ctx_col(4),
                  pl.BlockSpec((1, 2, LANES), lambda bi, h: (h, 0, 0)),
                  pl.BlockSpec((t, LANES), lambda bi, h: (0, 0)),
                  pl.BlockSpec((t, LANES), lambda bi, h: (0, 0)),
                  pl.BlockSpec((LANES, LANES), lambda bi, h: (0, 0)),
                  c_in, c_in,
                  pl.BlockSpec((r, d), lambda bi, h: (0, 0)),
                  pl.BlockSpec((d, mod_tn), lambda bi, h: (0, mod_blk0 + bi * heads + h)),
                  pl.BlockSpec((1, mod_tn), lambda bi, h: (0, mod_blk0 + bi * heads + h))],
        out_specs=[pl.BlockSpec((1, t, LANES), lambda bi, h: (bi, 0, h)), c_out, c_out,
                   pl.BlockSpec((r, mod_tn), lambda bi, h: (0, bi * heads + h))],
        out_shape=[jax.ShapeDtypeStruct((b, t, heads * LANES), BF16), c_shape, c_shape,
                   jax.ShapeDtypeStruct((r, mod_cols), F32)],
        scratch_shapes=[pltpu.VMEM((t, LANES), F32),
                        pltpu.VMEM((t // RET_CHUNK, LANES, 2 * LANES), BF16),
                        pltpu.VMEM((LANES, LANES), F32), pltpu.VMEM((LANES, LANES), F32)],
        compiler_params=_params("arbitrary", "arbitrary"),
        name="ret",
    )(p_x, p_x, p_x, p_x, p_c, p_c, decay_logit, cos, sin_signed, swap, cast_src1, cast_src2,
      c_rows, w_mod, b_mod.reshape(1, -1))


def _merge_kernel(yh_ref, yr_ref, wh_ref, wr_ref, gh_ref, gr_ref, o_ref):
    acc_h = _dot(yh_ref[0], wh_ref[...])
    acc_r = _dot(yr_ref[0], wr_ref[...])
    for j in range(gh_ref.shape[1]):
        sl = slice(j * LANES, (j + 1) * LANES)
        o_ref[0, :, sl] = (jax.nn.sigmoid(gh_ref[0, j]) * acc_h[:, sl]
                           + jax.nn.sigmoid(gr_ref[0, j]) * acc_r[:, sl]).astype(o_ref.dtype)


def _merge_call(y_hg, y_ret, w_bh, w_br, p_x, gate_h_blk, gate_r_blk):
    b, t, kw = y_hg.shape
    d = w_bh.shape[1]
    tm = _tile(t, 1024, SUBLANES)
    tn = _tile(d, 512)
    nb = tn // LANES
    return pl.pallas_call(
        _merge_kernel,
        grid=(b, t // tm, d // tn),
        in_specs=[pl.BlockSpec((1, tm, kw), lambda bi, i, j: (bi, i, 0)),
                  pl.BlockSpec((1, tm, kw), lambda bi, i, j: (bi, i, 0)),
                  pl.BlockSpec((kw, tn), lambda bi, i, j: (0, j)),
                  pl.BlockSpec((kw, tn), lambda bi, i, j: (0, j)),
                  pl.BlockSpec((1, nb, tm, LANES), lambda bi, i, j: (bi, gate_h_blk // nb + j, i, 0)),
                  pl.BlockSpec((1, nb, tm, LANES), lambda bi, i, j: (bi, gate_r_blk // nb + j, i, 0))],
        out_specs=pl.BlockSpec((1, tm, tn), lambda bi, i, j: (bi, i, j)),
        out_shape=jax.ShapeDtypeStruct((b, t, d), BF16),
        compiler_params=_params("parallel", "parallel", "arbitrary"),
        name="merge",
    )(y_hg, y_ret, w_bh, w_br, p_x, p_x)


def _gated_residual_kernel(a_ref, w_ref, x_ref, gate_ref, o_ref):
    o_ref[0] = x_ref[0] + gate_ref[0] * _dot(a_ref[0], w_ref[...])


def _out_proj_call(a, w, x, mod3, gate_blk_of):
    b, t, kw = a.shape
    d = w.shape[1]
    tm = _tile(t, 1024, SUBLANES)
    tn = _tile(d, 1024)
    return pl.pallas_call(
        _gated_residual_kernel,
        grid=(b, t // tm, d // tn),
        in_specs=[pl.BlockSpec((1, tm, kw), lambda bi, i, j: (bi, i, 0)),
                  pl.BlockSpec((kw, tn), lambda bi, i, j: (0, j)),
                  pl.BlockSpec((1, tm, tn), lambda bi, i, j: (bi, i, j)),
                  pl.BlockSpec((1, 1, tn), lambda bi, i, j: (bi, 0, gate_blk_of(tn) + j))],
        out_specs=pl.BlockSpec((1, tm, tn), lambda bi, i, j: (bi, i, j)),
        out_shape=jax.ShapeDtypeStruct((b, t, d), F32),
        compiler_params=_params("parallel", "parallel", "arbitrary"),
        name="out_proj",
    )(a, w, x, mod3)


def _ff1_kernel(n_cast, h_ref, w_ref, c_ref, o_ref, co_ref):
    z = jnp.maximum(_dot(h_ref[0], w_ref[...]), 0.0)
    o_ref[0] = (z * z).astype(o_ref.dtype)
    _cast_block(c_ref, co_ref, _grid_step(3), n_cast)


def _ff1_call(h, w, cast_src):
    b, t, d = h.shape
    n = w.shape[1]
    tm = _tile(t, 1024, SUBLANES)
    tn = _tile(n, 1024)
    grid = (b, t // tm, n // tn)
    n_cast, c_in, c_out, c_shape = _cast_specs(cast_src, grid)
    return pl.pallas_call(
        functools.partial(_ff1_kernel, n_cast),
        grid=grid,
        in_specs=[pl.BlockSpec((1, tm, d), lambda bi, i, j: (bi, i, 0)),
                  pl.BlockSpec((d, tn), lambda bi, i, j: (0, j)),
                  c_in],
        out_specs=[pl.BlockSpec((1, tm, tn), lambda bi, i, j: (bi, i, j)), c_out],
        out_shape=[jax.ShapeDtypeStruct((b, t, n), BF16), c_shape],
        compiler_params=_params("arbitrary", "arbitrary", "arbitrary"),
        name="ff1",
    )(h, w, cast_src)


def _ff2_kernel(a_ref, w_ref, x_ref, gate_ref, o_ref):
    k = pl.program_id(3)
    sb = _sub_block(o_ref.shape[2])
    cols = [slice(c, c + sb) for c in range(0, o_ref.shape[2], sb)]

    @pl.when(k == 0)
    def _():
        for c in cols:
            o_ref[0, :, c] = x_ref[0, :, c] + gate_ref[0, :, c] * _dot(a_ref[0], w_ref[:, c])

    @pl.when(k > 0)
    def _():
        for c in cols:
            o_ref[0, :, c] += gate_ref[0, :, c] * _dot(a_ref[0], w_ref[:, c])


def _ff2_call(a, w, x, mod3, gate_blk_of):
    b, t, kw = a.shape
    d = w.shape[1]
    tm = _tile(t, 1024, SUBLANES)
    tn = _tile(d, 1024)
    tk = _tile(kw, 4096)
    return pl.pallas_call(
        _ff2_kernel,
        grid=(b, t // tm, d // tn, kw // tk),
        in_specs=[pl.BlockSpec((1, tm, tk), lambda bi, i, j, k: (bi, i, k)),
                  pl.BlockSpec((tk, tn), lambda bi, i, j, k: (k, j)),
                  pl.BlockSpec((1, tm, tn), lambda bi, i, j, k: (bi, i, j)),
                  pl.BlockSpec((1, 1, tn), lambda bi, i, j, k: (bi, 0, gate_blk_of(tn) + j))],
        out_specs=pl.BlockSpec((1, tm, tn), lambda bi, i, j, k: (bi, i, j)),
        out_shape=jax.ShapeDtypeStruct((b, t, d), F32),
        compiler_params=_params("parallel", "parallel", "parallel", "arbitrary"),
        name="ff2",
    )(a, w, x, mod3)


def _rope_tables(t_len):
    pos = jnp.arange(t_len)
    row = (pos // GRID_W).astype(F32)
    col = (pos % GRID_W).astype(F32)
    n_freq = LANES // 4
    inv_freq = ROPE_BASE ** (-jnp.arange(n_freq, dtype=F32) / n_freq)
    ang = jnp.concatenate([row[:, None] * inv_freq, col[:, None] * inv_freq], axis=-1)
    cos, sin = jnp.cos(ang), jnp.sin(ang)
    cos_rep = jnp.repeat(cos, 2, axis=-1)
    sin_signed = jnp.stack([-sin, sin], axis=-1).reshape(t_len, LANES)
    return cos_rep, sin_signed


def kernel(x, c, ctx, c_ctx, w_mod, b_mod, norm1_g, norm2_g, w_in, hg_lb_logits, hg_norm_g, ret_decay_logit,
           w_branch_hgrn, w_branch_ret, w_out, w_ff1, w_ff2, final_norm_g):
    b, t, d = x.shape
    assert w_mod.shape[0] == 1 and hg_lb_logits.shape[0] == 2, "one layer"
    assert b + 1 <= SUBLANES
    hw = w_branch_hgrn.shape[1]
    heads = hw // LANES
    assert w_branch_ret.shape[1] == hw and w_in.shape[2] == 9 * hw + 2 * d
    n_state = 5 * hw

    c_rows = jnp.zeros((SUBLANES, d), F32).at[:b].set(c).at[b].set(c_ctx)
    mod_a = _mod_call(c_rows, w_mod[0], b_mod[0], 2 * d).reshape(SUBLANES, 1, 2 * d)

    h_x = _norm_mod_call(x, norm1_g[0], mod_a, lambda bi: bi, 0, 1)
    h_c = _norm_mod_call(ctx, norm1_g[0], mod_a, lambda bi: b, 0, 1)
    p_x, w_ff1_b = _in_proj_call(h_x, w_in[0], w_in.shape[2], 1024, cast_src=w_ff1[0])
    p_c = _in_proj_call(h_c, w_in[0], n_state, 256)

    y_hg, w_out_b = _hgrn_call(p_x, p_c, hg_lb_logits, hg_norm_g[0], heads, w_out[0])
    cos_rep, sin_signed = _rope_tables(t)
    decay_logit = jnp.broadcast_to(ret_decay_logit[0].T[:, :, None], (heads, 2, LANES))
    y_ret, w_bh_b, w_br_b, mod_b = _ret_call(p_x, p_c, decay_logit, cos_rep, sin_signed, heads,
                                             w_branch_hgrn[0], w_branch_ret[0], c_rows, w_mod[0], b_mod[0], 2 * d)
    mod_b = mod_b.reshape(SUBLANES, 1, 4 * d)

    merged = _merge_call(y_hg, y_ret, w_bh_b, w_br_b, p_x, 9 * heads, 9 * heads + d // LANES)
    x1 = _out_proj_call(merged, w_out_b, x, mod_b, lambda tn: 0)
    h2 = _norm_mod_call(x1, norm2_g[0], mod_b, lambda bi: bi, 1, 2)
    act, w_ff2_b = _ff1_call(h2, w_ff1_b, w_ff2[0])
    x2 = _ff2_call(act, w_ff2_b, x1, mod_b, lambda tn: 3 * d // tn)
    return _final_norm_call(x2, final_norm_g)
```
